```python
import functools
import jax, jax.numpy as jnp
from jax import lax
import numpy as np

D_MODEL = 2048
BATCH = 2
SEQ = 4096
DEPTH = 2
DEC_BATCH = 4
DEC_SEQ = 2048
PAST_LEN = 128

HEAD_DIM = 128
D_FOURIER = D_MODEL // 4
N_FOURIER_GROUPS = 4
FOURIER_GROUP = D_FOURIER // N_FOURIER_GROUPS
D_ATTN = D_MODEL - D_FOURIER
N_ATTN_HEADS = D_ATTN // HEAD_DIM
D_MIX = D_FOURIER + D_ATTN
D_IN_PROJ = D_FOURIER + 3 * D_ATTN
DILATED_BRANCHES = ((128, 1), (512, 4), (2048, 16))
D_FF_DENSE = 5632
N_EXPERTS = 8
TOP_K = 2
D_FF_EXPERT = 7168
N_DENSE = (DEPTH + 1) // 2
N_MOE = DEPTH // 2
EPS = 1e-6
NEG_INF = -1e30
ADA_W_SCALE = 0.3

kernel_name = "hymba_fnet_longnet_alibi_adaln_encoder"


def rmsnorm(x, g):
    xf = x.astype(jnp.float32)
    y = xf * lax.rsqrt(jnp.mean(xf * xf, axis=-1, keepdims=True) + EPS)
    return (y * g.astype(jnp.float32)).astype(x.dtype)


def modulate(h, shift, scale):
    return h * (1 + scale[:, None, :]) + shift[:, None, :]


def alibi_slopes():
    h = jnp.arange(N_ATTN_HEADS, dtype=jnp.float32)
    return jnp.exp2(-8.0 * (h + 1.0) / N_ATTN_HEADS)


def fourier_mix(u, w_fmix):
    B, S, _ = u.shape
    ug = u.astype(jnp.float32).reshape(B, S, N_FOURIER_GROUPS, FOURIER_GROUP)
    f = jnp.fft.fft2(ug, axes=(1, 3), norm="ortho").real
    out = jnp.einsum("bsgc,gcd->bsgd", f, w_fmix.astype(jnp.float32))
    return out.reshape(B, S, D_FOURIER).astype(u.dtype)


def dilated_branch(q, k, v, slopes, window, dilation):
    B, S, H, Dh = q.shape
    n_side = (window // 2) // dilation
    blk = n_side
    L = S // dilation
    G = B * dilation
    nb = -(-L // blk)
    Lp = nb * blk

    def to_sub(t):
        return t.reshape(B, L, dilation, H, Dh).transpose(0, 2, 1, 3, 4).reshape(G, L, H, Dh)

    qb = jnp.pad(to_sub(q), ((0, 0), (0, Lp - L), (0, 0), (0, 0))).reshape(G, nb, blk, H, Dh)

    def key_windows(t):
        tb = jnp.pad(to_sub(t), ((0, 0), (blk, Lp - L + blk), (0, 0), (0, 0))).reshape(G, nb + 2, blk, H, Dh)
        return jnp.concatenate([tb[:, :-2], tb[:, 1:-1], tb[:, 2:]], axis=2)

    kw = key_windows(k)
    vw = key_windows(v)
    s = jnp.einsum("gnqhd,gnkhd->gnhqk", qb, kw, preferred_element_type=jnp.float32)
    qi = jnp.arange(blk)[:, None]
    kj = jnp.arange(3 * blk)[None, :]
    rel = kj - blk - qi
    key_pos = jnp.arange(nb)[:, None, None] * blk + (kj - blk)[None]
    valid = (jnp.abs(rel) <= n_side)[None] & (key_pos >= 0) & (key_pos < L)
    bias = -slopes[:, None, None] * (dilation * jnp.abs(rel)).astype(jnp.float32)[None]
    s = jnp.where(valid[None, :, None], s + bias[None, None], NEG_INF)
    m = jnp.max(s, axis=-1)
    p = jnp.exp(s - m[..., None])
    den = jnp.sum(p, axis=-1)
    num = jnp.einsum("gnhqk,gnkhd->gnqhd", p, vw.astype(jnp.float32))

    def from_sub(t):
        rest = t.shape[3:]
        t = t.reshape((G, Lp) + rest)[:, :L]
        t = t.reshape((B, dilation, L) + rest)
        return jnp.swapaxes(t, 1, 2).reshape((B, S) + rest)

    return (from_sub(num), from_sub(jnp.swapaxes(m, 2, 3)), from_sub(jnp.swapaxes(den, 2, 3)))


def dilated_attention(q, k, v):
    slopes = alibi_slopes()
    outs = [dilated_branch(q, k, v, slopes, w, d) for (w, d) in DILATED_BRANCHES]
    m_all = functools.reduce(jnp.maximum, [o[1] for o in outs])
    num = 0.0
    den = 0.0
    for n_g, m_g, s_g in outs:
        w_g = jnp.exp(m_g - m_all)
        num = num + n_g * w_g[..., None]
        den = den + s_g * w_g
    return num / den[..., None]


def mixer(h, w_in, w_fmix, g_out_f, g_out_a, w_out):
    B, S, _ = h.shape
    z = h @ w_in
    u = z[..., :D_FOURIER]
    q = z[..., D_FOURIER:D_FOURIER + D_ATTN].reshape(B, S, N_ATTN_HEADS, HEAD_DIM) * (HEAD_DIM ** -0.5)
    k = z[..., D_FOURIER + D_ATTN:D_FOURIER + 2 * D_ATTN].reshape(B, S, N_ATTN_HEADS, HEAD_DIM)
    v = z[..., D_FOURIER + 2 * D_ATTN:].reshape(B, S, N_ATTN_HEADS, HEAD_DIM)
    yf = fourier_mix(u, w_fmix)
    ya = dilated_attention(q, k, v).reshape(B, S, D_ATTN).astype(h.dtype)
    y = jnp.concatenate([rmsnorm(yf, g_out_f), rmsnorm(ya, g_out_a)], axis=-1)
    return y @ w_out


def swiglu(h, w1, w3, w2):
    return (jax.nn.silu(h @ w1) * (h @ w3)) @ w2


def moe_swiglu(h, w_router, w1, w3, w2):
    B, S, D = h.shape
    t = h.reshape(B * S, D)
    logits = (t @ w_router).astype(jnp.float32)
    top_val, top_idx = lax.top_k(logits, TOP_K)
    gates_top = jax.nn.softmax(top_val, axis=-1)
    gates = jnp.sum(jax.nn.one_hot(top_idx, N_EXPERTS, dtype=jnp.float32) * gates_top[..., None], axis=1)
    y = jnp.zeros((B * S, D), jnp.float32)
    for e in range(N_EXPERTS):
        y = y + gates[:, e:e + 1] * swiglu(t, w1[e], w3[e], w2[e]).astype(jnp.float32)
    return y.astype(h.dtype).reshape(B, S, D)


def trunk(x, c, w_ada, b_ada, g_norm_mix, g_norm_ff, w_in, w_fmix, g_out_f, g_out_a, w_out,
          w_ff1, w_ff3, w_ff2, w_router, w_e1, w_e3, w_e2, g_final):
    for l in range(DEPTH):
        mod = jax.nn.silu(c) @ w_ada[l] + b_ada[l]
        sh1, sc1, g1, sh2, sc2, g2 = jnp.split(mod, 6, axis=-1)
        h = modulate(rmsnorm(x, g_norm_mix[l]), sh1, sc1)
        x = x + g1[:, None, :] * mixer(h, w_in[l], w_fmix[l], g_out_f[l], g_out_a[l], w_out[l])
        h = modulate(rmsnorm(x, g_norm_ff[l]), sh2, sc2)
        if l % 2 == 0:
            j = l // 2
            f = swiglu(h, w_ff1[j], w_ff3[j], w_ff2[j])
        else:
            j = l // 2
            f = moe_swiglu(h, w_router[j], w_e1[j], w_e3[j], w_e2[j])
        x = x + g2[:, None, :] * f
    return rmsnorm(x, g_final)


def setup_inputs(seed: int = 0) -> dict:
    key = jax.random.key(seed)
    ks = jax.random.split(key, 20)
    f32 = jnp.float32
    D = D_MODEL

    def nrm(k, shape, scale):
        return jax.random.normal(k, shape, f32) * scale

    return {
        "x_prompt": nrm(ks[0], (BATCH, SEQ, D), 1.0),
        "x_sample": nrm(ks[1], (DEC_BATCH, DEC_SEQ, D), 1.0),
        "c_prompt": nrm(ks[2], (BATCH, D), 1.0),
        "c_sample": nrm(ks[3], (DEC_BATCH, D), 1.0),
        "w_ada": nrm(ks[4], (DEPTH, D, 6 * D), ADA_W_SCALE * D ** -0.5),
        "b_ada": nrm(ks[5], (DEPTH, 6 * D), 0.02),
        "g_norm_mix": 1.0 + nrm(ks[6], (DEPTH, D), 0.02),
        "g_norm_ff": 1.0 + nrm(ks[7], (DEPTH, D), 0.02),
        "w_in": nrm(ks[8], (DEPTH, D, D_IN_PROJ), D ** -0.5),
        "w_fmix": nrm(ks[9], (DEPTH, N_FOURIER_GROUPS, FOURIER_GROUP, FOURIER_GROUP), FOURIER_GROUP ** -0.5),
        "g_out_f": 1.0 + nrm(ks[10], (DEPTH, D_FOURIER), 0.02),
        "g_out_a": 1.0 + nrm(ks[11], (DEPTH, D_ATTN), 0.02),
        "w_out": nrm(ks[12], (DEPTH, D_MIX, D), D_MIX ** -0.5),
        "w_ff1": nrm(ks[13], (N_DENSE, D, D_FF_DENSE), D ** -0.5),
        "w_ff3": nrm(ks[14], (N_DENSE, D, D_FF_DENSE), D ** -0.5),
        "w_ff2": nrm(ks[15], (N_DENSE, D_FF_DENSE, D), D_FF_DENSE ** -0.5),
        "w_router": nrm(ks[16], (N_MOE, D, N_EXPERTS), D ** -0.5),
        "w_e1": nrm(ks[17], (N_MOE, N_EXPERTS, D, D_FF_EXPERT), D ** -0.5),
        "w_e3": nrm(ks[18], (N_MOE, N_EXPERTS, D, D_FF_EXPERT), D ** -0.5),
        "w_e2": nrm(ks[19], (N_MOE, N_EXPERTS, D_FF_EXPERT, D), D_FF_EXPERT ** -0.5),
        "g_final": 1.0 + nrm(jax.random.fold_in(key, 99), (D,), 0.02),
    }


def reference(x_prompt, x_sample, c_prompt, c_sample, w_ada, b_ada, g_norm_mix, g_norm_ff,
              w_in, w_fmix, g_out_f, g_out_a, w_out, w_ff1, w_ff3, w_ff2,
              w_router, w_e1, w_e3, w_e2, g_final):
    y_prompt = trunk(x_prompt, c_prompt, w_ada, b_ada, g_norm_mix, g_norm_ff, w_in, w_fmix, g_out_f,
                     g_out_a, w_out, w_ff1, w_ff3, w_ff2, w_router, w_e1, w_e3, w_e2, g_final)
    y_sample = trunk(x_sample, c_sample, w_ada, b_ada, g_norm_mix, g_norm_ff, w_in, w_fmix, g_out_f,
                     g_out_a, w_out, w_ff1, w_ff3, w_ff2, w_router, w_e1, w_e3, w_e2, g_final)
    return (y_prompt, y_sample)
```

```python
import functools
import math

import numpy as np
import jax
import jax.numpy as jnp
from jax import lax
from jax.experimental import pallas as pl
from jax.experimental.pallas import tpu as pltpu

HEAD_DIM = 128
FOURIER_GROUP = 128
N_FOURIER_GROUPS = 4
D_FOURIER = FOURIER_GROUP * N_FOURIER_GROUPS
DILATED_BRANCHES = ((128, 1), (512, 4), (2048, 16))
TOP_K = 2
EPS = 1e-6
NEG_INF = -1e30
LANES = 128
HEADS_PER_STEP = 2
VMEM_LIMIT = 56 * 1024 * 1024

BF16 = jnp.bfloat16
F32 = jnp.float32


def _params(n_axes, vmem=VMEM_LIMIT):
    return pltpu.CompilerParams(
        dimension_semantics=("arbitrary",) * n_axes, vmem_limit_bytes=vmem)


def _ada_kernel(c_ref, w_ref, b_ref, o_ref):
    c = c_ref[...]
    a = (c * jax.nn.sigmoid(c)).astype(BF16)
    acc = jnp.dot(a, w_ref[...].astype(BF16), preferred_element_type=F32)
    o_ref[...] = acc + b_ref[...]


def ada_modulation(c, w_ada, b_ada, tn=1024):
    n_layers, d, n = w_ada.shape
    r8 = c.shape[0]
    return pl.pallas_call(
        _ada_kernel,
        grid=(n_layers, n // tn),
        in_specs=[
            pl.BlockSpec((r8, d), lambda l, j: (0, 0)),
            pl.BlockSpec((None, d, tn), lambda l, j: (l, 0, j)),
            pl.BlockSpec((None, 1, tn), lambda l, j: (l, 0, j)),
        ],
        out_specs=pl.BlockSpec((None, r8, tn), lambda l, j: (l, 0, j)),
        out_shape=jax.ShapeDtypeStruct((n_layers, r8, n), F32),
        compiler_params=_params(2),
        name="ada_modulation",
    )(c, w_ada, b_ada.reshape(n_layers, 1, n))


def _norm_mod(x, g, sc, sh):
    y = x * lax.rsqrt(jnp.mean(x * x, axis=-1, keepdims=True) + EPS)
    return (y * g) * (1.0 + sc) + sh


def _norm_mod_kernel(rows_ref, x_ref, g_ref, sc_ref, sh_ref, h_ref):
    del rows_ref
    h_ref[...] = _norm_mod(x_ref[...], g_ref[...], sc_ref[...], sh_ref[...]).astype(h_ref.dtype)


def _norm_mod_router_kernel(rows_ref, x_ref, g_ref, sc_ref, sh_ref, wr_ref, h_ref, route_ref,
                            *, n_experts):
    del rows_ref
    h = _norm_mod(x_ref[...], g_ref[...], sc_ref[...], sh_ref[...])
    h_ref[...] = h.astype(h_ref.dtype)
    logits = jnp.dot(h, wr_ref[...], preferred_element_type=F32,
                     precision=lax.Precision.HIGHEST)
    lane = lax.broadcasted_iota(jnp.int32, logits.shape, 1)
    logits = jnp.where(lane < n_experts, logits, -jnp.inf)
    m1 = jnp.max(logits, axis=-1, keepdims=True)
    i1 = jnp.min(jnp.where(logits == m1, lane, LANES), axis=-1, keepdims=True)
    rest = jnp.where(lane == i1, -jnp.inf, logits)
    m2 = jnp.max(rest, axis=-1, keepdims=True)
    i2 = jnp.min(jnp.where(rest == m2, lane, LANES), axis=-1, keepdims=True)
    e2 = jnp.exp(m2 - m1)
    den = 1.0 + e2
    route = jnp.where(lane == 0, i1.astype(F32), 0.0)
    route = jnp.where(lane == 1, i2.astype(F32), route)
    route = jnp.where(lane == 2, 1.0 / den, route)
    route = jnp.where(lane == 3, e2 / den, route)
    route_ref[...] = route


def norm_modulate(x, rows, g, mod4, layer_part, *, out_dtype=BF16, w_router=None, tm=512):
    t, d = x.shape
    sh_part, sc_part = layer_part, layer_part + 1
    in_specs = [
        pl.BlockSpec((tm, d), lambda i, rows: (i, 0)),
        pl.BlockSpec((1, d), lambda i, rows: (0, 0)),
        pl.BlockSpec((None, None, 1, d), lambda i, rows: (rows[i], sc_part, 0, 0)),
        pl.BlockSpec((None, None, 1, d), lambda i, rows: (rows[i], sh_part, 0, 0)),
    ]
    h_spec = pl.BlockSpec((tm, d), lambda i, rows: (i, 0))
    h_shape = jax.ShapeDtypeStruct((t, d), out_dtype)
    if w_router is None:
        return pl.pallas_call(
            _norm_mod_kernel,
            grid_spec=pltpu.PrefetchScalarGridSpec(
                num_scalar_prefetch=1, grid=(t // tm,), in_specs=in_specs, out_specs=h_spec),
            out_shape=h_shape,
            compiler_params=_params(1),
            name="norm_modulate",
        )(rows, x, g, mod4, mod4)
    n_experts = w_router.shape[1]
    wr = jnp.pad(w_router, ((0, 0), (0, LANES - n_experts)))
    in_specs.append(pl.BlockSpec((d, LANES), lambda i, rows: (0, 0)))
    return pl.pallas_call(
        functools.partial(_norm_mod_router_kernel, n_experts=n_experts),
        grid_spec=pltpu.PrefetchScalarGridSpec(
            num_scalar_prefetch=1, grid=(t // tm,), in_specs=in_specs,
            out_specs=[h_spec, pl.BlockSpec((tm, LANES), lambda i, rows: (i, 0))]),
        out_shape=[h_shape, jax.ShapeDtypeStruct((t, LANES), F32)],
        compiler_params=_params(1),
        name="norm_modulate_router",
    )(rows, x, g, mod4, mod4, wr)


CAST_ROWS = 512


def _cast_weights_if_changed(te_ref, w_refs, wb_refs):
    i = pl.program_id(1)
    changed = jnp.logical_or(i == 0, te_ref[i] != te_ref[jnp.maximum(i - 1, 0)])

    @pl.when(changed)
    def _():
        k = w_refs[0].shape[0]
        step = math.gcd(k, CAST_ROWS)

        def body(c, carry):
            r0 = pl.multiple_of(c * step, step)
            for w_ref, wb_ref in zip(w_refs, wb_refs):
                wb_ref[pl.ds(r0, step), :] = w_ref[pl.ds(r0, step), :].astype(BF16)
            return carry

        lax.fori_loop(0, k // step, body, 0)


def _mm_plain_kernel(te_ref, rows_ref, a_ref, w_ref, o_ref, wb_ref, *, col_scale):
    del rows_ref
    _cast_weights_if_changed(te_ref, (w_ref,), (wb_ref,))
    acc = jnp.dot(a_ref[...], wb_ref[...], preferred_element_type=F32)
    if col_scale is not None:
        lo, hi, scale = col_scale
        j = pl.program_id(0)
        acc = acc * jnp.where(jnp.logical_and(j >= lo, j < hi), scale, 1.0).astype(F32)
    o_ref[...] = acc.astype(o_ref.dtype)


def _mm_swiglu_kernel(te_ref, rows_ref, a_ref, w1_ref, w3_ref, o_ref, wb1_ref, wb3_ref):
    del rows_ref
    _cast_weights_if_changed(te_ref, (w1_ref, w3_ref), (wb1_ref, wb3_ref))
    a = a_ref[...]
    g = jnp.dot(a, wb1_ref[...], preferred_element_type=F32)
    u = jnp.dot(a, wb3_ref[...], preferred_element_type=F32)
    o_ref[...] = ((g * jax.nn.sigmoid(g)) * u).astype(o_ref.dtype)


def _mm_resid_kernel(te_ref, rows_ref, a_ref, w_ref, x_ref, gate_ref, o_ref, wb_ref):
    del rows_ref
    _cast_weights_if_changed(te_ref, (w_ref,), (wb_ref,))
    acc = jnp.dot(a_ref[...], wb_ref[...], preferred_element_type=F32)
    o_ref[...] = x_ref[...] + gate_ref[...] * acc


def _w_spec(k, tn, w_index):
    return pl.BlockSpec((None, k, tn), lambda j, i, te, rows: (w_index(te[i]), 0, j))


def matmul(a, w, *, w_index, tile_expert, rows, tm, tn, out_dtype, col_scale=None):
    m, k = a.shape
    n = w.shape[-1]
    return pl.pallas_call(
        functools.partial(_mm_plain_kernel, col_scale=col_scale),
        grid_spec=pltpu.PrefetchScalarGridSpec(
            num_scalar_prefetch=2, grid=(n // tn, m // tm),
            in_specs=[pl.BlockSpec((tm, k), lambda j, i, te, rows: (i, 0)),
                      _w_spec(k, tn, w_index)],
            out_specs=pl.BlockSpec((tm, tn), lambda j, i, te, rows: (i, j)),
            scratch_shapes=[pltpu.VMEM((k, tn), BF16)]),
        out_shape=jax.ShapeDtypeStruct((m, n), out_dtype),
        compiler_params=_params(2),
        name="matmul",
    )(tile_expert, rows, a, w)


def matmul_swiglu(a, w1, w3, *, w_index, tile_expert, rows, tm, tn):
    m, k = a.shape
    n = w1.shape[-1]
    return pl.pallas_call(
        _mm_swiglu_kernel,
        grid_spec=pltpu.PrefetchScalarGridSpec(
            num_scalar_prefetch=2, grid=(n // tn, m // tm),
            in_specs=[pl.BlockSpec((tm, k), lambda j, i, te, rows: (i, 0)),
                      _w_spec(k, tn, w_index), _w_spec(k, tn, w_index)],
            out_specs=pl.BlockSpec((tm, tn), lambda j, i, te, rows: (i, j)),
            scratch_shapes=[pltpu.VMEM((k, tn), BF16), pltpu.VMEM((k, tn), BF16)]),
        out_shape=jax.ShapeDtypeStruct((m, n), BF16),
        compiler_params=_params(2),
        name="matmul_swiglu",
    )(tile_expert, rows, a, w1, w3)


def matmul_gated_residual(a, w, x, mod4, gate_part, *, w_index, tile_expert, rows, tm, tn):
    m, k = a.shape
    n = w.shape[-1]
    return pl.pallas_call(
        _mm_resid_kernel,
        grid_spec=pltpu.PrefetchScalarGridSpec(
            num_scalar_prefetch=2, grid=(n // tn, m // tm),
            in_specs=[pl.BlockSpec((tm, k), lambda j, i, te, rows: (i, 0)),
                      _w_spec(k, tn, w_index),
                      pl.BlockSpec((tm, tn), lambda j, i, te, rows: (i, j)),
                      pl.BlockSpec((None, None, 1, tn),
                                   lambda j, i, te, rows: (rows[i], gate_part, 0, j))],
            out_specs=pl.BlockSpec((tm, tn), lambda j, i, te, rows: (i, j)),
            scratch_shapes=[pltpu.VMEM((k, tn), BF16)]),
        out_shape=jax.ShapeDtypeStruct((m, n), F32),
        compiler_params=_params(2),
        name="matmul_gated_residual",
    )(tile_expert, rows, a, w, x, mod4)


def _dft_cos_sin(n):
    j = lax.broadcasted_iota(jnp.int32, (n, n), 0)
    k = lax.broadcasted_iota(jnp.int32, (n, n), 1)
    ang = ((j * k) % n).astype(F32) * (2.0 * math.pi / n)
    return jnp.cos(ang), jnp.sin(ang)


def _fourier_ab_kernel(c_ref, s_ref, w_ref, o_ref):
    w = w_ref[...]
    scale = FOURIER_GROUP ** -0.5
    a = jnp.dot(c_ref[...], w, preferred_element_type=F32, precision=lax.Precision.HIGHEST)
    b = jnp.dot(s_ref[...], w, preferred_element_type=F32, precision=lax.Precision.HIGHEST)
    o_ref[:, :FOURIER_GROUP] = (a * scale).astype(o_ref.dtype)
    o_ref[:, FOURIER_GROUP:] = (b * scale).astype(o_ref.dtype)


def fourier_ab(w_fmix):
    n_layers, n_groups, c, _ = w_fmix.shape
    cos_g, sin_g = _dft_cos_sin(c)
    return pl.pallas_call(
        _fourier_ab_kernel,
        grid=(n_layers, n_groups),
        in_specs=[pl.BlockSpec((c, c), lambda l, g: (0, 0)),
                  pl.BlockSpec((c, c), lambda l, g: (0, 0)),
                  pl.BlockSpec((None, None, c, c), lambda l, g: (l, g, 0, 0))],
        out_specs=pl.BlockSpec((None, None, c, 2 * c), lambda l, g: (l, g, 0, 0)),
        out_shape=jax.ShapeDtypeStruct((n_layers, n_groups, c, 2 * c), BF16),
        compiler_params=_params(2),
        name="fourier_ab",
    )(cos_g, sin_g, w_fmix)


def _fourier_channel_kernel(u_ref, ab_ref, pq_ref):
    c = FOURIER_GROUP
    for g in range(N_FOURIER_GROUPS):
        pq = jnp.dot(u_ref[:, g * c:(g + 1) * c], ab_ref[g], preferred_element_type=F32)
        pq_ref[0, :, g * c:(g + 1) * c] = pq[:, :c].astype(pq_ref.dtype)
        pq_ref[1, :, g * c:(g + 1) * c] = pq[:, c:].astype(pq_ref.dtype)


def fourier_channel_stage(z, ab, layer, *, n_rows, seq, row_block0, tm=512):
    tiles = seq // tm
    return pl.pallas_call(
        _fourier_channel_kernel,
        grid=(n_rows, tiles),
        in_specs=[pl.BlockSpec((tm, D_FOURIER), lambda b, i: (row_block0 + b * tiles + i, 0)),
                  pl.BlockSpec((None, N_FOURIER_GROUPS, FOURIER_GROUP, 2 * FOURIER_GROUP),
                               lambda b, i: (layer, 0, 0, 0))],
        out_specs=pl.BlockSpec((None, 2, tm, D_FOURIER), lambda b, i: (b, 0, i, 0)),
        out_shape=jax.ShapeDtypeStruct((n_rows, 2, seq, D_FOURIER), BF16),
        compiler_params=_params(2),
        name="fourier_channel_stage",
    )(z, ab)


def _fourier_seq_kernel(cs_ref, pq_ref, o_ref, *, scale):
    o_ref[...] = jnp.dot(cs_ref[...], pq_ref[...], preferred_element_type=F32) * scale


def fourier_sequence_stage(cs, pq, *, n_rows, seq, tm=512):
    tiles = seq // tm
    return pl.pallas_call(
        functools.partial(_fourier_seq_kernel, scale=seq ** -0.5),
        grid=(n_rows, tiles),
        in_specs=[pl.BlockSpec((tm, 2 * seq), lambda b, i: (i, 0)),
                  pl.BlockSpec((None, 2 * seq, D_FOURIER), lambda b, i: (b, 0, 0))],
        out_specs=pl.BlockSpec((tm, D_FOURIER), lambda b, i: (b * tiles + i, 0)),
        out_shape=jax.ShapeDtypeStruct((n_rows * seq, D_FOURIER), F32),
        compiler_params=_params(2),
        name="fourier_sequence_stage",
    )(cs, pq.reshape(n_rows, 2 * seq, D_FOURIER))


def dft_matrix(seq):
    c, s = _dft_cos_sin(seq)
    return jnp.concatenate([c, -s], axis=1).astype(BF16)


def _attn_kernel(slopes_ref, q_ref, k_ref, v_ref, o_ref, lse_ref, *, dilation, n_side, bq, bk):
    seq = q_ref.shape[0]
    hg = pl.program_id(2)
    row = lax.broadcasted_iota(jnp.int32, (bq, bk), 0)
    col = lax.broadcasted_iota(jnp.int32, (bq, bk), 1)
    lane = lax.broadcasted_iota(jnp.int32, (bq, LANES), 1)

    def body(qi, carry):
        q0 = pl.multiple_of(qi * bq, bq)
        ks = pl.multiple_of(jnp.clip(q0 - n_side, 0, seq - bk), n_side)
        dist = jnp.abs(col - row + (ks - q0))
        valid = dist <= n_side
        dist_f = dist.astype(F32) * float(dilation)
        lse_tile = jnp.zeros((bq, LANES), F32)
        for hh in range(HEADS_PER_STEP):
            cols = slice(hh * HEAD_DIM, (hh + 1) * HEAD_DIM)
            slope = slopes_ref[hg * HEADS_PER_STEP + hh]
            q = q_ref[pl.ds(q0, bq), cols]
            k = k_ref[pl.ds(ks, bk), cols]
            v = v_ref[pl.ds(ks, bk), cols]
            s = lax.dot_general(q, k, (((1,), (1,)), ((), ())), preferred_element_type=F32)
            s = jnp.where(valid, s - slope * dist_f, NEG_INF)
            m = jnp.max(s, axis=-1, keepdims=True)
            p = jnp.exp(s - m)
            den = jnp.sum(p, axis=-1, keepdims=True)
            num = jnp.dot(p.astype(BF16), v, preferred_element_type=F32)
            o_ref[pl.ds(q0, bq), cols] = (num / den).astype(o_ref.dtype)
            lse_tile = jnp.where(lane == hh, m + jnp.log(den), lse_tile)
        lse_ref[pl.ds(q0, bq), :] = lse_tile
        return carry

    lax.fori_loop(0, seq // bq, body, 0)


def attention_branch(z, slopes, *, window, dilation, n_rows, seq, row0, n_heads, d_in):
    t = z.shape[0]
    tg = n_rows * seq
    sub = seq // dilation
    n_side = (window // 2) // dilation
    bq = min(128, sub)
    bk = min(bq + 2 * n_side, sub)
    gw = HEADS_PER_STEP * HEAD_DIM
    groups = n_heads // HEADS_PER_STEP
    d_attn = n_heads * HEAD_DIM
    cpr = d_in // gw
    q_blk0 = D_FOURIER // gw
    rb0 = row0 // seq
    zv = z.reshape(t // dilation, dilation * d_in)

    def in_map(off):
        return lambda b, r, g, slopes: (rb0 + b, r * cpr + off + g)

    in_specs = [pl.BlockSpec((sub, gw), in_map(q_blk0)),
                pl.BlockSpec((sub, gw), in_map(q_blk0 + groups)),
                pl.BlockSpec((sub, gw), in_map(q_blk0 + 2 * groups))]
    o, lse = pl.pallas_call(
        functools.partial(_attn_kernel, dilation=dilation, n_side=n_side, bq=bq, bk=bk),
        grid_spec=pltpu.PrefetchScalarGridSpec(
            num_scalar_prefetch=1, grid=(n_rows, dilation, groups), in_specs=in_specs,
            out_specs=[pl.BlockSpec((sub, gw), lambda b, r, g, slopes: (b, r * groups + g)),
                       pl.BlockSpec((sub, LANES), lambda b, r, g, slopes: (b, r * groups + g))]),
        out_shape=[jax.ShapeDtypeStruct((tg // dilation, dilation * d_attn), F32),
                   jax.ShapeDtypeStruct((tg // dilation, dilation * groups * LANES), F32)],
        compiler_params=_params(3),
        name=f"attention_d{dilation}",
    )(slopes, zv, zv, zv)
    return o.reshape(tg, d_attn), lse.reshape(tg, groups * LANES)


def _rms(x, g):
    return x * lax.rsqrt(jnp.mean(x * x, axis=-1, keepdims=True) + EPS) * g


def _merge_tile(f_ref, o_refs, l_refs, gf_ref, ga_ref, y_ref, ya_ref, n_heads):
    for h in range(n_heads):
        lc = (h // HEADS_PER_STEP) * LANES + h % HEADS_PER_STEP
        lses = [l_ref[:, lc:lc + 1] for l_ref in l_refs]
        top = functools.reduce(jnp.maximum, lses)
        ws = [jnp.exp(l - top) for l in lses]
        den = functools.reduce(lambda a, b: a + b, ws)
        cols = slice(h * HEAD_DIM, (h + 1) * HEAD_DIM)
        num = functools.reduce(lambda a, b: a + b,
                               [o_ref[:, cols] * w for o_ref, w in zip(o_refs, ws)])
        ya_ref[:, cols] = num / den
    y_ref[:, :D_FOURIER] = _rms(f_ref[...], gf_ref[...]).astype(y_ref.dtype)
    y_ref[:, D_FOURIER:] = _rms(ya_ref[...], ga_ref[...]).astype(y_ref.dtype)


def _merge_kernel(*refs, n_heads, group_tiles):
    n_br = len(DILATED_BRANCHES)
    per_group = 1 + 2 * n_br
    n_groups = len(group_tiles)
    gf_ref, ga_ref, y_ref, ya_ref = refs[n_groups * per_group:]
    i = pl.program_id(0)
    tile0 = 0
    for gi, tiles in enumerate(group_tiles):
        grp = refs[gi * per_group:(gi + 1) * per_group]

        @pl.when(jnp.logical_and(i >= tile0, i < tile0 + tiles))
        def _(grp=grp):
            _merge_tile(grp[0], grp[1:1 + n_br], grp[1 + n_br:], gf_ref, ga_ref, y_ref, ya_ref, n_heads)

        tile0 += tiles


def merge_and_norm(group_inputs, g_out_f, g_out_a, *, n_heads, tm=256):
    d_attn = n_heads * HEAD_DIM
    group_tiles = [g[0].shape[0] // tm for g in group_inputs]
    n_tiles = sum(group_tiles)
    in_specs, args = [], []
    tile0 = 0
    for (f, os_, ls_), tiles in zip(group_inputs, group_tiles):
        row = lambda i, tile0=tile0, tiles=tiles: (jnp.clip(i - tile0, 0, tiles - 1), 0)
        in_specs += [pl.BlockSpec((tm, D_FOURIER), row)]
        in_specs += [pl.BlockSpec((tm, d_attn), row)] * len(os_)
        in_specs += [pl.BlockSpec((tm, ls_[0].shape[1]), row)] * len(ls_)
        args += [f, *os_, *ls_]
        tile0 += tiles
    fixed = lambda i: (0, 0)
    in_specs += [pl.BlockSpec((1, D_FOURIER), fixed), pl.BlockSpec((1, d_attn), fixed)]
    return pl.pallas_call(
        functools.partial(_merge_kernel, n_heads=n_heads, group_tiles=tuple(group_tiles)),
        grid=(n_tiles,),
        in_specs=in_specs,
        out_specs=pl.BlockSpec((tm, D_FOURIER + d_attn), lambda i: (i, 0)),
        out_shape=jax.ShapeDtypeStruct((n_tiles * tm, D_FOURIER + d_attn), BF16),
        scratch_shapes=[pltpu.VMEM((tm, d_attn), F32)],
        compiler_params=_params(1),
        name="merge_and_norm",
    )(*args, g_out_f, g_out_a)


def _gather_kernel(tok_ref, h_ref, o_ref, buf_ref, sem):
    tm = buf_ref.shape[0]

    def copy(r):
        return pltpu.make_async_copy(h_ref.at[pl.ds(tok_ref[0, r], 1)], buf_ref.at[pl.ds(r, 1)], sem)

    def start(r, carry):
        copy(r).start()
        return carry

    def wait(r, carry):
        copy(r).wait()
        return carry

    lax.fori_loop(0, tm, start, 0)
    lax.fori_loop(0, tm, wait, 0)
    o_ref[...] = buf_ref[...].astype(o_ref.dtype)


def gather_rows(h, row_token, *, tm):
    p = row_token.shape[0]
    d = h.shape[1]
    return pl.pallas_call(
        _gather_kernel,
        grid=(p // tm,),
        in_specs=[pl.BlockSpec((None, 1, tm), lambda i: (i, 0, 0), memory_space=pltpu.SMEM),
                  pl.BlockSpec(memory_space=pl.ANY)],
        out_specs=pl.BlockSpec((tm, d), lambda i: (i, 0)),
        out_shape=jax.ShapeDtypeStruct((p, d), BF16),
        scratch_shapes=[pltpu.VMEM((tm, d), F32), pltpu.SemaphoreType.DMA(())],
        compiler_params=_params(1),
        name="moe_gather",
    )(row_token.reshape(p // tm, 1, tm), h)


def _combine_kernel(rows_ref, pos_ref, o_ref, route_ref, x_ref, gate_ref, out_ref, buf_ref, sem):
    del rows_ref
    tm = x_ref.shape[0]

    def copy(r):
        return pltpu.make_async_copy(o_ref.at[pl.ds(pos_ref[0, r], 1)],
                                     buf_ref.at[pl.ds(r, 1)], sem)

    def start(r, carry):
        copy(r).start()
        return carry

    def wait(r, carry):
        copy(r).wait()
        return carry

    lax.fori_loop(0, TOP_K * tm, start, 0)
    lax.fori_loop(0, TOP_K * tm, wait, 0)
    route = route_ref[...]
    y = route[:, 2:3] * buf_ref[pl.ds(0, tm), :] + route[:, 3:4] * buf_ref[pl.ds(tm, tm), :]
    out_ref[...] = x_ref[...] + gate_ref[...] * y


def moe_combine(o_sorted, pos, route, x, mod4, gate_part, rows, *, tm):
    t, d = x.shape
    return pl.pallas_call(
        _combine_kernel,
        grid_spec=pltpu.PrefetchScalarGridSpec(
            num_scalar_prefetch=1, grid=(t // tm,),
            in_specs=[pl.BlockSpec((None, 1, TOP_K * tm), lambda i, rows: (i, 0, 0),
                                   memory_space=pltpu.SMEM),
                      pl.BlockSpec(memory_space=pl.ANY),
                      pl.BlockSpec((tm, LANES), lambda i, rows: (i, 0)),
                      pl.BlockSpec((tm, d), lambda i, rows: (i, 0)),
                      pl.BlockSpec((None, None, 1, d), lambda i, rows: (rows[i], gate_part, 0, 0))],
            out_specs=pl.BlockSpec((tm, d), lambda i, rows: (i, 0)),
            scratch_shapes=[pltpu.VMEM((TOP_K * tm, d), F32), pltpu.SemaphoreType.DMA(())]),
        out_shape=jax.ShapeDtypeStruct((t, d), F32),
        compiler_params=_params(1),
        name="moe_combine",
    )(rows, pos, o_sorted, route, x, mod4)


def moe_dispatch_plan(route, n_experts, tm):
    t = route.shape[0]
    n_slots = t * TOP_K
    p = n_slots + n_experts * tm
    expert = route[:, :TOP_K].astype(jnp.int32).reshape(n_slots)
    onehot = (expert[:, None] == jnp.arange(n_experts)[None, :]).astype(jnp.int32)
    rank = jnp.take_along_axis(jnp.cumsum(onehot, axis=0) - onehot, expert[:, None], axis=1)[:, 0]
    counts = jnp.sum(onehot, axis=0)
    padded = ((counts + tm - 1) // tm) * tm
    ends = jnp.cumsum(padded)
    starts = ends - padded
    pos = starts[expert] + rank
    row_token = jnp.zeros((p,), jnp.int32).at[pos].set(jnp.arange(n_slots, dtype=jnp.int32) // TOP_K)
    tile_start = jnp.arange(p // tm, dtype=jnp.int32) * tm
    tile_expert = jnp.minimum(jnp.searchsorted(ends, tile_start, side="right"),
                              n_experts - 1).astype(jnp.int32)
    return row_token, pos.reshape(t, TOP_K), tile_expert


def _final_norm_kernel(x_ref, g_ref, o_ref):
    o_ref[...] = _rms(x_ref[...], g_ref[...])


def final_norm(x, g, *, row0, n_rows, tm=512):
    d = x.shape[1]
    blk0 = row0 // tm
    return pl.pallas_call(
        _final_norm_kernel,
        grid=(n_rows // tm,),
        in_specs=[pl.BlockSpec((tm, d), lambda i: (blk0 + i, 0)),
                  pl.BlockSpec((1, d), lambda i: (0, 0))],
        out_specs=pl.BlockSpec((tm, d), lambda i: (i, 0)),
        out_shape=jax.ShapeDtypeStruct((n_rows, d), F32),
        compiler_params=_params(1),
        name="final_norm",
    )(x, g)


TM = 512
TM_COMBINE = 256


def _tile_rows(groups, tm):
    rows = []
    base = 0
    for n_rows, seq in groups:
        for b in range(n_rows):
            rows += [base + b] * (seq // tm)
        base += n_rows
    return jnp.asarray(np.asarray(rows, np.int32))


def kernel(x_prompt, x_sample, c_prompt, c_sample, w_ada, b_ada, g_norm_mix, g_norm_ff, w_in, w_fmix,
           g_out_f, g_out_a, w_out, w_ff1, w_ff3, w_ff2, w_router, w_e1, w_e3, w_e2, g_final):
    depth, d, d_in = w_in.shape
    groups = [(x_prompt.shape[0], x_prompt.shape[1]), (x_sample.shape[0], x_sample.shape[1])]
    group_row0 = [0, groups[0][0] * groups[0][1]]
    t = sum(b * s for b, s in groups)
    n_req = sum(b for b, _ in groups)
    d_attn = d - D_FOURIER
    n_heads = d_attn // HEAD_DIM
    n_experts = w_router.shape[-1]
    d_ff_e = w_e1.shape[-1]

    x = jnp.concatenate([x_prompt.reshape(-1, d), x_sample.reshape(-1, d)], axis=0)
    c = jnp.concatenate([c_prompt, c_sample], axis=0)
    c = jnp.pad(c, ((0, -n_req % 8), (0, 0)))
    mod = ada_modulation(c, w_ada, b_ada)
    rows = _tile_rows(groups, TM)
    rows_c = _tile_rows(groups, TM_COMBINE)
    dense_te = jnp.zeros((t // TM,), jnp.int32)
    slopes = jnp.exp2(-8.0 * (jnp.arange(n_heads, dtype=F32) + 1.0) / n_heads)
    ab = fourier_ab(w_fmix)
    dft = [dft_matrix(seq) for _, seq in groups]
    q_chunks = (D_FOURIER // TM, (D_FOURIER + d_attn) // TM, HEAD_DIM ** -0.5)

    for l in range(depth):
        mod4 = mod[l].reshape(mod.shape[1], 6, 1, d)
        layer = lambda e, l=l: l
        h = norm_modulate(x, rows, g_norm_mix[l:l + 1], mod4, 0, tm=TM)
        z = matmul(h, w_in, w_index=layer, tile_expert=dense_te, rows=rows, tm=TM, tn=TM,
                   out_dtype=BF16, col_scale=q_chunks)

        group_inputs = []
        for gi, ((n_rows, seq), row0) in enumerate(zip(groups, group_row0)):
            pq = fourier_channel_stage(z, ab, l, n_rows=n_rows, seq=seq, row_block0=row0 // TM, tm=TM)
            f = fourier_sequence_stage(dft[gi], pq, n_rows=n_rows, seq=seq, tm=TM)
            branches = [attention_branch(z, slopes, window=window, dilation=dilation, n_rows=n_rows,
                                         seq=seq, row0=row0, n_heads=n_heads, d_in=d_in)
                        for window, dilation in DILATED_BRANCHES]
            group_inputs.append((f, [b[0] for b in branches], [b[1] for b in branches]))
        y = merge_and_norm(group_inputs, g_out_f[l:l + 1], g_out_a[l:l + 1], n_heads=n_heads)
        x = matmul_gated_residual(y, w_out, x, mod4, 2, w_index=layer, tile_expert=dense_te, rows=rows,
                                  tm=TM, tn=TM)

        j = l // 2
        if l % 2 == 0:
            h = norm_modulate(x, rows, g_norm_ff[l:l + 1], mod4, 3, tm=TM)
            act = matmul_swiglu(h, w_ff1, w_ff3, w_index=lambda e, j=j: j, tile_expert=dense_te,
                                rows=rows, tm=TM, tn=TM)
            x = matmul_gated_residual(act, w_ff2, x, mod4, 5, w_index=lambda e, j=j: j,
                                      tile_expert=dense_te, rows=rows, tm=TM, tn=256)
        else:
            h, route = norm_modulate(x, rows, g_norm_ff[l:l + 1], mod4, 3, out_dtype=F32,
                                     w_router=w_router[j], tm=TM)
            row_token, pos, tile_expert = moe_dispatch_plan(route, n_experts, TM)
            xs = gather_rows(h, row_token, tm=TM)
            expert_w = lambda e, j=j: j * n_experts + e
            act = matmul_swiglu(xs, w_e1.reshape(-1, d, d_ff_e), w_e3.reshape(-1, d, d_ff_e),
                                w_index=expert_w, tile_expert=tile_expert, rows=tile_expert, tm=TM, tn=TM)
            o_sorted = matmul(act, w_e2.reshape(-1, d_ff_e, d), w_index=expert_w,
                              tile_expert=tile_expert, rows=tile_expert, tm=TM, tn=256, out_dtype=F32)
            pos_tiles = pos.reshape(t // TM_COMBINE, TM_COMBINE, TOP_K).transpose(0, 2, 1)
            pos_tiles = pos_tiles.reshape(t // TM_COMBINE, 1, TOP_K * TM_COMBINE)
            x = moe_combine(o_sorted, pos_tiles, route, x, mod4, 5, rows_c, tm=TM_COMBINE)

    g = g_final.reshape(1, d)
    outs = []
    for (n_rows, seq), row0 in zip(groups, group_row0):
        outs.append(final_norm(x, g, row0=row0, n_rows=n_rows * seq, tm=TM).reshape(n_rows, seq, d))
    return tuple(outs)
```

```python
import functools
import math

import numpy as np
import jax
import jax.numpy as jnp
from jax import lax
from jax.experimental import pallas as pl
from jax.experimental.pallas import tpu as pltpu

HEAD_DIM = 128
FOURIER_GROUP = 128
N_FOURIER_GROUPS = 4
D_FOURIER = FOURIER_GROUP * N_FOURIER_GROUPS
DILATED_BRANCHES = ((128, 1), (512, 4), (2048, 16))
TOP_K = 2
EPS = 1e-6
NEG_INF = -1e30
LANES = 128
VMEM_LIMIT = 56 * 1024 * 1024
ATTN_BLOCK_BYTES = 4 * 1024 * 1024

BF16 = jnp.bfloat16
F32 = jnp.float32


def _params(n_axes, vmem=VMEM_LIMIT):
    return pltpu.CompilerParams(
        dimension_semantics=("arbitrary",) * n_axes, vmem_limit_bytes=vmem)


def _ada_kernel(c_ref, w_ref, b_ref, o_ref):
    c = c_ref[...]
    a = (c * jax.nn.sigmoid(c)).astype(BF16)
    acc = jnp.dot(a, w_ref[...].astype(BF16), preferred_element_type=F32)
    o_ref[...] = acc + b_ref[...]


def ada_modulation(c, w_ada, b_ada, tn=1024):
    n_layers, d, n = w_ada.shape
    r8 = c.shape[0]
    return pl.pallas_call(
        _ada_kernel,
        grid=(n_layers, n // tn),
        in_specs=[
            pl.BlockSpec((r8, d), lambda l, j: (0, 0)),
            pl.BlockSpec((None, d, tn), lambda l, j: (l, 0, j)),
            pl.BlockSpec((None, 1, tn), lambda l, j: (l, 0, j)),
        ],
        out_specs=pl.BlockSpec((None, r8, tn), lambda l, j: (l, 0, j)),
        out_shape=jax.ShapeDtypeStruct((n_layers, r8, n), F32),
        compiler_params=_params(2),
        name="ada_modulation",
    )(c, w_ada, b_ada.reshape(n_layers, 1, n))


def _norm_mod(x, g, sc, sh):
    y = x * lax.rsqrt(jnp.mean(x * x, axis=-1, keepdims=True) + EPS)
    return (y * g) * (1.0 + sc) + sh


def _norm_mod_kernel(rows_ref, x_ref, g_ref, sc_ref, sh_ref, h_ref):
    del rows_ref
    h_ref[...] = _norm_mod(x_ref[...], g_ref[...], sc_ref[...], sh_ref[...]).astype(h_ref.dtype)


def _norm_mod_router_kernel(rows_ref, x_ref, g_ref, sc_ref, sh_ref, wr_ref, h_ref, route_ref,
                            *, n_experts):
    del rows_ref
    h = _norm_mod(x_ref[...], g_ref[...], sc_ref[...], sh_ref[...])
    h_ref[...] = h.astype(h_ref.dtype)
    logits = jnp.dot(h, wr_ref[...], preferred_element_type=F32,
                     precision=lax.Precision.HIGHEST)
    lane = lax.broadcasted_iota(jnp.int32, logits.shape, 1)
    logits = jnp.where(lane < n_experts, logits, -jnp.inf)
    m1 = jnp.max(logits, axis=-1, keepdims=True)
    i1 = jnp.min(jnp.where(logits == m1, lane, LANES), axis=-1, keepdims=True)
    rest = jnp.where(lane == i1, -jnp.inf, logits)
    m2 = jnp.max(rest, axis=-1, keepdims=True)
    i2 = jnp.min(jnp.where(rest == m2, lane, LANES), axis=-1, keepdims=True)
    e2 = jnp.exp(m2 - m1)
    den = 1.0 + e2
    route = jnp.where(lane == 0, i1.astype(F32), 0.0)
    route = jnp.where(lane == 1, i2.astype(F32), route)
    route = jnp.where(lane == 2, 1.0 / den, route)
    route = jnp.where(lane == 3, e2 / den, route)
    route_ref[...] = route


def norm_modulate(x, rows, g, mod4, layer_part, *, out_dtype=BF16, w_router=None, tm=512):
    t, d = x.shape
    sh_part, sc_part = layer_part, layer_part + 1
    in_specs = [
        pl.BlockSpec((tm, d), lambda i, rows: (i, 0)),
        pl.BlockSpec((1, d), lambda i, rows: (0, 0)),
        pl.BlockSpec((None, None, 1, d), lambda i, rows: (rows[i], sc_part, 0, 0)),
        pl.BlockSpec((None, None, 1, d), lambda i, rows: (rows[i], sh_part, 0, 0)),
    ]
    h_spec = pl.BlockSpec((tm, d), lambda i, rows: (i, 0))
    h_shape = jax.ShapeDtypeStruct((t, d), out_dtype)
    if w_router is None:
        return pl.pallas_call(
            _norm_mod_kernel,
            grid_spec=pltpu.PrefetchScalarGridSpec(
                num_scalar_prefetch=1, grid=(t // tm,), in_specs=in_specs, out_specs=h_spec),
            out_shape=h_shape,
            compiler_params=_params(1),
            name="norm_modulate",
        )(rows, x, g, mod4, mod4)
    n_experts = w_router.shape[1]
    wr = jnp.pad(w_router, ((0, 0), (0, LANES - n_experts)))
    in_specs.append(pl.BlockSpec((d, LANES), lambda i, rows: (0, 0)))
    return pl.pallas_call(
        functools.partial(_norm_mod_router_kernel, n_experts=n_experts),
        grid_spec=pltpu.PrefetchScalarGridSpec(
            num_scalar_prefetch=1, grid=(t // tm,), in_specs=in_specs,
            out_specs=[h_spec, pl.BlockSpec((tm, LANES), lambda i, rows: (i, 0))]),
        out_shape=[h_shape, jax.ShapeDtypeStruct((t, LANES), F32)],
        compiler_params=_params(1),
        name="norm_modulate_router",
    )(rows, x, g, mod4, mod4, wr)


CAST_ROWS = 512


def _cast_weights_if_changed(te_ref, w_refs, wb_refs):
    i = pl.program_id(1)
    changed = jnp.logical_or(i == 0, te_ref[i] != te_ref[jnp.maximum(i - 1, 0)])

    @pl.when(changed)
    def _():
        k = w_refs[0].shape[0]
        step = math.gcd(k, CAST_ROWS)

        def body(c, carry):
            r0 = pl.multiple_of(c * step, step)
            for w_ref, wb_ref in zip(w_refs, wb_refs):
                wb_ref[pl.ds(r0, step), :] = w_ref[pl.ds(r0, step), :].astype(BF16)
            return carry

        lax.fori_loop(0, k // step, body, 0)


def _mm_plain_kernel(te_ref, rows_ref, used_ref, a_ref, w_ref, o_ref, wb_ref):
    del rows_ref
    _cast_weights_if_changed(te_ref, (w_ref,), (wb_ref,))

    @pl.when(pl.program_id(1) < used_ref[0])
    def _():
        acc = jnp.dot(a_ref[...], wb_ref[...], preferred_element_type=F32)
        o_ref[...] = acc.astype(o_ref.dtype)

    @pl.when(pl.program_id(1) >= used_ref[0])
    def _():
        o_ref[...] = jnp.zeros_like(o_ref)


def _mm_qkv_kernel(te_ref, rows_ref, used_ref, a_ref, w_ref, *rest, dilations, q_chunks, q_scale):
    del rows_ref, used_ref
    out_refs = rest[:len(dilations)]
    wb_ref, acc_ref = rest[len(dilations):]
    _cast_weights_if_changed(te_ref, (w_ref,), (wb_ref,))
    acc = jnp.dot(a_ref[...], wb_ref[...], preferred_element_type=F32)
    acc = acc * jnp.where(pl.program_id(0) < q_chunks, q_scale, 1.0).astype(F32)
    tm, tn = acc.shape
    slabs = tn // LANES
    for c in range(slabs):
        acc_ref[c] = acc[:, c * LANES:(c + 1) * LANES]
    for o_ref, d in zip(out_refs, dilations):
        if d == 1:
            o_ref[0] = acc.astype(o_ref.dtype)
            continue
        for r in range(d):
            for c in range(slabs):
                o_ref[r, :, c * LANES:(c + 1) * LANES] = (
                    acc_ref[c, pl.ds(r, tm // d, stride=d), :].astype(o_ref.dtype))


def _mm_swiglu_kernel(te_ref, rows_ref, used_ref, a_ref, w1_ref, w3_ref, o_ref, wb1_ref, wb3_ref):
    del rows_ref
    _cast_weights_if_changed(te_ref, (w1_ref, w3_ref), (wb1_ref, wb3_ref))

    @pl.when(pl.program_id(1) < used_ref[0])
    def _():
        a = a_ref[...]
        g = jnp.dot(a, wb1_ref[...], preferred_element_type=F32)
        u = jnp.dot(a, wb3_ref[...], preferred_element_type=F32)
        o_ref[...] = ((g * jax.nn.sigmoid(g)) * u).astype(o_ref.dtype)

    @pl.when(pl.program_id(1) >= used_ref[0])
    def _():
        o_ref[...] = jnp.zeros_like(o_ref)


def _mm_resid_kernel(te_ref, rows_ref, used_ref, a_ref, w_ref, x_ref, gate_ref, o_ref, wb_ref):
    del rows_ref, used_ref
    _cast_weights_if_changed(te_ref, (w_ref,), (wb_ref,))
    acc = jnp.dot(a_ref[...], wb_ref[...], preferred_element_type=F32)
    o_ref[...] = x_ref[...] + gate_ref[...] * acc


def _a_spec(tm, k):
    return pl.BlockSpec((tm, k), lambda j, i, te, rows, used: (jnp.minimum(i, used[0] - 1), 0))


def _w_spec(k, tn, w_index, col_block0=0):
    return pl.BlockSpec((None, k, tn),
                        lambda j, i, te, rows, used: (w_index(te[i]), 0, col_block0 + j))


def _out_spec(tm, tn):
    return pl.BlockSpec((tm, tn), lambda j, i, te, rows, used: (i, j))


class Tiling:
    def __init__(self, tile_expert, rows, used):
        self.args = (tile_expert, rows, used)


def matmul_qkv(a, w, tiling, *, w_index, tm, tn, col0, n, dilations, q_cols, q_scale):
    m, k = a.shape
    return pl.pallas_call(
        functools.partial(_mm_qkv_kernel, dilations=tuple(dilations), q_chunks=q_cols // tn,
                          q_scale=q_scale),
        grid_spec=pltpu.PrefetchScalarGridSpec(
            num_scalar_prefetch=3, grid=(n // tn, m // tm),
            in_specs=[_a_spec(tm, k), _w_spec(k, tn, w_index, col0 // tn)],
            out_specs=[pl.BlockSpec((d, tm // d, tn), lambda j, i, te, rows, used: (0, i, j))
                       for d in dilations],
            scratch_shapes=[pltpu.VMEM((k, tn), BF16), pltpu.VMEM((tn // LANES, tm, LANES), F32)]),
        out_shape=[jax.ShapeDtypeStruct((d, m // d, n), BF16) for d in dilations],
        compiler_params=_params(2),
        name="matmul_qkv",
    )(*tiling.args, a, w)


def matmul(a, w, tiling, *, w_index, tm, tn, out_dtype, n=None):
    m, k = a.shape
    n = w.shape[-1] if n is None else n
    return pl.pallas_call(
        _mm_plain_kernel,
        grid_spec=pltpu.PrefetchScalarGridSpec(
            num_scalar_prefetch=3, grid=(n // tn, m // tm),
            in_specs=[_a_spec(tm, k), _w_spec(k, tn, w_index)],
            out_specs=_out_spec(tm, tn),
            scratch_shapes=[pltpu.VMEM((k, tn), BF16)]),
        out_shape=jax.ShapeDtypeStruct((m, n), out_dtype),
        compiler_params=_params(2),
        name="matmul",
    )(*tiling.args, a, w)


def matmul_swiglu(a, w1, w3, tiling, *, w_index, tm, tn):
    m, k = a.shape
    n = w1.shape[-1]
    return pl.pallas_call(
        _mm_swiglu_kernel,
        grid_spec=pltpu.PrefetchScalarGridSpec(
            num_scalar_prefetch=3, grid=(n // tn, m // tm),
            in_specs=[_a_spec(tm, k), _w_spec(k, tn, w_index), _w_spec(k, tn, w_index)],
            out_specs=_out_spec(tm, tn),
            scratch_shapes=[pltpu.VMEM((k, tn), BF16), pltpu.VMEM((k, tn), BF16)]),
        out_shape=jax.ShapeDtypeStruct((m, n), BF16),
        compiler_params=_params(2),
        name="matmul_swiglu",
    )(*tiling.args, a, w1, w3)


def matmul_gated_residual(a, w, x, mod4, gate_part, tiling, *, w_index, tm, tn):
    m, k = a.shape
    n = w.shape[-1]
    return pl.pallas_call(
        _mm_resid_kernel,
        grid_spec=pltpu.PrefetchScalarGridSpec(
            num_scalar_prefetch=3, grid=(n // tn, m // tm),
            in_specs=[_a_spec(tm, k), _w_spec(k, tn, w_index), _out_spec(tm, tn),
                      pl.BlockSpec((None, None, 1, tn),
                                   lambda j, i, te, rows, used: (rows[i], gate_part, 0, j))],
            out_specs=_out_spec(tm, tn),
            scratch_shapes=[pltpu.VMEM((k, tn), BF16)]),
        out_shape=jax.ShapeDtypeStruct((m, n), F32),
        compiler_params=_params(2),
        name="matmul_gated_residual",
    )(*tiling.args, a, w, x, mod4)


def _dft_cos_sin(n):
    j = lax.broadcasted_iota(jnp.int32, (n, n), 0)
    k = lax.broadcasted_iota(jnp.int32, (n, n), 1)
    ang = ((j * k) % n).astype(F32) * (2.0 * math.pi / n)
    return jnp.cos(ang), jnp.sin(ang)


def _fourier_ab_kernel(c_ref, s_ref, w_ref, o_ref):
    w = w_ref[...]
    scale = FOURIER_GROUP ** -0.5
    a = jnp.dot(c_ref[...], w, preferred_element_type=F32, precision=lax.Precision.HIGHEST)
    b = jnp.dot(s_ref[...], w, preferred_element_type=F32, precision=lax.Precision.HIGHEST)
    o_ref[:, :FOURIER_GROUP] = (a * scale).astype(o_ref.dtype)
    o_ref[:, FOURIER_GROUP:] = (b * scale).astype(o_ref.dtype)


def fourier_ab(w_fmix):
    n_layers, n_groups, c, _ = w_fmix.shape
    cos_g, sin_g = _dft_cos_sin(c)
    return pl.pallas_call(
        _fourier_ab_kernel,
        grid=(n_layers, n_groups),
        in_specs=[pl.BlockSpec((c, c), lambda l, g: (0, 0)),
                  pl.BlockSpec((c, c), lambda l, g: (0, 0)),
                  pl.BlockSpec((None, None, c, c), lambda l, g: (l, g, 0, 0))],
        out_specs=pl.BlockSpec((None, None, c, 2 * c), lambda l, g: (l, g, 0, 0)),
        out_shape=jax.ShapeDtypeStruct((n_layers, n_groups, c, 2 * c), BF16),
        compiler_params=_params(2),
        name="fourier_ab",
    )(cos_g, sin_g, w_fmix)


def _fourier_channel_kernel(u_ref, ab_ref, pq_ref):
    c = FOURIER_GROUP
    for g in range(N_FOURIER_GROUPS):
        pq = jnp.dot(u_ref[:, g * c:(g + 1) * c], ab_ref[g], preferred_element_type=F32)
        pq_ref[0, :, g * c:(g + 1) * c] = pq[:, :c].astype(pq_ref.dtype)
        pq_ref[1, :, g * c:(g + 1) * c] = pq[:, c:].astype(pq_ref.dtype)


def fourier_channel_stage(u, ab, layer, *, n_rows, seq, row_block0, tm=512):
    tiles = seq // tm
    return pl.pallas_call(
        _fourier_channel_kernel,
        grid=(n_rows, tiles),
        in_specs=[pl.BlockSpec((tm, D_FOURIER), lambda b, i: (row_block0 + b * tiles + i, 0)),
                  pl.BlockSpec((None, N_FOURIER_GROUPS, FOURIER_GROUP, 2 * FOURIER_GROUP),
                               lambda b, i: (layer, 0, 0, 0))],
        out_specs=pl.BlockSpec((None, 2, tm, D_FOURIER), lambda b, i: (b, 0, i, 0)),
        out_shape=jax.ShapeDtypeStruct((n_rows, 2, seq, D_FOURIER), BF16),
        compiler_params=_params(2),
        name="fourier_channel_stage",
    )(u, ab)


def _fourier_seq_kernel(cs_ref, pq_ref, o_ref, *, scale):
    o_ref[...] = jnp.dot(cs_ref[...], pq_ref[...], preferred_element_type=F32) * scale


def fourier_sequence_stage(cs, pq, *, n_rows, seq, tm=512):
    tiles = seq // tm
    return pl.pallas_call(
        functools.partial(_fourier_seq_kernel, scale=seq ** -0.5),
        grid=(n_rows, tiles),
        in_specs=[pl.BlockSpec((tm, 2 * seq), lambda b, i: (i, 0)),
                  pl.BlockSpec((None, 2 * seq, D_FOURIER), lambda b, i: (b, 0, 0))],
        out_specs=pl.BlockSpec((tm, D_FOURIER), lambda b, i: (b * tiles + i, 0)),
        out_shape=jax.ShapeDtypeStruct((n_rows * seq, D_FOURIER), F32),
        compiler_params=_params(2),
        name="fourier_sequence_stage",
    )(cs, pq.reshape(n_rows, 2 * seq, D_FOURIER))


def dft_matrix(seq):
    c, s = _dft_cos_sin(seq)
    return jnp.concatenate([c, -s], axis=1).astype(BF16)


def _attn_kernel(slopes_ref, q_ref, k_ref, v_ref, o_ref, lse_ref, bias_ref, *, dilation, n_side,
                 bq, bk, hps):
    sub = q_ref.shape[0]
    n_cases = bias_ref.shape[0]
    lane = lax.broadcasted_iota(jnp.int32, (bq, LANES), 1)

    @pl.when(jnp.logical_and(pl.program_id(1) == 0, pl.program_id(2) == 0))
    def _():
        row = lax.broadcasted_iota(jnp.int32, (bq, bk), 0)
        col = lax.broadcasted_iota(jnp.int32, (bq, bk), 1)
        for case in range(n_cases):
            dist = jnp.abs(col - row - case * n_side)
            valid = dist <= n_side
            dist_f = dist.astype(F32) * float(dilation)
            for hh in range(hps):
                slope = slopes_ref[pl.program_id(0) * hps + hh]
                bias_ref[case, hh] = jnp.where(valid, -slope * dist_f, NEG_INF)

    def body(qi, carry):
        q0 = pl.multiple_of(qi * bq, bq)
        ks = pl.multiple_of(jnp.clip(q0 - n_side, 0, sub - bk), n_side)
        case = (q0 - ks) // n_side
        lse_tile = jnp.zeros((bq, LANES), F32)
        for hh in range(hps):
            cols = slice(hh * HEAD_DIM, (hh + 1) * HEAD_DIM)
            q = q_ref[pl.ds(q0, bq), cols]
            k = k_ref[pl.ds(ks, bk), cols]
            v = v_ref[pl.ds(ks, bk), cols]
            s = lax.dot_general(q, k, (((1,), (1,)), ((), ())), preferred_element_type=F32)
            s = s + bias_ref[case, hh]
            m = jnp.max(s, axis=-1, keepdims=True)
            p = jnp.exp(s - m)
            den = jnp.sum(p, axis=-1, keepdims=True)
            num = jnp.dot(p.astype(BF16), v, preferred_element_type=F32)
            o_ref[pl.ds(q0, bq), cols] = (num / den).astype(o_ref.dtype)
            lse_tile = jnp.where(lane == hh, m + jnp.log(den), lse_tile)
        lse_ref[pl.ds(q0, bq), :] = lse_tile
        return carry

    lax.fori_loop(0, sub // bq, body, 0)


def heads_per_step(sub, n_heads):
    fits = [h for h in range(1, n_heads + 1)
            if n_heads % h == 0 and sub * h * HEAD_DIM * 2 <= ATTN_BLOCK_BYTES]
    return max(fits)


def attention_branch(qkv, slopes, *, window, dilation, n_rows, seq, row0, n_heads):
    tg = n_rows * seq
    sub = seq // dilation
    n_side = (window // 2) // dilation
    bq = min(128, sub)
    bk = min(bq + 2 * n_side, sub)
    n_cases = 1 if sub == bq else 3
    hps = heads_per_step(sub, n_heads)
    gw = hps * HEAD_DIM
    groups = n_heads // hps
    d_attn = n_heads * HEAD_DIM
    rb0 = row0 // seq

    def in_map(part):
        return lambda g, b, r, slopes: (r, rb0 + b, part * groups + g)

    out_map = lambda g, b, r, slopes: (r, b, g)
    o, lse = pl.pallas_call(
        functools.partial(_attn_kernel, dilation=dilation, n_side=n_side, bq=bq, bk=bk, hps=hps),
        grid_spec=pltpu.PrefetchScalarGridSpec(
            num_scalar_prefetch=1, grid=(groups, n_rows, dilation),
            in_specs=[pl.BlockSpec((None, sub, gw), in_map(part)) for part in range(3)],
            out_specs=[pl.BlockSpec((None, sub, gw), out_map),
                       pl.BlockSpec((None, sub, LANES), out_map)],
            scratch_shapes=[pltpu.VMEM((n_cases, hps, bq, bk), F32)]),
        out_shape=[jax.ShapeDtypeStruct((dilation, tg // dilation, d_attn), BF16),
                   jax.ShapeDtypeStruct((dilation, tg // dilation, groups * LANES), F32)],
        compiler_params=_params(3),
        name=f"attention_d{dilation}",
    )(slopes, qkv, qkv, qkv)
    return o, lse, hps


def _rms(x, g):
    return x * lax.rsqrt(jnp.mean(x * x, axis=-1, keepdims=True) + EPS) * g


def _merge_tile(f_ref, o_refs, l_refs, gf_ref, ga_ref, y_ref, ya_ref, on_ref, ln_ref, n_heads, hps):
    tm = y_ref.shape[0]
    for bi, (o_ref, l_ref) in enumerate(zip(o_refs, l_refs)):
        d = o_ref.shape[0]
        if d == 1:
            continue
        for r in range(d):
            rows = pl.ds(r, tm // d, stride=d)
            for h in range(n_heads):
                on_ref[bi, h, rows, :] = o_ref[r, :, h * HEAD_DIM:(h + 1) * HEAD_DIM].astype(F32)
            for g in range(l_ref.shape[2] // LANES):
                ln_ref[bi, g, rows, :] = l_ref[r, :, g * LANES:(g + 1) * LANES]

    for h in range(n_heads):
        cols = slice(h * HEAD_DIM, (h + 1) * HEAD_DIM)
        os_, lses = [], []
        for bi, (o_ref, l_ref) in enumerate(zip(o_refs, l_refs)):
            g, hh = divmod(h, hps[bi])
            if o_ref.shape[0] == 1:
                os_.append(o_ref[0, :, cols].astype(F32))
                lses.append(l_ref[0, :, g * LANES + hh:g * LANES + hh + 1])
            else:
                os_.append(on_ref[bi, h])
                lses.append(ln_ref[bi, g, :, hh:hh + 1])
        top = functools.reduce(jnp.maximum, lses)
        ws = [jnp.exp(l - top) for l in lses]
        den = functools.reduce(lambda a, b: a + b, ws)
        num = functools.reduce(lambda a, b: a + b, [o * w for o, w in zip(os_, ws)])
        ya_ref[:, cols] = num / den
    y_ref[:, :D_FOURIER] = _rms(f_ref[...], gf_ref[...]).astype(y_ref.dtype)
    y_ref[:, D_FOURIER:] = _rms(ya_ref[...], ga_ref[...]).astype(y_ref.dtype)


def _merge_kernel(*refs, n_heads, group_tiles, hps):
    n_br = len(DILATED_BRANCHES)
    per_group = 1 + 2 * n_br
    n_groups = len(group_tiles)
    gf_ref, ga_ref, y_ref, ya_ref, on_ref, ln_ref = refs[n_groups * per_group:]
    i = pl.program_id(0)
    tile0 = 0
    for gi, tiles in enumerate(group_tiles):
        grp = refs[gi * per_group:(gi + 1) * per_group]

        @pl.when(jnp.logical_and(i >= tile0, i < tile0 + tiles))
        def _(grp=grp, gi=gi):
            _merge_tile(grp[0], grp[1:1 + n_br], grp[1 + n_br:], gf_ref, ga_ref, y_ref, ya_ref,
                        on_ref, ln_ref, n_heads, hps[gi])

        tile0 += tiles


def merge_and_norm(group_inputs, g_out_f, g_out_a, *, n_heads, tm=256):
    d_attn = n_heads * HEAD_DIM
    group_tiles = [g[0].shape[0] // tm for g in group_inputs]
    n_tiles = sum(group_tiles)
    n_br = len(DILATED_BRANCHES)
    in_specs, args = [], []
    tile0 = 0
    for (f, os_, ls_, _), tiles in zip(group_inputs, group_tiles):
        local = lambda i, tile0=tile0, tiles=tiles: jnp.clip(i - tile0, 0, tiles - 1)
        in_specs.append(pl.BlockSpec((tm, D_FOURIER), lambda i, local=local: (local(i), 0)))
        for arr in (*os_, *ls_):
            d = arr.shape[0]
            in_specs.append(pl.BlockSpec((d, tm // d, arr.shape[2]),
                                         lambda i, local=local: (0, local(i), 0)))
        args += [f, *os_, *ls_]
        tile0 += tiles
    fixed = lambda i: (0, 0)
    in_specs += [pl.BlockSpec((1, D_FOURIER), fixed), pl.BlockSpec((1, d_attn), fixed)]
    max_groups = max(l.shape[2] // LANES for g in group_inputs for l in g[2])
    return pl.pallas_call(
        functools.partial(_merge_kernel, n_heads=n_heads, group_tiles=tuple(group_tiles),
                          hps=tuple(tuple(g[3]) for g in group_inputs)),
        grid=(n_tiles,),
        in_specs=in_specs,
        out_specs=pl.BlockSpec((tm, D_FOURIER + d_attn), lambda i: (i, 0)),
        out_shape=jax.ShapeDtypeStruct((n_tiles * tm, D_FOURIER + d_attn), BF16),
        scratch_shapes=[pltpu.VMEM((tm, d_attn), F32),
                        pltpu.VMEM((n_br, n_heads, tm, HEAD_DIM), F32),
                        pltpu.VMEM((n_br, max_groups, tm, LANES), F32)],
        compiler_params=_params(1),
        name="merge_and_norm",
    )(*args, g_out_f, g_out_a)


def _gather_kernel(tok_ref, h_ref, o_ref, buf_ref, sem):
    tm = buf_ref.shape[0]

    def copy(r):
        return pltpu.make_async_copy(h_ref.at[pl.ds(tok_ref[0, r], 1)], buf_ref.at[pl.ds(r, 1)], sem)

    def start(r, carry):
        copy(r).start()
        return carry

    def wait(r, carry):
        copy(r).wait()
        return carry

    lax.fori_loop(0, tm, start, 0)
    lax.fori_loop(0, tm, wait, 0)
    o_ref[...] = buf_ref[...].astype(o_ref.dtype)


def gather_rows(h, row_token, *, tm):
    p = row_token.shape[0]
    d = h.shape[1]
    return pl.pallas_call(
        _gather_kernel,
        grid=(p // tm,),
        in_specs=[pl.BlockSpec((None, 1, tm), lambda i: (i, 0, 0), memory_space=pltpu.SMEM),
                  pl.BlockSpec(memory_space=pl.ANY)],
        out_specs=pl.BlockSpec((tm, d), lambda i: (i, 0)),
        out_shape=jax.ShapeDtypeStruct((p, d), BF16),
        scratch_shapes=[pltpu.VMEM((tm, d), F32), pltpu.SemaphoreType.DMA(())],
        compiler_params=_params(1),
        name="moe_gather",
    )(row_token.reshape(p // tm, 1, tm), h)


def _combine_kernel(rows_ref, pos_ref, o_ref, route_ref, x_ref, gate_ref, out_ref, buf_ref, sem):
    del rows_ref
    tm = x_ref.shape[0]

    def copy(r):
        return pltpu.make_async_copy(o_ref.at[pl.ds(pos_ref[0, r], 1)],
                                     buf_ref.at[pl.ds(r, 1)], sem)

    def start(r, carry):
        copy(r).start()
        return carry

    def wait(r, carry):
        copy(r).wait()
        return carry

    lax.fori_loop(0, TOP_K * tm, start, 0)
    lax.fori_loop(0, TOP_K * tm, wait, 0)
    route = route_ref[...]
    y = route[:, 2:3] * buf_ref[pl.ds(0, tm), :] + route[:, 3:4] * buf_ref[pl.ds(tm, tm), :]
    out_ref[...] = x_ref[...] + gate_ref[...] * y


def moe_combine(o_sorted, pos, route, x, mod4, gate_part, rows, *, tm):
    t, d = x.shape
    return pl.pallas_call(
        _combine_kernel,
        grid_spec=pltpu.PrefetchScalarGridSpec(
            num_scalar_prefetch=1, grid=(t // tm,),
            in_specs=[pl.BlockSpec((None, 1, TOP_K * tm), lambda i, rows: (i, 0, 0),
                                   memory_space=pltpu.SMEM),
                      pl.BlockSpec(memory_space=pl.ANY),
                      pl.BlockSpec((tm, LANES), lambda i, rows: (i, 0)),
                      pl.BlockSpec((tm, d), lambda i, rows: (i, 0)),
                      pl.BlockSpec((None, None, 1, d), lambda i, rows: (rows[i], gate_part, 0, 0))],
            out_specs=pl.BlockSpec((tm, d), lambda i, rows: (i, 0)),
            scratch_shapes=[pltpu.VMEM((TOP_K * tm, d), F32), pltpu.SemaphoreType.DMA(())]),
        out_shape=jax.ShapeDtypeStruct((t, d), F32),
        compiler_params=_params(1),
        name="moe_combine",
    )(rows, pos, o_sorted, route, x, mod4)


def moe_dispatch_plan(route, n_experts, tm):
    t = route.shape[0]
    n_slots = t * TOP_K
    p = n_slots + n_experts * tm
    expert = route[:, :TOP_K].astype(jnp.int32).reshape(n_slots)
    onehot = (expert[:, None] == jnp.arange(n_experts)[None, :]).astype(jnp.int32)
    rank = jnp.take_along_axis(jnp.cumsum(onehot, axis=0) - onehot, expert[:, None], axis=1)[:, 0]
    counts = jnp.sum(onehot, axis=0)
    padded = ((counts + tm - 1) // tm) * tm
    ends = jnp.cumsum(padded)
    starts = ends - padded
    pos = starts[expert] + rank
    row_token = jnp.zeros((p,), jnp.int32).at[pos].set(jnp.arange(n_slots, dtype=jnp.int32) // TOP_K)
    tile_start = jnp.arange(p // tm, dtype=jnp.int32) * tm
    tile_expert = jnp.sum((tile_start[:, None] >= ends[None, :]).astype(jnp.int32), axis=1)
    tile_expert = jnp.minimum(tile_expert, n_experts - 1)
    used = (ends[-1:] // tm).astype(jnp.int32)
    return row_token, pos.reshape(t, TOP_K), tile_expert, used


def _final_norm_kernel(x_ref, g_ref, o_ref):
    o_ref[...] = _rms(x_ref[...], g_ref[...])


def final_norm(x, g, *, row0, n_rows, tm=512):
    d = x.shape[1]
    blk0 = row0 // tm
    return pl.pallas_call(
        _final_norm_kernel,
        grid=(n_rows // tm,),
        in_specs=[pl.BlockSpec((tm, d), lambda i: (blk0 + i, 0)),
                  pl.BlockSpec((1, d), lambda i: (0, 0))],
        out_specs=pl.BlockSpec((tm, d), lambda i: (i, 0)),
        out_shape=jax.ShapeDtypeStruct((n_rows, d), F32),
        compiler_params=_params(1),
        name="final_norm",
    )(x, g)


TM = 512
TM_COMBINE = 256


def _tile_rows(groups, tm):
    rows = []
    base = 0
    for n_rows, seq in groups:
        for b in range(n_rows):
            rows += [base + b] * (seq // tm)
        base += n_rows
    return jnp.asarray(np.asarray(rows, np.int32))


def kernel(x_prompt, x_sample, c_prompt, c_sample, w_ada, b_ada, g_norm_mix, g_norm_ff, w_in, w_fmix,
           g_out_f, g_out_a, w_out, w_ff1, w_ff3, w_ff2, w_router, w_e1, w_e3, w_e2, g_final):
    depth, d, d_in = w_in.shape
    groups = [(x_prompt.shape[0], x_prompt.shape[1]), (x_sample.shape[0], x_sample.shape[1])]
    group_row0 = [0, groups[0][0] * groups[0][1]]
    t = sum(b * s for b, s in groups)
    n_req = sum(b for b, _ in groups)
    d_attn = d - D_FOURIER
    n_heads = d_attn // HEAD_DIM
    n_experts = w_router.shape[-1]
    d_ff_e = w_e1.shape[-1]
    dilations = [dil for _, dil in DILATED_BRANCHES]

    x = jnp.concatenate([x_prompt.reshape(-1, d), x_sample.reshape(-1, d)], axis=0)
    c = jnp.concatenate([c_prompt, c_sample], axis=0)
    c = jnp.pad(c, ((0, -n_req % 8), (0, 0)))
    mod = ada_modulation(c, w_ada, b_ada)
    rows = _tile_rows(groups, TM)
    rows_c = _tile_rows(groups, TM_COMBINE)
    dense = Tiling(jnp.zeros((t // TM,), jnp.int32), rows, jnp.full((1,), t // TM, jnp.int32))
    slopes = jnp.exp2(-8.0 * (jnp.arange(n_heads, dtype=F32) + 1.0) / n_heads)
    ab = fourier_ab(w_fmix)
    dft = [dft_matrix(seq) for _, seq in groups]

    for l in range(depth):
        mod4 = mod[l].reshape(mod.shape[1], 6, 1, d)
        layer = lambda e, l=l: l
        h = norm_modulate(x, rows, g_norm_mix[l:l + 1], mod4, 0, tm=TM)
        u = matmul(h, w_in, dense, w_index=layer, tm=TM, tn=D_FOURIER, out_dtype=BF16, n=D_FOURIER)
        qkvs = matmul_qkv(h, w_in, dense, w_index=layer, tm=TM, tn=TM, col0=D_FOURIER, n=3 * d_attn,
                          dilations=dilations, q_cols=d_attn, q_scale=HEAD_DIM ** -0.5)

        group_inputs = []
        for gi, ((n_rows, seq), row0) in enumerate(zip(groups, group_row0)):
            pq = fourier_channel_stage(u, ab, l, n_rows=n_rows, seq=seq, row_block0=row0 // TM, tm=TM)
            f = fourier_sequence_stage(dft[gi], pq, n_rows=n_rows, seq=seq, tm=TM)
            branches = [attention_branch(qkv, slopes, window=window, dilation=dilation, n_rows=n_rows,
                                         seq=seq, row0=row0, n_heads=n_heads)
                        for qkv, (window, dilation) in zip(qkvs, DILATED_BRANCHES)]
            group_inputs.append((f, *[[b[k] for b in branches] for k in range(3)]))
        y = merge_and_norm(group_inputs, g_out_f[l:l + 1], g_out_a[l:l + 1], n_heads=n_heads)
        x = matmul_gated_residual(y, w_out, x, mod4, 2, dense, w_index=layer, tm=TM, tn=TM)

        j = l // 2
        if l % 2 == 0:
            h = norm_modulate(x, rows, g_norm_ff[l:l + 1], mod4, 3, tm=TM)
            act = matmul_swiglu(h, w_ff1, w_ff3, dense, w_index=lambda e, j=j: j, tm=TM, tn=TM)
            x = matmul_gated_residual(act, w_ff2, x, mod4, 5, dense, w_index=lambda e, j=j: j,
                                      tm=TM, tn=256)
        else:
            h, route = norm_modulate(x, rows, g_norm_ff[l:l + 1], mod4, 3, out_dtype=F32,
                                     w_router=w_router[j], tm=TM)
            row_token, pos, tile_expert, used = moe_dispatch_plan(route, n_experts, TM)
            routed = Tiling(tile_expert, tile_expert, used)
            xs = gather_rows(h, row_token, tm=TM)
            expert_w = lambda e, j=j: j * n_experts + e
            act = matmul_swiglu(xs, w_e1.reshape(-1, d, d_ff_e), w_e3.reshape(-1, d, d_ff_e), routed,
                                w_index=expert_w, tm=TM, tn=TM)
            o_sorted = matmul(act, w_e2.reshape(-1, d_ff_e, d), routed, w_index=expert_w,
                              tm=TM, tn=256, out_dtype=F32)
            pos_tiles = pos.reshape(t // TM_COMBINE, TM_COMBINE, TOP_K).transpose(0, 2, 1)
            pos_tiles = pos_tiles.reshape(t // TM_COMBINE, 1, TOP_K * TM_COMBINE)
            x = moe_combine(o_sorted, pos_tiles, route, x, mod4, 5, rows_c, tm=TM_COMBINE)

    g = g_final.reshape(1, d)
    outs = []
    for (n_rows, seq), row0 in zip(groups, group_row0):
        outs.append(final_norm(x, g, row0=row0, n_rows=n_rows * seq, tm=TM).reshape(n_rows, seq, d))
    return tuple(outs)
```

```python
import functools
import math

import numpy as np
import jax
import jax.numpy as jnp
from jax import lax
from jax.experimental import pallas as pl
from jax.experimental.pallas import tpu as pltpu

HEAD_DIM = 128
FOURIER_GROUP = 128
N_FOURIER_GROUPS = 4
D_FOURIER = FOURIER_GROUP * N_FOURIER_GROUPS
DILATED_BRANCHES = ((128, 1), (512, 4), (2048, 16))
TOP_K = 2
EPS = 1e-6
NEG_INF = -1e30
LOG2_E = math.log2(math.e)
LANES = 128
VMEM_LIMIT = 56 * 1024 * 1024
ATTN_BLOCK_BYTES = 4 * 1024 * 1024

BF16 = jnp.bfloat16
F32 = jnp.float32


def _params(n_axes, vmem=VMEM_LIMIT):
    return pltpu.CompilerParams(
        dimension_semantics=("arbitrary",) * n_axes, vmem_limit_bytes=vmem)


def _ada_kernel(c_ref, w_ref, b_ref, o_ref):
    c = c_ref[...]
    a = (c * jax.nn.sigmoid(c)).astype(BF16)
    acc = jnp.dot(a, w_ref[...].astype(BF16), preferred_element_type=F32)
    o_ref[...] = acc + b_ref[...]


def ada_modulation(c, w_ada, b_ada, tn=1024):
    n_layers, d, n = w_ada.shape
    r8 = c.shape[0]
    return pl.pallas_call(
        _ada_kernel,
        grid=(n_layers, n // tn),
        in_specs=[
            pl.BlockSpec((r8, d), lambda l, j: (0, 0)),
            pl.BlockSpec((None, d, tn), lambda l, j: (l, 0, j)),
            pl.BlockSpec((None, 1, tn), lambda l, j: (l, 0, j)),
        ],
        out_specs=pl.BlockSpec((None, r8, tn), lambda l, j: (l, 0, j)),
        out_shape=jax.ShapeDtypeStruct((n_layers, r8, n), F32),
        compiler_params=_params(2),
        name="ada_modulation",
    )(c, w_ada, b_ada.reshape(n_layers, 1, n))


def _norm_mod(x, g, sc, sh):
    y = x * lax.rsqrt(jnp.mean(x * x, axis=-1, keepdims=True) + EPS)
    return (y * g) * (1.0 + sc) + sh


def _norm_mod_kernel(rows_ref, x_ref, g_ref, sc_ref, sh_ref, h_ref):
    del rows_ref
    h_ref[...] = _norm_mod(x_ref[...], g_ref[...], sc_ref[...], sh_ref[...]).astype(h_ref.dtype)


def _norm_mod_router_kernel(rows_ref, x_ref, g_ref, sc_ref, sh_ref, wr_ref, h_ref, route_ref,
                            *, n_experts):
    del rows_ref
    h = _norm_mod(x_ref[...], g_ref[...], sc_ref[...], sh_ref[...])
    h_ref[...] = h.astype(h_ref.dtype)
    logits = jnp.dot(h, wr_ref[...], preferred_element_type=F32,
                     precision=lax.Precision.HIGHEST)
    lane = lax.broadcasted_iota(jnp.int32, logits.shape, 1)
    logits = jnp.where(lane < n_experts, logits, -jnp.inf)
    m1 = jnp.max(logits, axis=-1, keepdims=True)
    i1 = jnp.min(jnp.where(logits == m1, lane, LANES), axis=-1, keepdims=True)
    rest = jnp.where(lane == i1, -jnp.inf, logits)
    m2 = jnp.max(rest, axis=-1, keepdims=True)
    i2 = jnp.min(jnp.where(rest == m2, lane, LANES), axis=-1, keepdims=True)
    e2 = jnp.exp(m2 - m1)
    den = 1.0 + e2
    route = jnp.where(lane == 0, i1.astype(F32), 0.0)
    route = jnp.where(lane == 1, i2.astype(F32), route)
    route = jnp.where(lane == 2, 1.0 / den, route)
    route = jnp.where(lane == 3, e2 / den, route)
    route_ref[...] = route


def norm_modulate(x, rows, g, mod4, layer_part, *, out_dtype=BF16, w_router=None, tm=512):
    t, d = x.shape
    sh_part, sc_part = layer_part, layer_part + 1
    in_specs = [
        pl.BlockSpec((tm, d), lambda i, rows: (i, 0)),
        pl.BlockSpec((1, d), lambda i, rows: (0, 0)),
        pl.BlockSpec((None, None, 1, d), lambda i, rows: (rows[i], sc_part, 0, 0)),
        pl.BlockSpec((None, None, 1, d), lambda i, rows: (rows[i], sh_part, 0, 0)),
    ]
    h_spec = pl.BlockSpec((tm, d), lambda i, rows: (i, 0))
    h_shape = jax.ShapeDtypeStruct((t, d), out_dtype)
    if w_router is None:
        return pl.pallas_call(
            _norm_mod_kernel,
            grid_spec=pltpu.PrefetchScalarGridSpec(
                num_scalar_prefetch=1, grid=(t // tm,), in_specs=in_specs, out_specs=h_spec),
            out_shape=h_shape,
            compiler_params=_params(1),
            name="norm_modulate",
        )(rows, x, g, mod4, mod4)
    n_experts = w_router.shape[1]
    wr = jnp.pad(w_router, ((0, 0), (0, LANES - n_experts)))
    in_specs.append(pl.BlockSpec((d, LANES), lambda i, rows: (0, 0)))
    return pl.pallas_call(
        functools.partial(_norm_mod_router_kernel, n_experts=n_experts),
        grid_spec=pltpu.PrefetchScalarGridSpec(
            num_scalar_prefetch=1, grid=(t // tm,), in_specs=in_specs,
            out_specs=[h_spec, pl.BlockSpec((tm, LANES), lambda i, rows: (i, 0))]),
        out_shape=[h_shape, jax.ShapeDtypeStruct((t, LANES), F32)],
        compiler_params=_params(1),
        name="norm_modulate_router",
    )(rows, x, g, mod4, mod4, wr)


CAST_ROWS = 512


def _cast_weights_if_changed(te_ref, w_refs, wb_refs):
    i = pl.program_id(1)
    changed = jnp.logical_or(i == 0, te_ref[i] != te_ref[jnp.maximum(i - 1, 0)])

    @pl.when(changed)
    def _():
        k = w_refs[0].shape[0]
        step = math.gcd(k, CAST_ROWS)

        def body(c, carry):
            r0 = pl.multiple_of(c * step, step)
            for w_ref, wb_ref in zip(w_refs, wb_refs):
                wb_ref[pl.ds(r0, step), :] = w_ref[pl.ds(r0, step), :].astype(BF16)
            return carry

        lax.fori_loop(0, k // step, body, 0)


def _mm_plain_kernel(te_ref, rows_ref, used_ref, a_ref, w_ref, o_ref, wb_ref):
    del rows_ref
    _cast_weights_if_changed(te_ref, (w_ref,), (wb_ref,))

    @pl.when(pl.program_id(1) < used_ref[0])
    def _():
        acc = jnp.dot(a_ref[...], wb_ref[...], preferred_element_type=F32)
        o_ref[...] = acc.astype(o_ref.dtype)

    @pl.when(pl.program_id(1) >= used_ref[0])
    def _():
        o_ref[...] = jnp.zeros_like(o_ref)


def _mm_qkv_kernel(te_ref, rows_ref, used_ref, a_ref, w_ref, *rest, dilations, q_chunks, q_scale):
    del rows_ref, used_ref
    out_refs = rest[:len(dilations)]
    wb_ref, acc_ref = rest[len(dilations):]
    _cast_weights_if_changed(te_ref, (w_ref,), (wb_ref,))
    acc = jnp.dot(a_ref[...], wb_ref[...], preferred_element_type=F32)
    acc = acc * jnp.where(pl.program_id(0) < q_chunks, q_scale, 1.0).astype(F32)
    tm, tn = acc.shape
    slabs = tn // LANES
    for c in range(slabs):
        acc_ref[c] = acc[:, c * LANES:(c + 1) * LANES]
    for o_ref, d in zip(out_refs, dilations):
        if d == 1:
            o_ref[0] = acc.astype(o_ref.dtype)
            continue
        for r in range(d):
            for c in range(slabs):
                o_ref[r, :, c * LANES:(c + 1) * LANES] = (
                    acc_ref[c, pl.ds(r, tm // d, stride=d), :].astype(o_ref.dtype))


def _mm_swiglu_kernel(te_ref, rows_ref, used_ref, a_ref, w1_ref, w3_ref, o_ref, wb1_ref, wb3_ref):
    del rows_ref
    _cast_weights_if_changed(te_ref, (w1_ref, w3_ref), (wb1_ref, wb3_ref))

    @pl.when(pl.program_id(1) < used_ref[0])
    def _():
        a = a_ref[...]
        g = jnp.dot(a, wb1_ref[...], preferred_element_type=F32)
        u = jnp.dot(a, wb3_ref[...], preferred_element_type=F32)
        o_ref[...] = ((g * jax.nn.sigmoid(g)) * u).astype(o_ref.dtype)

    @pl.when(pl.program_id(1) >= used_ref[0])
    def _():
        o_ref[...] = jnp.zeros_like(o_ref)


def _mm_resid_kernel(te_ref, rows_ref, used_ref, a_ref, w_ref, x_ref, gate_ref, o_ref, wb_ref):
    del rows_ref, used_ref
    _cast_weights_if_changed(te_ref, (w_ref,), (wb_ref,))
    acc = jnp.dot(a_ref[...], wb_ref[...], preferred_element_type=F32)
    o_ref[...] = x_ref[...] + gate_ref[...] * acc


def _a_spec(tm, k):
    return pl.BlockSpec((tm, k), lambda j, i, te, rows, used: (jnp.minimum(i, used[0] - 1), 0))


def _w_spec(k, tn, w_index, col_block0=0, single_buffer=False):
    mode = dict(pipeline_mode=pl.Buffered(1)) if single_buffer else {}
    return pl.BlockSpec((None, k, tn),
                        lambda j, i, te, rows, used: (w_index(te[i]), 0, col_block0 + j), **mode)


def _out_spec(tm, tn):
    return pl.BlockSpec((tm, tn), lambda j, i, te, rows, used: (i, j))


class Tiling:
    def __init__(self, tile_expert, rows, used):
        self.args = (tile_expert, rows, used)


def matmul_qkv(a, w, tiling, *, w_index, tm, tn, col0, n, dilations, q_cols, q_scale):
    m, k = a.shape
    return pl.pallas_call(
        functools.partial(_mm_qkv_kernel, dilations=tuple(dilations), q_chunks=q_cols // tn,
                          q_scale=q_scale),
        grid_spec=pltpu.PrefetchScalarGridSpec(
            num_scalar_prefetch=3, grid=(n // tn, m // tm),
            in_specs=[_a_spec(tm, k), _w_spec(k, tn, w_index, col0 // tn)],
            out_specs=[pl.BlockSpec((d, tm // d, tn), lambda j, i, te, rows, used: (0, i, j))
                       for d in dilations],
            scratch_shapes=[pltpu.VMEM((k, tn), BF16), pltpu.VMEM((tn // LANES, tm, LANES), F32)]),
        out_shape=[jax.ShapeDtypeStruct((d, m // d, n), BF16) for d in dilations],
        compiler_params=_params(2),
        name="matmul_qkv",
    )(*tiling.args, a, w)


def matmul(a, w, tiling, *, w_index, tm, tn, out_dtype, n=None, single_buffer_w=False):
    m, k = a.shape
    n = w.shape[-1] if n is None else n
    return pl.pallas_call(
        _mm_plain_kernel,
        grid_spec=pltpu.PrefetchScalarGridSpec(
            num_scalar_prefetch=3, grid=(n // tn, m // tm),
            in_specs=[_a_spec(tm, k), _w_spec(k, tn, w_index, single_buffer=single_buffer_w)],
            out_specs=_out_spec(tm, tn),
            scratch_shapes=[pltpu.VMEM((k, tn), BF16)]),
        out_shape=jax.ShapeDtypeStruct((m, n), out_dtype),
        compiler_params=_params(2),
        name="matmul",
    )(*tiling.args, a, w)


def matmul_swiglu(a, w1, w3, tiling, *, w_index, tm, tn):
    m, k = a.shape
    n = w1.shape[-1]
    return pl.pallas_call(
        _mm_swiglu_kernel,
        grid_spec=pltpu.PrefetchScalarGridSpec(
            num_scalar_prefetch=3, grid=(n // tn, m // tm),
            in_specs=[_a_spec(tm, k), _w_spec(k, tn, w_index), _w_spec(k, tn, w_index)],
            out_specs=_out_spec(tm, tn),
            scratch_shapes=[pltpu.VMEM((k, tn), BF16), pltpu.VMEM((k, tn), BF16)]),
        out_shape=jax.ShapeDtypeStruct((m, n), BF16),
        compiler_params=_params(2),
        name="matmul_swiglu",
    )(*tiling.args, a, w1, w3)


def matmul_gated_residual(a, w, x, mod4, gate_part, tiling, *, w_index, tm, tn):
    m, k = a.shape
    n = w.shape[-1]
    return pl.pallas_call(
        _mm_resid_kernel,
        grid_spec=pltpu.PrefetchScalarGridSpec(
            num_scalar_prefetch=3, grid=(n // tn, m // tm),
            in_specs=[_a_spec(tm, k), _w_spec(k, tn, w_index), _out_spec(tm, tn),
                      pl.BlockSpec((None, None, 1, tn),
                                   lambda j, i, te, rows, used: (rows[i], gate_part, 0, j))],
            out_specs=_out_spec(tm, tn),
            scratch_shapes=[pltpu.VMEM((k, tn), BF16)]),
        out_shape=jax.ShapeDtypeStruct((m, n), F32),
        compiler_params=_params(2),
        name="matmul_gated_residual",
    )(*tiling.args, a, w, x, mod4)


def _dft_cos_sin(n):
    j = lax.broadcasted_iota(jnp.int32, (n, n), 0)
    k = lax.broadcasted_iota(jnp.int32, (n, n), 1)
    ang = ((j * k) % n).astype(F32) * (2.0 * math.pi / n)
    return jnp.cos(ang), jnp.sin(ang)


def _fourier_ab_kernel(c_ref, s_ref, w_ref, o_ref):
    w = w_ref[...]
    scale = FOURIER_GROUP ** -0.5
    a = jnp.dot(c_ref[...], w, preferred_element_type=F32, precision=lax.Precision.HIGHEST)
    b = jnp.dot(s_ref[...], w, preferred_element_type=F32, precision=lax.Precision.HIGHEST)
    o_ref[:, :FOURIER_GROUP] = (a * scale).astype(o_ref.dtype)
    o_ref[:, FOURIER_GROUP:] = (b * scale).astype(o_ref.dtype)


def fourier_ab(w_fmix):
    n_layers, n_groups, c, _ = w_fmix.shape
    cos_g, sin_g = _dft_cos_sin(c)
    return pl.pallas_call(
        _fourier_ab_kernel,
        grid=(n_layers, n_groups),
        in_specs=[pl.BlockSpec((c, c), lambda l, g: (0, 0)),
                  pl.BlockSpec((c, c), lambda l, g: (0, 0)),
                  pl.BlockSpec((None, None, c, c), lambda l, g: (l, g, 0, 0))],
        out_specs=pl.BlockSpec((None, None, c, 2 * c), lambda l, g: (l, g, 0, 0)),
        out_shape=jax.ShapeDtypeStruct((n_layers, n_groups, c, 2 * c), BF16),
        compiler_params=_params(2),
        name="fourier_ab",
    )(cos_g, sin_g, w_fmix)


def _fourier_channel_kernel(u_ref, ab_ref, pq_ref):
    c = FOURIER_GROUP
    for g in range(N_FOURIER_GROUPS):
        pq = jnp.dot(u_ref[:, g * c:(g + 1) * c], ab_ref[g], preferred_element_type=F32)
        pq_ref[0, :, g * c:(g + 1) * c] = pq[:, :c].astype(pq_ref.dtype)
        pq_ref[1, :, g * c:(g + 1) * c] = pq[:, c:].astype(pq_ref.dtype)


def fourier_channel_stage(u, ab, layer, *, n_rows, seq, row_block0, tm=512):
    tiles = seq // tm
    return pl.pallas_call(
        _fourier_channel_kernel,
        grid=(n_rows, tiles),
        in_specs=[pl.BlockSpec((tm, D_FOURIER), lambda b, i: (row_block0 + b * tiles + i, 0)),
                  pl.BlockSpec((None, N_FOURIER_GROUPS, FOURIER_GROUP, 2 * FOURIER_GROUP),
                               lambda b, i: (layer, 0, 0, 0))],
        out_specs=pl.BlockSpec((None, 2, tm, D_FOURIER), lambda b, i: (b, 0, i, 0)),
        out_shape=jax.ShapeDtypeStruct((n_rows, 2, seq, D_FOURIER), BF16),
        compiler_params=_params(2),
        name="fourier_channel_stage",
    )(u, ab)


def _fourier_seq_kernel(cs_ref, pq_ref, o_ref, *, scale):
    o_ref[...] = jnp.dot(cs_ref[...], pq_ref[...], preferred_element_type=F32) * scale


def fourier_sequence_stage(cs, pq, *, n_rows, seq, tm=512):
    tiles = seq // tm
    return pl.pallas_call(
        functools.partial(_fourier_seq_kernel, scale=seq ** -0.5),
        grid=(n_rows, tiles),
        in_specs=[pl.BlockSpec((tm, 2 * seq), lambda b, i: (i, 0)),
                  pl.BlockSpec((None, 2 * seq, D_FOURIER), lambda b, i: (b, 0, 0))],
        out_specs=pl.BlockSpec((tm, D_FOURIER), lambda b, i: (b * tiles + i, 0)),
        out_shape=jax.ShapeDtypeStruct((n_rows * seq, D_FOURIER), F32),
        compiler_params=_params(2),
        name="fourier_sequence_stage",
    )(cs, pq.reshape(n_rows, 2 * seq, D_FOURIER))


DFT_SPLIT = 64


def _dft_kernel(t1_ref, t2_ref, o_ref):
    seq = t2_ref.shape[2]
    c1, s1 = t1_ref[0], t1_ref[1]
    c2, s2 = t2_ref[0], t2_ref[1]
    o_ref[:, :seq] = (c1 * c2 - s1 * s2).astype(o_ref.dtype)
    o_ref[:, seq:] = (-(s1 * c2 + c1 * s2)).astype(o_ref.dtype)


def dft_matrix(seq):
    k = jnp.arange(seq, dtype=jnp.int32)

    def table(mult):
        ang = ((mult[:, None] * k[None, :]) % seq).astype(F32) * (2.0 * math.pi / seq)
        return jnp.stack([jnp.cos(ang), jnp.sin(ang)])

    n_coarse = seq // DFT_SPLIT
    t1 = table(jnp.arange(n_coarse, dtype=jnp.int32) * DFT_SPLIT).reshape(2, n_coarse, 1, seq)
    t2 = table(jnp.arange(DFT_SPLIT, dtype=jnp.int32))
    return pl.pallas_call(
        _dft_kernel,
        grid=(n_coarse,),
        in_specs=[pl.BlockSpec((2, None, 1, seq), lambda a: (0, a, 0, 0)),
                  pl.BlockSpec((2, DFT_SPLIT, seq), lambda a: (0, 0, 0))],
        out_specs=pl.BlockSpec((DFT_SPLIT, 2 * seq), lambda a: (a, 0)),
        out_shape=jax.ShapeDtypeStruct((seq, 2 * seq), BF16),
        compiler_params=_params(1),
        name="dft_matrix",
    )(t1, t2)


def _attn_kernel(slopes_ref, q_ref, k_ref, v_ref, o_ref, lse_ref, bias_ref, *, dilation, n_side,
                 bq, bk, hps):
    sub = q_ref.shape[0]
    n_cases = bias_ref.shape[0]
    lane = lax.broadcasted_iota(jnp.int32, (bq, LANES), 1)

    @pl.when(jnp.logical_and(pl.program_id(1) == 0, pl.program_id(2) == 0))
    def _():
        row = lax.broadcasted_iota(jnp.int32, (bq, bk), 0)
        col = lax.broadcasted_iota(jnp.int32, (bq, bk), 1)
        for case in range(n_cases):
            dist = jnp.abs(col - row - case * n_side)
            valid = dist <= n_side
            dist_f = dist.astype(F32) * float(dilation)
            for hh in range(hps):
                slope = slopes_ref[pl.program_id(0) * hps + hh] * LOG2_E
                bias_ref[case, hh] = jnp.where(valid, -slope * dist_f, NEG_INF)

    def body(qi, carry):
        q0 = pl.multiple_of(qi * bq, bq)
        ks = pl.multiple_of(jnp.clip(q0 - n_side, 0, sub - bk), n_side)
        case = (q0 - ks) // n_side
        lse_tile = jnp.zeros((bq, LANES), F32)
        for hh in range(hps):
            cols = slice(hh * HEAD_DIM, (hh + 1) * HEAD_DIM)
            q = q_ref[pl.ds(q0, bq), cols]
            k = k_ref[pl.ds(ks, bk), cols]
            v = v_ref[pl.ds(ks, bk), cols]
            s = lax.dot_general(q, k, (((1,), (1,)), ((), ())), preferred_element_type=F32)
            s = s + bias_ref[case, hh]
            m = jnp.max(s, axis=-1, keepdims=True)
            p = jnp.exp2(s - m)
            den = jnp.sum(p, axis=-1, keepdims=True)
            num = jnp.dot(p.astype(BF16), v, preferred_element_type=F32)
            o_ref[pl.ds(q0, bq), cols] = (num / den).astype(o_ref.dtype)
            lse_tile = jnp.where(lane == pl.program_id(0) * hps + hh, m + jnp.log2(den), lse_tile)
        lse_ref[pl.ds(q0, bq), :] = lse_tile
        return carry

    lax.fori_loop(0, sub // bq, body, 0)


def heads_per_step(sub, n_heads):
    fits = [h for h in range(1, n_heads + 1)
            if n_heads % h == 0 and sub * h * HEAD_DIM * 2 <= ATTN_BLOCK_BYTES]
    return max(fits)


def attention_branch(qkv, slopes, *, window, dilation, n_rows, seq, row0, n_heads):
    tg = n_rows * seq
    sub = seq // dilation
    n_side = (window // 2) // dilation
    bq = min(128, sub)
    bk = min(bq + 2 * n_side, sub)
    n_cases = 1 if sub == bq else 3
    hps = heads_per_step(sub, n_heads)
    gw = hps * HEAD_DIM
    groups = n_heads // hps
    d_attn = n_heads * HEAD_DIM
    rb0 = row0 // seq

    def in_map(part):
        return lambda g, b, r, slopes: (r, rb0 + b, part * groups + g)

    out_map = lambda g, b, r, slopes: (r, b, g)
    o, lse = pl.pallas_call(
        functools.partial(_attn_kernel, dilation=dilation, n_side=n_side, bq=bq, bk=bk, hps=hps),
        grid_spec=pltpu.PrefetchScalarGridSpec(
            num_scalar_prefetch=1, grid=(groups, n_rows, dilation),
            in_specs=[pl.BlockSpec((None, sub, gw), in_map(part)) for part in range(3)],
            out_specs=[pl.BlockSpec((None, sub, gw), out_map),
                       pl.BlockSpec((None, sub, LANES), out_map)],
            scratch_shapes=[pltpu.VMEM((n_cases, hps, bq, bk), F32)]),
        out_shape=[jax.ShapeDtypeStruct((dilation, tg // dilation, d_attn), BF16),
                   jax.ShapeDtypeStruct((dilation, tg // dilation, groups * LANES), F32)],
        compiler_params=_params(3),
        name=f"attention_d{dilation}",
    )(slopes, qkv, qkv, qkv)
    return o, lse


def _rms(x, g):
    return x * lax.rsqrt(jnp.mean(x * x, axis=-1, keepdims=True) + EPS) * g


def _merge_tile(f_ref, o_refs, l_refs, gf_ref, ga_ref, y_ref, ya_ref, on_ref, ln_ref, n_heads):
    tm = y_ref.shape[0]
    lses = []
    for bi, (o_ref, l_ref) in enumerate(zip(o_refs, l_refs)):
        d = o_ref.shape[0]
        blocks = l_ref.shape[2] // LANES
        if d == 1:
            parts = [l_ref[0, :, g * LANES:(g + 1) * LANES] for g in range(blocks)]
        else:
            for r in range(d):
                rows = pl.ds(r, tm // d, stride=d)
                for h in range(n_heads):
                    on_ref[bi, h, rows, :] = o_ref[r, :, h * HEAD_DIM:(h + 1) * HEAD_DIM].astype(F32)
                for g in range(blocks):
                    ln_ref[bi, g, rows, :] = l_ref[r, :, g * LANES:(g + 1) * LANES]
            parts = [ln_ref[bi, g] for g in range(blocks)]
        lses.append(functools.reduce(lambda a, b: a + b, parts))

    top = functools.reduce(jnp.maximum, lses)
    ws = [jnp.exp2(l - top) for l in lses]
    inv = 1.0 / functools.reduce(lambda a, b: a + b, ws)
    ws = [w * inv for w in ws]
    for h in range(n_heads):
        cols = slice(h * HEAD_DIM, (h + 1) * HEAD_DIM)
        acc = None
        for bi, o_ref in enumerate(o_refs):
            o = o_ref[0, :, cols].astype(F32) if o_ref.shape[0] == 1 else on_ref[bi, h]
            term = o * ws[bi][:, h:h + 1]
            acc = term if acc is None else acc + term
        ya_ref[:, cols] = acc
    y_ref[:, :D_FOURIER] = _rms(f_ref[...], gf_ref[...]).astype(y_ref.dtype)
    y_ref[:, D_FOURIER:] = _rms(ya_ref[...], ga_ref[...]).astype(y_ref.dtype)


def _merge_kernel(*refs, n_heads, group_tiles):
    n_br = len(DILATED_BRANCHES)
    per_group = 1 + 2 * n_br
    n_groups = len(group_tiles)
    gf_ref, ga_ref, y_ref, ya_ref, on_ref, ln_ref = refs[n_groups * per_group:]
    i = pl.program_id(0)
    tile0 = 0
    for gi, tiles in enumerate(group_tiles):
        grp = refs[gi * per_group:(gi + 1) * per_group]

        @pl.when(jnp.logical_and(i >= tile0, i < tile0 + tiles))
        def _(grp=grp):
            _merge_tile(grp[0], grp[1:1 + n_br], grp[1 + n_br:], gf_ref, ga_ref, y_ref, ya_ref,
                        on_ref, ln_ref, n_heads)

        tile0 += tiles


def merge_and_norm(group_inputs, g_out_f, g_out_a, *, n_heads, tm=256):
    d_attn = n_heads * HEAD_DIM
    group_tiles = [g[0].shape[0] // tm for g in group_inputs]
    n_tiles = sum(group_tiles)
    n_br = len(DILATED_BRANCHES)
    in_specs, args = [], []
    tile0 = 0
    for (f, os_, ls_), tiles in zip(group_inputs, group_tiles):
        local = lambda i, tile0=tile0, tiles=tiles: jnp.clip(i - tile0, 0, tiles - 1)
        in_specs.append(pl.BlockSpec((tm, D_FOURIER), lambda i, local=local: (local(i), 0)))
        for arr in (*os_, *ls_):
            d = arr.shape[0]
            in_specs.append(pl.BlockSpec((d, tm // d, arr.shape[2]),
                                         lambda i, local=local: (0, local(i), 0)))
        args += [f, *os_, *ls_]
        tile0 += tiles
    fixed = lambda i: (0, 0)
    in_specs += [pl.BlockSpec((1, D_FOURIER), fixed), pl.BlockSpec((1, d_attn), fixed)]
    max_groups = max(l.shape[2] // LANES for g in group_inputs for l in g[2])
    return pl.pallas_call(
        functools.partial(_merge_kernel, n_heads=n_heads, group_tiles=tuple(group_tiles)),
        grid=(n_tiles,),
        in_specs=in_specs,
        out_specs=pl.BlockSpec((tm, D_FOURIER + d_attn), lambda i: (i, 0)),
        out_shape=jax.ShapeDtypeStruct((n_tiles * tm, D_FOURIER + d_attn), BF16),
        scratch_shapes=[pltpu.VMEM((tm, d_attn), F32),
                        pltpu.VMEM((n_br, n_heads, tm, HEAD_DIM), F32),
                        pltpu.VMEM((n_br, max_groups, tm, LANES), F32)],
        compiler_params=_params(1),
        name="merge_and_norm",
    )(*args, g_out_f, g_out_a)


def _gather_kernel(tok_ref, h_ref, o_ref, buf_ref, sem):
    tm = buf_ref.shape[0]

    def copy(r):
        return pltpu.make_async_copy(h_ref.at[pl.ds(tok_ref[0, r], 1)], buf_ref.at[pl.ds(r, 1)], sem)

    def start(r, carry):
        copy(r).start()
        return carry

    def wait(r, carry):
        copy(r).wait()
        return carry

    lax.fori_loop(0, tm, start, 0, unroll=8)
    lax.fori_loop(0, tm, wait, 0, unroll=8)
    o_ref[...] = buf_ref[...].astype(o_ref.dtype)


def gather_rows(h, row_token, *, tm):
    p = row_token.shape[0]
    d = h.shape[1]
    return pl.pallas_call(
        _gather_kernel,
        grid=(p // tm,),
        in_specs=[pl.BlockSpec((None, 1, tm), lambda i: (i, 0, 0), memory_space=pltpu.SMEM),
                  pl.BlockSpec(memory_space=pl.ANY)],
        out_specs=pl.BlockSpec((tm, d), lambda i: (i, 0)),
        out_shape=jax.ShapeDtypeStruct((p, d), BF16),
        scratch_shapes=[pltpu.VMEM((tm, d), F32), pltpu.SemaphoreType.DMA(())],
        compiler_params=_params(1),
        name="moe_gather",
    )(row_token.reshape(p // tm, 1, tm), h)


def _combine_kernel(rows_ref, pos_ref, o_ref, route_ref, x_ref, gate_ref, out_ref, buf_ref, sem):
    del rows_ref
    tm = x_ref.shape[0]

    def copy(r):
        return pltpu.make_async_copy(o_ref.at[pl.ds(pos_ref[0, r], 1)],
                                     buf_ref.at[pl.ds(r, 1)], sem)

    def start(r, carry):
        copy(r).start()
        return carry

    def wait(r, carry):
        copy(r).wait()
        return carry

    lax.fori_loop(0, TOP_K * tm, start, 0, unroll=8)
    lax.fori_loop(0, TOP_K * tm, wait, 0, unroll=8)
    route = route_ref[...]
    y = route[:, 2:3] * buf_ref[pl.ds(0, tm), :] + route[:, 3:4] * buf_ref[pl.ds(tm, tm), :]
    out_ref[...] = x_ref[...] + gate_ref[...] * y


def moe_combine(o_sorted, pos, route, x, mod4, gate_part, rows, *, tm):
    t, d = x.shape
    return pl.pallas_call(
        _combine_kernel,
        grid_spec=pltpu.PrefetchScalarGridSpec(
            num_scalar_prefetch=1, grid=(t // tm,),
            in_specs=[pl.BlockSpec((None, 1, TOP_K * tm), lambda i, rows: (i, 0, 0),
                                   memory_space=pltpu.SMEM),
                      pl.BlockSpec(memory_space=pl.ANY),
                      pl.BlockSpec((tm, LANES), lambda i, rows: (i, 0)),
                      pl.BlockSpec((tm, d), lambda i, rows: (i, 0)),
                      pl.BlockSpec((None, None, 1, d), lambda i, rows: (rows[i], gate_part, 0, 0))],
            out_specs=pl.BlockSpec((tm, d), lambda i, rows: (i, 0)),
            scratch_shapes=[pltpu.VMEM((TOP_K * tm, d), F32), pltpu.SemaphoreType.DMA(())]),
        out_shape=jax.ShapeDtypeStruct((t, d), F32),
        compiler_params=_params(1),
        name="moe_combine",
    )(rows, pos, o_sorted, route, x, mod4)


def moe_dispatch_plan(route, n_experts, tm):
    t = route.shape[0]
    n_slots = t * TOP_K
    p = n_slots + n_experts * tm
    expert = route[:, :TOP_K].astype(jnp.int32).reshape(n_slots)
    onehot = (expert[:, None] == jnp.arange(n_experts)[None, :]).astype(jnp.int32)
    rank = jnp.take_along_axis(jnp.cumsum(onehot, axis=0) - onehot, expert[:, None], axis=1)[:, 0]
    counts = jnp.sum(onehot, axis=0)
    padded = ((counts + tm - 1) // tm) * tm
    ends = jnp.cumsum(padded)
    starts = ends - padded
    pos = starts[expert] + rank
    row_token = jnp.zeros((p,), jnp.int32).at[pos].set(jnp.arange(n_slots, dtype=jnp.int32) // TOP_K)
    tile_start = jnp.arange(p // tm, dtype=jnp.int32) * tm
    tile_expert = jnp.sum((tile_start[:, None] >= ends[None, :]).astype(jnp.int32), axis=1)
    tile_expert = jnp.minimum(tile_expert, n_experts - 1)
    used = (ends[-1:] // tm).astype(jnp.int32)
    return row_token, pos.reshape(t, TOP_K), tile_expert, used


def _final_norm_kernel(x_ref, g_ref, o_ref):
    o_ref[...] = _rms(x_ref[...], g_ref[...])


def final_norm(x, g, *, row0, n_rows, tm=512):
    d = x.shape[1]
    blk0 = row0 // tm
    return pl.pallas_call(
        _final_norm_kernel,
        grid=(n_rows // tm,),
        in_specs=[pl.BlockSpec((tm, d), lambda i: (blk0 + i, 0)),
                  pl.BlockSpec((1, d), lambda i: (0, 0))],
        out_specs=pl.BlockSpec((tm, d), lambda i: (i, 0)),
        out_shape=jax.ShapeDtypeStruct((n_rows, d), F32),
        compiler_params=_params(1),
        name="final_norm",
    )(x, g)


TM = 512
TM_BIG = 1024
TM_COMBINE = 256


def _tile_rows(groups, tm):
    rows = []
    base = 0
    for n_rows, seq in groups:
        for b in range(n_rows):
            rows += [base + b] * (seq // tm)
        base += n_rows
    return jnp.asarray(np.asarray(rows, np.int32))


def kernel(x_prompt, x_sample, c_prompt, c_sample, w_ada, b_ada, g_norm_mix, g_norm_ff, w_in, w_fmix,
           g_out_f, g_out_a, w_out, w_ff1, w_ff3, w_ff2, w_router, w_e1, w_e3, w_e2, g_final):
    depth, d, d_in = w_in.shape
    groups = [(x_prompt.shape[0], x_prompt.shape[1]), (x_sample.shape[0], x_sample.shape[1])]
    group_row0 = [0, groups[0][0] * groups[0][1]]
    t = sum(b * s for b, s in groups)
    n_req = sum(b for b, _ in groups)
    d_attn = d - D_FOURIER
    n_heads = d_attn // HEAD_DIM
    n_experts = w_router.shape[-1]
    d_ff_e = w_e1.shape[-1]
    dilations = [dil for _, dil in DILATED_BRANCHES]

    x = jnp.concatenate([x_prompt.reshape(-1, d), x_sample.reshape(-1, d)], axis=0)
    c = jnp.concatenate([c_prompt, c_sample], axis=0)
    c = jnp.pad(c, ((0, -n_req % 8), (0, 0)))
    mod = ada_modulation(c, w_ada, b_ada)
    rows = _tile_rows(groups, TM)
    rows_c = _tile_rows(groups, TM_COMBINE)
    dense = Tiling(jnp.zeros((t // TM,), jnp.int32), rows, jnp.full((1,), t // TM, jnp.int32))
    dense_big = Tiling(jnp.zeros((t // TM_BIG,), jnp.int32), _tile_rows(groups, TM_BIG),
                       jnp.full((1,), t // TM_BIG, jnp.int32))
    slopes = jnp.exp2(-8.0 * (jnp.arange(n_heads, dtype=F32) + 1.0) / n_heads)
    ab = fourier_ab(w_fmix)
    dft = [dft_matrix(seq) for _, seq in groups]

    for l in range(depth):
        mod4 = mod[l].reshape(mod.shape[1], 6, 1, d)
        layer = lambda e, l=l: l
        h = norm_modulate(x, rows, g_norm_mix[l:l + 1], mod4, 0, tm=TM)
        u = matmul(h, w_in, dense_big, w_index=layer, tm=TM_BIG, tn=D_FOURIER, out_dtype=BF16,
                   n=D_FOURIER)
        qkvs = matmul_qkv(h, w_in, dense_big, w_index=layer, tm=TM_BIG, tn=512, col0=D_FOURIER,
                          n=3 * d_attn, dilations=dilations, q_cols=d_attn,
                          q_scale=HEAD_DIM ** -0.5 * LOG2_E)

        group_inputs = []
        for gi, ((n_rows, seq), row0) in enumerate(zip(groups, group_row0)):
            pq = fourier_channel_stage(u, ab, l, n_rows=n_rows, seq=seq, row_block0=row0 // TM, tm=TM)
            f = fourier_sequence_stage(dft[gi], pq, n_rows=n_rows, seq=seq, tm=TM)
            branches = [attention_branch(qkv, slopes, window=window, dilation=dilation, n_rows=n_rows,
                                         seq=seq, row0=row0, n_heads=n_heads)
                        for qkv, (window, dilation) in zip(qkvs, DILATED_BRANCHES)]
            group_inputs.append((f, [b[0] for b in branches], [b[1] for b in branches]))
        y = merge_and_norm(group_inputs, g_out_f[l:l + 1], g_out_a[l:l + 1], n_heads=n_heads)
        x = matmul_gated_residual(y, w_out, x, mod4, 2, dense_big, w_index=layer, tm=TM_BIG, tn=512)

        j = l // 2
        if l % 2 == 0:
            h = norm_modulate(x, rows, g_norm_ff[l:l + 1], mod4, 3, tm=TM)
            act = matmul_swiglu(h, w_ff1, w_ff3, dense_big, w_index=lambda e, j=j: j, tm=TM_BIG, tn=512)
            x = matmul_gated_residual(act, w_ff2, x, mod4, 5, dense, w_index=lambda e, j=j: j,
                                      tm=TM, tn=512)
        else:
            h, route = norm_modulate(x, rows, g_norm_ff[l:l + 1], mod4, 3, out_dtype=F32,
                                     w_router=w_router[j], tm=TM)
            row_token, pos, tile_expert, used = moe_dispatch_plan(route, n_experts, TM)
            routed = Tiling(tile_expert, tile_expert, used)
            xs = gather_rows(h, row_token, tm=TM)
            expert_w = lambda e, j=j: j * n_experts + e
            act = matmul_swiglu(xs, w_e1.reshape(-1, d, d_ff_e), w_e3.reshape(-1, d, d_ff_e), routed,
                                w_index=expert_w, tm=TM, tn=TM)
            o_sorted = matmul(act, w_e2.reshape(-1, d_ff_e, d), routed, w_index=expert_w,
                              tm=TM, tn=512, out_dtype=F32, single_buffer_w=True)
            pos_tiles = pos.reshape(t // TM_COMBINE, TM_COMBINE, TOP_K).transpose(0, 2, 1)
            pos_tiles = pos_tiles.reshape(t // TM_COMBINE, 1, TOP_K * TM_COMBINE)
            x = moe_combine(o_sorted, pos_tiles, route, x, mod4, 5, rows_c, tm=TM_COMBINE)

    g = g_final.reshape(1, d)
    outs = []
    for (n_rows, seq), row0 in zip(groups, group_row0):
        outs.append(final_norm(x, g, row0=row0, n_rows=n_rows * seq, tm=TM).reshape(n_rows, seq, d))
    return tuple(outs)
```

```python
import functools
import math

import numpy as np
import jax
import jax.numpy as jnp
from jax import lax
from jax.experimental import pallas as pl
from jax.experimental.pallas import tpu as pltpu

HEAD_DIM = 128
FOURIER_GROUP = 128
N_FOURIER_GROUPS = 4
D_FOURIER = FOURIER_GROUP * N_FOURIER_GROUPS
DILATED_BRANCHES = ((128, 1), (512, 4), (2048, 16))
TOP_K = 2
EPS = 1e-6
NEG_INF = -1e30
LOG2_E = math.log2(math.e)
LANES = 128
VMEM_LIMIT = 56 * 1024 * 1024
ATTN_BLOCK_BYTES = 4 * 1024 * 1024

BF16 = jnp.bfloat16
F32 = jnp.float32


def _params(n_axes, vmem=VMEM_LIMIT):
    return pltpu.CompilerParams(
        dimension_semantics=("arbitrary",) * n_axes, vmem_limit_bytes=vmem)


def _ada_kernel(c_ref, w_ref, b_ref, o_ref):
    c = c_ref[...]
    a = (c * jax.nn.sigmoid(c)).astype(BF16)
    acc = jnp.dot(a, w_ref[...].astype(BF16), preferred_element_type=F32)
    o_ref[...] = acc + b_ref[...]


def ada_modulation(c, w_ada, b_ada, tn=1024):
    n_layers, d, n = w_ada.shape
    r8 = c.shape[0]
    return pl.pallas_call(
        _ada_kernel,
        grid=(n_layers, n // tn),
        in_specs=[
            pl.BlockSpec((r8, d), lambda l, j: (0, 0)),
            pl.BlockSpec((None, d, tn), lambda l, j: (l, 0, j)),
            pl.BlockSpec((None, 1, tn), lambda l, j: (l, 0, j)),
        ],
        out_specs=pl.BlockSpec((None, r8, tn), lambda l, j: (l, 0, j)),
        out_shape=jax.ShapeDtypeStruct((n_layers, r8, n), F32),
        compiler_params=_params(2),
        name="ada_modulation",
    )(c, w_ada, b_ada.reshape(n_layers, 1, n))


def _norm_mod(x, g, sc, sh):
    y = x * lax.rsqrt(jnp.mean(x * x, axis=-1, keepdims=True) + EPS)
    return (y * g) * (1.0 + sc) + sh


def _norm_mod_kernel(rows_ref, x_ref, g_ref, sc_ref, sh_ref, h_ref):
    del rows_ref
    h_ref[...] = _norm_mod(x_ref[...], g_ref[...], sc_ref[...], sh_ref[...]).astype(h_ref.dtype)


def _norm_mod_router_kernel(rows_ref, x_ref, g_ref, sc_ref, sh_ref, wr_ref, h_ref, route_ref,
                            *, n_experts):
    del rows_ref
    h = _norm_mod(x_ref[...], g_ref[...], sc_ref[...], sh_ref[...])
    h_ref[...] = h.astype(h_ref.dtype)
    logits = jnp.dot(h, wr_ref[...], preferred_element_type=F32,
                     precision=lax.Precision.HIGHEST)
    lane = lax.broadcasted_iota(jnp.int32, logits.shape, 1)
    logits = jnp.where(lane < n_experts, logits, -jnp.inf)
    m1 = jnp.max(logits, axis=-1, keepdims=True)
    i1 = jnp.min(jnp.where(logits == m1, lane, LANES), axis=-1, keepdims=True)
    rest = jnp.where(lane == i1, -jnp.inf, logits)
    m2 = jnp.max(rest, axis=-1, keepdims=True)
    i2 = jnp.min(jnp.where(rest == m2, lane, LANES), axis=-1, keepdims=True)
    e2 = jnp.exp(m2 - m1)
    den = 1.0 + e2
    route = jnp.where(lane == 0, i1.astype(F32), 0.0)
    route = jnp.where(lane == 1, i2.astype(F32), route)
    route = jnp.where(lane == 2, 1.0 / den, route)
    route = jnp.where(lane == 3, e2 / den, route)
    route_ref[...] = route


def norm_modulate(x, rows, g, mod4, layer_part, *, out_dtype=BF16, w_router=None, tm=512):
    t, d = x.shape
    sh_part, sc_part = layer_part, layer_part + 1
    in_specs = [
        pl.BlockSpec((tm, d), lambda i, rows: (i, 0)),
        pl.BlockSpec((1, d), lambda i, rows: (0, 0)),
        pl.BlockSpec((None, None, 1, d), lambda i, rows: (rows[i], sc_part, 0, 0)),
        pl.BlockSpec((None, None, 1, d), lambda i, rows: (rows[i], sh_part, 0, 0)),
    ]
    h_spec = pl.BlockSpec((tm, d), lambda i, rows: (i, 0))
    h_shape = jax.ShapeDtypeStruct((t, d), out_dtype)
    if w_router is None:
        return pl.pallas_call(
            _norm_mod_kernel,
            grid_spec=pltpu.PrefetchScalarGridSpec(
                num_scalar_prefetch=1, grid=(t // tm,), in_specs=in_specs, out_specs=h_spec),
            out_shape=h_shape,
            compiler_params=_params(1),
            name="norm_modulate",
        )(rows, x, g, mod4, mod4)
    n_experts = w_router.shape[1]
    wr = jnp.pad(w_router, ((0, 0), (0, LANES - n_experts)))
    in_specs.append(pl.BlockSpec((d, LANES), lambda i, rows: (0, 0)))
    return pl.pallas_call(
        functools.partial(_norm_mod_router_kernel, n_experts=n_experts),
        grid_spec=pltpu.PrefetchScalarGridSpec(
            num_scalar_prefetch=1, grid=(t // tm,), in_specs=in_specs,
            out_specs=[h_spec, pl.BlockSpec((tm, LANES), lambda i, rows: (i, 0))]),
        out_shape=[h_shape, jax.ShapeDtypeStruct((t, LANES), F32)],
        compiler_params=_params(1),
        name="norm_modulate_router",
    )(rows, x, g, mod4, mod4, wr)


CAST_ROWS = 512


def _cast_weights_if_changed(te_ref, w_refs, wb_refs):
    i = pl.program_id(1)
    changed = jnp.logical_or(i == 0, te_ref[i] != te_ref[jnp.maximum(i - 1, 0)])

    @pl.when(changed)
    def _():
        k = w_refs[0].shape[0]
        step = math.gcd(k, CAST_ROWS)

        def body(c, carry):
            r0 = pl.multiple_of(c * step, step)
            for w_ref, wb_ref in zip(w_refs, wb_refs):
                wb_ref[pl.ds(r0, step), :] = w_ref[pl.ds(r0, step), :].astype(BF16)
            return carry

        lax.fori_loop(0, k // step, body, 0)


def _mm_plain_kernel(te_ref, rows_ref, used_ref, a_ref, w_ref, o_ref, wb_ref):
    del rows_ref
    _cast_weights_if_changed(te_ref, (w_ref,), (wb_ref,))

    @pl.when(pl.program_id(1) < used_ref[0])
    def _():
        acc = jnp.dot(a_ref[...], wb_ref[...], preferred_element_type=F32)
        o_ref[...] = acc.astype(o_ref.dtype)

    @pl.when(pl.program_id(1) >= used_ref[0])
    def _():
        o_ref[...] = jnp.zeros_like(o_ref)


def _mm_qkv_kernel(te_ref, rows_ref, used_ref, a_ref, w_ref, *rest, dilations, q_chunks, q_scale):
    del rows_ref, used_ref
    out_refs = rest[:len(dilations)]
    wb_ref, acc_ref = rest[len(dilations):]
    _cast_weights_if_changed(te_ref, (w_ref,), (wb_ref,))
    acc = jnp.dot(a_ref[...], wb_ref[...], preferred_element_type=F32)
    acc = acc * jnp.where(pl.program_id(0) < q_chunks, q_scale, 1.0).astype(F32)
    tm, tn = acc.shape
    slabs = tn // LANES
    for c in range(slabs):
        acc_ref[c] = acc[:, c * LANES:(c + 1) * LANES]
    for o_ref, d in zip(out_refs, dilations):
        if d == 1:
            o_ref[0] = acc.astype(o_ref.dtype)
            continue
        for r in range(d):
            for c in range(slabs):
                o_ref[r, :, c * LANES:(c + 1) * LANES] = (
                    acc_ref[c, pl.ds(r, tm // d, stride=d), :].astype(o_ref.dtype))


def _mm_swiglu_kernel(te_ref, rows_ref, used_ref, a_ref, w1_ref, w3_ref, o_ref, wb1_ref, wb3_ref):
    del rows_ref
    _cast_weights_if_changed(te_ref, (w1_ref, w3_ref), (wb1_ref, wb3_ref))

    @pl.when(pl.program_id(1) < used_ref[0])
    def _():
        a = a_ref[...]
        g = jnp.dot(a, wb1_ref[...], preferred_element_type=F32)
        u = jnp.dot(a, wb3_ref[...], preferred_element_type=F32)
        o_ref[...] = ((g * jax.nn.sigmoid(g)) * u).astype(o_ref.dtype)

    @pl.when(pl.program_id(1) >= used_ref[0])
    def _():
        o_ref[...] = jnp.zeros_like(o_ref)


def _mm_resid_kernel(te_ref, rows_ref, used_ref, a_ref, w_ref, x_ref, gate_ref, o_ref, wb_ref):
    del rows_ref, used_ref
    _cast_weights_if_changed(te_ref, (w_ref,), (wb_ref,))
    acc = jnp.dot(a_ref[...], wb_ref[...], preferred_element_type=F32)
    o_ref[...] = x_ref[...] + gate_ref[...] * acc


def _a_spec(tm, k):
    return pl.BlockSpec((tm, k), lambda j, i, te, rows, used: (jnp.minimum(i, used[0] - 1), 0))


def _w_spec(k, tn, w_index, col_block0=0, single_buffer=False):
    mode = dict(pipeline_mode=pl.Buffered(1)) if single_buffer else {}
    return pl.BlockSpec((None, k, tn),
                        lambda j, i, te, rows, used: (w_index(te[i]), 0, col_block0 + j), **mode)


def _out_spec(tm, tn):
    return pl.BlockSpec((tm, tn), lambda j, i, te, rows, used: (i, j))


class Tiling:
    def __init__(self, tile_expert, rows, used):
        self.args = (tile_expert, rows, used)


def matmul_qkv(a, w, tiling, *, w_index, tm, tn, col0, n, dilations, q_cols, q_scale):
    m, k = a.shape
    return pl.pallas_call(
        functools.partial(_mm_qkv_kernel, dilations=tuple(dilations), q_chunks=q_cols // tn,
                          q_scale=q_scale),
        grid_spec=pltpu.PrefetchScalarGridSpec(
            num_scalar_prefetch=3, grid=(n // tn, m // tm),
            in_specs=[_a_spec(tm, k), _w_spec(k, tn, w_index, col0 // tn)],
            out_specs=[pl.BlockSpec((d, tm // d, tn), lambda j, i, te, rows, used: (0, i, j))
                       for d in dilations],
            scratch_shapes=[pltpu.VMEM((k, tn), BF16), pltpu.VMEM((tn // LANES, tm, LANES), F32)]),
        out_shape=[jax.ShapeDtypeStruct((d, m // d, n), BF16) for d in dilations],
        compiler_params=_params(2),
        name="matmul_qkv",
    )(*tiling.args, a, w)


def matmul(a, w, tiling, *, w_index, tm, tn, out_dtype, n=None, single_buffer_w=False):
    m, k = a.shape
    n = w.shape[-1] if n is None else n
    return pl.pallas_call(
        _mm_plain_kernel,
        grid_spec=pltpu.PrefetchScalarGridSpec(
            num_scalar_prefetch=3, grid=(n // tn, m // tm),
            in_specs=[_a_spec(tm, k), _w_spec(k, tn, w_index, single_buffer=single_buffer_w)],
            out_specs=_out_spec(tm, tn),
            scratch_shapes=[pltpu.VMEM((k, tn), BF16)]),
        out_shape=jax.ShapeDtypeStruct((m, n), out_dtype),
        compiler_params=_params(2),
        name="matmul",
    )(*tiling.args, a, w)


def matmul_swiglu(a, w1, w3, tiling, *, w_index, tm, tn):
    m, k = a.shape
    n = w1.shape[-1]
    return pl.pallas_call(
        _mm_swiglu_kernel,
        grid_spec=pltpu.PrefetchScalarGridSpec(
            num_scalar_prefetch=3, grid=(n // tn, m // tm),
            in_specs=[_a_spec(tm, k), _w_spec(k, tn, w_index), _w_spec(k, tn, w_index)],
            out_specs=_out_spec(tm, tn),
            scratch_shapes=[pltpu.VMEM((k, tn), BF16), pltpu.VMEM((k, tn), BF16)]),
        out_shape=jax.ShapeDtypeStruct((m, n), BF16),
        compiler_params=_params(2),
        name="matmul_swiglu",
    )(*tiling.args, a, w1, w3)


def matmul_gated_residual(a, w, x, mod4, gate_part, tiling, *, w_index, tm, tn):
    m, k = a.shape
    n = w.shape[-1]
    return pl.pallas_call(
        _mm_resid_kernel,
        grid_spec=pltpu.PrefetchScalarGridSpec(
            num_scalar_prefetch=3, grid=(n // tn, m // tm),
            in_specs=[_a_spec(tm, k), _w_spec(k, tn, w_index), _out_spec(tm, tn),
                      pl.BlockSpec((None, None, 1, tn),
                                   lambda j, i, te, rows, used: (rows[i], gate_part, 0, j))],
            out_specs=_out_spec(tm, tn),
            scratch_shapes=[pltpu.VMEM((k, tn), BF16)]),
        out_shape=jax.ShapeDtypeStruct((m, n), F32),
        compiler_params=_params(2),
        name="matmul_gated_residual",
    )(*tiling.args, a, w, x, mod4)


def _dft_cos_sin(n):
    j = lax.broadcasted_iota(jnp.int32, (n, n), 0)
    k = lax.broadcasted_iota(jnp.int32, (n, n), 1)
    ang = ((j * k) % n).astype(F32) * (2.0 * math.pi / n)
    return jnp.cos(ang), jnp.sin(ang)


def _fourier_ab_kernel(c_ref, s_ref, w_ref, o_ref):
    w = w_ref[...]
    scale = FOURIER_GROUP ** -0.5
    a = jnp.dot(c_ref[...], w, preferred_element_type=F32, precision=lax.Precision.HIGHEST)
    b = jnp.dot(s_ref[...], w, preferred_element_type=F32, precision=lax.Precision.HIGHEST)
    o_ref[:, :FOURIER_GROUP] = (a * scale).astype(o_ref.dtype)
    o_ref[:, FOURIER_GROUP:] = (b * scale).astype(o_ref.dtype)


def fourier_ab(w_fmix):
    n_layers, n_groups, c, _ = w_fmix.shape
    cos_g, sin_g = _dft_cos_sin(c)
    return pl.pallas_call(
        _fourier_ab_kernel,
        grid=(n_layers, n_groups),
        in_specs=[pl.BlockSpec((c, c), lambda l, g: (0, 0)),
                  pl.BlockSpec((c, c), lambda l, g: (0, 0)),
                  pl.BlockSpec((None, None, c, c), lambda l, g: (l, g, 0, 0))],
        out_specs=pl.BlockSpec((None, None, c, 2 * c), lambda l, g: (l, g, 0, 0)),
        out_shape=jax.ShapeDtypeStruct((n_layers, n_groups, c, 2 * c), BF16),
        compiler_params=_params(2),
        name="fourier_ab",
    )(cos_g, sin_g, w_fmix)


def _fourier_channel_kernel(u_ref, ab_ref, pq_ref):
    c = FOURIER_GROUP
    for g in range(N_FOURIER_GROUPS):
        pq = jnp.dot(u_ref[:, g * c:(g + 1) * c], ab_ref[g], preferred_element_type=F32)
        pq_ref[0, :, g * c:(g + 1) * c] = pq[:, :c].astype(pq_ref.dtype)
        pq_ref[1, :, g * c:(g + 1) * c] = pq[:, c:].astype(pq_ref.dtype)


def fourier_channel_stage(u, ab, layer, *, n_rows, seq, row_block0, tm=512):
    tiles = seq // tm
    return pl.pallas_call(
        _fourier_channel_kernel,
        grid=(n_rows, tiles),
        in_specs=[pl.BlockSpec((tm, D_FOURIER), lambda b, i: (row_block0 + b * tiles + i, 0)),
                  pl.BlockSpec((None, N_FOURIER_GROUPS, FOURIER_GROUP, 2 * FOURIER_GROUP),
                               lambda b, i: (layer, 0, 0, 0))],
        out_specs=pl.BlockSpec((None, 2, tm, D_FOURIER), lambda b, i: (b, 0, i, 0)),
        out_shape=jax.ShapeDtypeStruct((n_rows, 2, seq, D_FOURIER), BF16),
        compiler_params=_params(2),
        name="fourier_channel_stage",
    )(u, ab)


def _fourier_seq_kernel(cs_ref, pq_ref, o_ref, *, scale):
    o_ref[...] = jnp.dot(cs_ref[...], pq_ref[...], preferred_element_type=F32) * scale


def fourier_sequence_stage(cs, pq, *, n_rows, seq, tm=512):
    tiles = seq // tm
    return pl.pallas_call(
        functools.partial(_fourier_seq_kernel, scale=seq ** -0.5),
        grid=(n_rows, tiles),
        in_specs=[pl.BlockSpec((tm, 2 * seq), lambda b, i: (i, 0)),
                  pl.BlockSpec((None, 2 * seq, D_FOURIER), lambda b, i: (b, 0, 0))],
        out_specs=pl.BlockSpec((tm, D_FOURIER), lambda b, i: (b * tiles + i, 0)),
        out_shape=jax.ShapeDtypeStruct((n_rows * seq, D_FOURIER), F32),
        compiler_params=_params(2),
        name="fourier_sequence_stage",
    )(cs, pq.reshape(n_rows, 2 * seq, D_FOURIER))


DFT_SPLIT = 64


def _dft_kernel(t1_ref, t2_ref, o_ref):
    seq = t2_ref.shape[2]
    c1, s1 = t1_ref[0], t1_ref[1]
    c2, s2 = t2_ref[0], t2_ref[1]
    o_ref[:, :seq] = (c1 * c2 - s1 * s2).astype(o_ref.dtype)
    o_ref[:, seq:] = (-(s1 * c2 + c1 * s2)).astype(o_ref.dtype)


def dft_matrix(seq):
    k = jnp.arange(seq, dtype=jnp.int32)

    def table(mult):
        ang = ((mult[:, None] * k[None, :]) % seq).astype(F32) * (2.0 * math.pi / seq)
        return jnp.stack([jnp.cos(ang), jnp.sin(ang)])

    n_coarse = seq // DFT_SPLIT
    t1 = table(jnp.arange(n_coarse, dtype=jnp.int32) * DFT_SPLIT).reshape(2, n_coarse, 1, seq)
    t2 = table(jnp.arange(DFT_SPLIT, dtype=jnp.int32))
    return pl.pallas_call(
        _dft_kernel,
        grid=(n_coarse,),
        in_specs=[pl.BlockSpec((2, None, 1, seq), lambda a: (0, a, 0, 0)),
                  pl.BlockSpec((2, DFT_SPLIT, seq), lambda a: (0, 0, 0))],
        out_specs=pl.BlockSpec((DFT_SPLIT, 2 * seq), lambda a: (a, 0)),
        out_shape=jax.ShapeDtypeStruct((seq, 2 * seq), BF16),
        compiler_params=_params(1),
        name="dft_matrix",
    )(t1, t2)


def _attn_kernel(slopes_ref, q_ref, k_ref, v_ref, o_ref, lse_ref, bias_ref, *, dilation, n_side,
                 bq, bk, hps, stack, unroll):
    n_res, sub = q_ref.shape[:2]
    n_cases = bias_ref.shape[0]
    blocks = sub // bq if stack == 1 else 1
    lane = lax.broadcasted_iota(jnp.int32, (bq, LANES), 1)

    @pl.when(jnp.logical_and(pl.program_id(1) == 0, pl.program_id(2) == 0))
    def _():
        row = lax.broadcasted_iota(jnp.int32, (bq, bk), 0)
        col = lax.broadcasted_iota(jnp.int32, (bq, bk), 1)
        for case in range(n_cases):
            dist = jnp.abs(col - row - case * n_side)
            valid = dist <= n_side
            if stack > 1:
                valid = jnp.logical_and(valid, row // sub == col // sub)
            dist_f = dist.astype(F32) * float(dilation)
            for hh in range(hps):
                slope = slopes_ref[pl.program_id(0) * hps + hh] * LOG2_E
                bias_ref[case, hh] = jnp.where(valid, -slope * dist_f, NEG_INF)

    def body(it, carry):
        if stack == 1:
            res = it // blocks
            q0 = pl.multiple_of((it % blocks) * bq, bq)
            ks = pl.multiple_of(jnp.clip(q0 - n_side, 0, sub - bk), n_side)
            case = (q0 - ks) // n_side
            q_at = lambda ref, cols: ref[res, pl.ds(q0, bq), cols]
            k_at = lambda ref, cols: ref[res, pl.ds(ks, bk), cols]
        else:
            case = 0
            res = pl.ds(pl.multiple_of(it * stack, stack), stack)
            q_at = k_at = lambda ref, cols: ref[res, :, cols].reshape(stack * sub, HEAD_DIM)
        lse_tile = jnp.zeros((bq, LANES), F32)
        for hh in range(hps):
            cols = slice(hh * HEAD_DIM, (hh + 1) * HEAD_DIM)
            q = q_at(q_ref, cols)
            k = k_at(k_ref, cols)
            v = k_at(v_ref, cols)
            s = lax.dot_general(q, k, (((1,), (1,)), ((), ())), preferred_element_type=F32)
            s = s + bias_ref[case, hh]
            m = jnp.max(s, axis=-1, keepdims=True)
            p = jnp.exp2(s - m)
            den = jnp.sum(p, axis=-1, keepdims=True)
            num = jnp.dot(p.astype(BF16), v, preferred_element_type=F32)
            o = (num / den).astype(o_ref.dtype)
            if stack == 1:
                o_ref[res, pl.ds(q0, bq), cols] = o
            else:
                o_ref[res, :, cols] = o.reshape(stack, sub, HEAD_DIM)
            lse_tile = jnp.where(lane == pl.program_id(0) * hps + hh, m + jnp.log2(den), lse_tile)
        if stack == 1:
            lse_ref[res, pl.ds(q0, bq), :] = lse_tile
        else:
            lse_ref[res] = lse_tile.reshape(stack, sub, LANES)
        return carry

    lax.fori_loop(0, n_res * blocks // stack, body, 0, unroll=unroll)


ATTN_CHAINS = 12
ATTN_KEY_BLOCK = 256


def attention_step_shape(sub, n_heads, dilation, rows_per_trip):
    def largest(n, unit):
        return max(c for c in range(1, n + 1) if n % c == 0 and c * unit <= ATTN_BLOCK_BYTES)

    hps = largest(n_heads, sub * HEAD_DIM * 2)
    n_res = largest(dilation, sub * hps * HEAD_DIM * 2)
    trips = n_res * sub // rows_per_trip
    unroll = max(u for u in range(1, trips + 1) if trips % u == 0 and u * hps <= max(ATTN_CHAINS, hps))
    return hps, n_res, unroll


def attention_branch(qkv, slopes, *, window, dilation, n_rows, seq, row0, n_heads):
    tg = n_rows * seq
    sub = seq // dilation
    n_side = (window // 2) // dilation
    if sub >= ATTN_KEY_BLOCK:
        stack, bq, bk, n_cases = 1, ATTN_KEY_BLOCK - 2 * n_side, ATTN_KEY_BLOCK, 3
    else:
        stack = ATTN_KEY_BLOCK // sub
        bq, bk, n_cases = ATTN_KEY_BLOCK, ATTN_KEY_BLOCK, 1
    hps, n_res, unroll = attention_step_shape(sub, n_heads, dilation, bq)
    assert n_res % stack == 0 and sub % (bq // stack) == 0
    gw = hps * HEAD_DIM
    groups = n_heads // hps
    d_attn = n_heads * HEAD_DIM
    rb0 = row0 // seq

    def in_map(part):
        return lambda g, b, r, slopes: (r, rb0 + b, part * groups + g)

    out_map = lambda g, b, r, slopes: (r, b, g)
    o, lse = pl.pallas_call(
        functools.partial(_attn_kernel, dilation=dilation, n_side=n_side, bq=bq, bk=bk, hps=hps,
                          stack=stack, unroll=unroll),
        grid_spec=pltpu.PrefetchScalarGridSpec(
            num_scalar_prefetch=1, grid=(groups, n_rows, dilation // n_res),
            in_specs=[pl.BlockSpec((n_res, sub, gw), in_map(part)) for part in range(3)],
            out_specs=[pl.BlockSpec((n_res, sub, gw), out_map),
                       pl.BlockSpec((n_res, sub, LANES), out_map)],
            scratch_shapes=[pltpu.VMEM((n_cases, hps, bq, bk), F32)]),
        out_shape=[jax.ShapeDtypeStruct((dilation, tg // dilation, d_attn), BF16),
                   jax.ShapeDtypeStruct((dilation, tg // dilation, groups * LANES), F32)],
        compiler_params=_params(3),
        name=f"attention_d{dilation}",
    )(slopes, qkv, qkv, qkv)
    return o, lse


def _rms(x, g):
    return x * lax.rsqrt(jnp.mean(x * x, axis=-1, keepdims=True) + EPS) * g


def _merge_tile(f_ref, o_refs, l_refs, gf_ref, ga_ref, y_ref, ya_ref, on_ref, ln_ref, n_heads):
    tm = y_ref.shape[0]
    lses = []
    for bi, (o_ref, l_ref) in enumerate(zip(o_refs, l_refs)):
        d = o_ref.shape[0]
        blocks = l_ref.shape[2] // LANES
        if d == 1:
            parts = [l_ref[0, :, g * LANES:(g + 1) * LANES] for g in range(blocks)]
        else:
            for r in range(d):
                rows = pl.ds(r, tm // d, stride=d)
                for h in range(n_heads):
                    on_ref[bi, h, rows, :] = o_ref[r, :, h * HEAD_DIM:(h + 1) * HEAD_DIM].astype(F32)
                for g in range(blocks):
                    ln_ref[bi, g, rows, :] = l_ref[r, :, g * LANES:(g + 1) * LANES]
            parts = [ln_ref[bi, g] for g in range(blocks)]
        lses.append(functools.reduce(lambda a, b: a + b, parts))

    top = functools.reduce(jnp.maximum, lses)
    ws = [jnp.exp2(l - top) for l in lses]
    inv = 1.0 / functools.reduce(lambda a, b: a + b, ws)
    ws = [w * inv for w in ws]
    for h in range(n_heads):
        cols = slice(h * HEAD_DIM, (h + 1) * HEAD_DIM)
        acc = None
        for bi, o_ref in enumerate(o_refs):
            o = o_ref[0, :, cols].astype(F32) if o_ref.shape[0] == 1 else on_ref[bi, h]
            term = o * ws[bi][:, h:h + 1]
            acc = term if acc is None else acc + term
        ya_ref[:, cols] = acc
    y_ref[:, :D_FOURIER] = _rms(f_ref[...], gf_ref[...]).astype(y_ref.dtype)
    y_ref[:, D_FOURIER:] = _rms(ya_ref[...], ga_ref[...]).astype(y_ref.dtype)


def _merge_kernel(*refs, n_heads, group_tiles):
    n_br = len(DILATED_BRANCHES)
    per_group = 1 + 2 * n_br
    n_groups = len(group_tiles)
    gf_ref, ga_ref, y_ref, ya_ref, on_ref, ln_ref = refs[n_groups * per_group:]
    i = pl.program_id(0)
    tile0 = 0
    for gi, tiles in enumerate(group_tiles):
        grp = refs[gi * per_group:(gi + 1) * per_group]

        @pl.when(jnp.logical_and(i >= tile0, i < tile0 + tiles))
        def _(grp=grp):
            _merge_tile(grp[0], grp[1:1 + n_br], grp[1 + n_br:], gf_ref, ga_ref, y_ref, ya_ref,
                        on_ref, ln_ref, n_heads)

        tile0 += tiles


def merge_and_norm(group_inputs, g_out_f, g_out_a, *, n_heads, tm=256):
    d_attn = n_heads * HEAD_DIM
    group_tiles = [g[0].shape[0] // tm for g in group_inputs]
    n_tiles = sum(group_tiles)
    n_br = len(DILATED_BRANCHES)
    in_specs, args = [], []
    tile0 = 0
    for (f, os_, ls_), tiles in zip(group_inputs, group_tiles):
        local = lambda i, tile0=tile0, tiles=tiles: jnp.clip(i - tile0, 0, tiles - 1)
        in_specs.append(pl.BlockSpec((tm, D_FOURIER), lambda i, local=local: (local(i), 0)))
        for arr in (*os_, *ls_):
            d = arr.shape[0]
            in_specs.append(pl.BlockSpec((d, tm // d, arr.shape[2]),
                                         lambda i, local=local: (0, local(i), 0)))
        args += [f, *os_, *ls_]
        tile0 += tiles
    fixed = lambda i: (0, 0)
    in_specs += [pl.BlockSpec((1, D_FOURIER), fixed), pl.BlockSpec((1, d_attn), fixed)]
    max_groups = max(l.shape[2] // LANES for g in group_inputs for l in g[2])
    return pl.pallas_call(
        functools.partial(_merge_kernel, n_heads=n_heads, group_tiles=tuple(group_tiles)),
        grid=(n_tiles,),
        in_specs=in_specs,
        out_specs=pl.BlockSpec((tm, D_FOURIER + d_attn), lambda i: (i, 0)),
        out_shape=jax.ShapeDtypeStruct((n_tiles * tm, D_FOURIER + d_attn), BF16),
        scratch_shapes=[pltpu.VMEM((tm, d_attn), F32),
                        pltpu.VMEM((n_br, n_heads, tm, HEAD_DIM), F32),
                        pltpu.VMEM((n_br, max_groups, tm, LANES), F32)],
        compiler_params=_params(1),
        name="merge_and_norm",
    )(*args, g_out_f, g_out_a)


def _gather_kernel(tok_ref, h_ref, o_ref, buf_ref, sem):
    tm = buf_ref.shape[0]

    def copy(r):
        return pltpu.make_async_copy(h_ref.at[pl.ds(tok_ref[0, r], 1)], buf_ref.at[pl.ds(r, 1)], sem)

    def start(r, carry):
        copy(r).start()
        return carry

    def wait(r, carry):
        copy(r).wait()
        return carry

    lax.fori_loop(0, tm, start, 0, unroll=8)
    lax.fori_loop(0, tm, wait, 0, unroll=8)
    o_ref[...] = buf_ref[...].astype(o_ref.dtype)


def gather_rows(h, row_token, *, tm):
    p = row_token.shape[0]
    d = h.shape[1]
    return pl.pallas_call(
        _gather_kernel,
        grid=(p // tm,),
        in_specs=[pl.BlockSpec((None, 1, tm), lambda i: (i, 0, 0), memory_space=pltpu.SMEM),
                  pl.BlockSpec(memory_space=pl.ANY)],
        out_specs=pl.BlockSpec((tm, d), lambda i: (i, 0)),
        out_shape=jax.ShapeDtypeStruct((p, d), BF16),
        scratch_shapes=[pltpu.VMEM((tm, d), F32), pltpu.SemaphoreType.DMA(())],
        compiler_params=_params(1),
        name="moe_gather",
    )(row_token.reshape(p // tm, 1, tm), h)


def _combine_kernel(rows_ref, pos_ref, o_ref, route_ref, x_ref, gate_ref, out_ref, buf_ref, sem):
    del rows_ref
    tm = x_ref.shape[0]

    def copy(r):
        return pltpu.make_async_copy(o_ref.at[pl.ds(pos_ref[0, r], 1)],
                                     buf_ref.at[pl.ds(r, 1)], sem)

    def start(r, carry):
        copy(r).start()
        return carry

    def wait(r, carry):
        copy(r).wait()
        return carry

    lax.fori_loop(0, TOP_K * tm, start, 0, unroll=8)
    lax.fori_loop(0, TOP_K * tm, wait, 0, unroll=8)
    route = route_ref[...]
    y = route[:, 2:3] * buf_ref[pl.ds(0, tm), :] + route[:, 3:4] * buf_ref[pl.ds(tm, tm), :]
    out_ref[...] = x_ref[...] + gate_ref[...] * y


def moe_combine(o_sorted, pos, route, x, mod4, gate_part, rows, *, tm):
    t, d = x.shape
    return pl.pallas_call(
        _combine_kernel,
        grid_spec=pltpu.PrefetchScalarGridSpec(
            num_scalar_prefetch=1, grid=(t // tm,),
            in_specs=[pl.BlockSpec((None, 1, TOP_K * tm), lambda i, rows: (i, 0, 0),
                                   memory_space=pltpu.SMEM),
                      pl.BlockSpec(memory_space=pl.ANY),
                      pl.BlockSpec((tm, LANES), lambda i, rows: (i, 0)),
                      pl.BlockSpec((tm, d), lambda i, rows: (i, 0)),
                      pl.BlockSpec((None, None, 1, d), lambda i, rows: (rows[i], gate_part, 0, 0))],
            out_specs=pl.BlockSpec((tm, d), lambda i, rows: (i, 0)),
            scratch_shapes=[pltpu.VMEM((TOP_K * tm, d), F32), pltpu.SemaphoreType.DMA(())]),
        out_shape=jax.ShapeDtypeStruct((t, d), F32),
        compiler_params=_params(1),
        name="moe_combine",
    )(rows, pos, o_sorted, route, x, mod4)


def moe_dispatch_plan(route, n_experts, tm):
    t = route.shape[0]
    n_slots = t * TOP_K
    p = n_slots + n_experts * tm
    expert = route[:, :TOP_K].astype(jnp.int32).reshape(n_slots)
    onehot = (expert[:, None] == jnp.arange(n_experts)[None, :]).astype(jnp.int32)
    rank = jnp.take_along_axis(jnp.cumsum(onehot, axis=0) - onehot, expert[:, None], axis=1)[:, 0]
    counts = jnp.sum(onehot, axis=0)
    padded = ((counts + tm - 1) // tm) * tm
    ends = jnp.cumsum(padded)
    starts = ends - padded
    pos = starts[expert] + rank
    row_token = jnp.zeros((p,), jnp.int32).at[pos].set(jnp.arange(n_slots, dtype=jnp.int32) // TOP_K)
    tile_start = jnp.arange(p // tm, dtype=jnp.int32) * tm
    tile_expert = jnp.sum((tile_start[:, None] >= ends[None, :]).astype(jnp.int32), axis=1)
    tile_expert = jnp.minimum(tile_expert, n_experts - 1)
    used = (ends[-1:] // tm).astype(jnp.int32)
    return row_token, pos.reshape(t, TOP_K), tile_expert, used


def _final_norm_kernel(x_ref, g_ref, o_ref):
    o_ref[...] = _rms(x_ref[...], g_ref[...])


def final_norm(x, g, *, row0, n_rows, tm=512):
    d = x.shape[1]
    blk0 = row0 // tm
    return pl.pallas_call(
        _final_norm_kernel,
        grid=(n_rows // tm,),
        in_specs=[pl.BlockSpec((tm, d), lambda i: (blk0 + i, 0)),
                  pl.BlockSpec((1, d), lambda i: (0, 0))],
        out_specs=pl.BlockSpec((tm, d), lambda i: (i, 0)),
        out_shape=jax.ShapeDtypeStruct((n_rows, d), F32),
        compiler_params=_params(1),
        name="final_norm",
    )(x, g)


TM = 512
TM_BIG = 1024
TM_COMBINE = 256


def _tile_rows(groups, tm):
    rows = []
    base = 0
    for n_rows, seq in groups:
        for b in range(n_rows):
            rows += [base + b] * (seq // tm)
        base += n_rows
    return jnp.asarray(np.asarray(rows, np.int32))


def kernel(x_prompt, x_sample, c_prompt, c_sample, w_ada, b_ada, g_norm_mix, g_norm_ff, w_in, w_fmix,
           g_out_f, g_out_a, w_out, w_ff1, w_ff3, w_ff2, w_router, w_e1, w_e3, w_e2, g_final):
    depth, d, d_in = w_in.shape
    groups = [(x_prompt.shape[0], x_prompt.shape[1]), (x_sample.shape[0], x_sample.shape[1])]
    group_row0 = [0, groups[0][0] * groups[0][1]]
    t = sum(b * s for b, s in groups)
    n_req = sum(b for b, _ in groups)
    d_attn = d - D_FOURIER
    n_heads = d_attn // HEAD_DIM
    n_experts = w_router.shape[-1]
    d_ff_e = w_e1.shape[-1]
    dilations = [dil for _, dil in DILATED_BRANCHES]

    x = jnp.concatenate([x_prompt.reshape(-1, d), x_sample.reshape(-1, d)], axis=0)
    c = jnp.concatenate([c_prompt, c_sample], axis=0)
    c = jnp.pad(c, ((0, -n_req % 8), (0, 0)))
    mod = ada_modulation(c, w_ada, b_ada)
    rows = _tile_rows(groups, TM)
    rows_c = _tile_rows(groups, TM_COMBINE)
    dense = Tiling(jnp.zeros((t // TM,), jnp.int32), rows, jnp.full((1,), t // TM, jnp.int32))
    dense_big = Tiling(jnp.zeros((t // TM_BIG,), jnp.int32), _tile_rows(groups, TM_BIG),
                       jnp.full((1,), t // TM_BIG, jnp.int32))
    slopes = jnp.exp2(-8.0 * (jnp.arange(n_heads, dtype=F32) + 1.0) / n_heads)
    ab = fourier_ab(w_fmix)
    dft = [dft_matrix(seq) for _, seq in groups]

    for l in range(depth):
        mod4 = mod[l].reshape(mod.shape[1], 6, 1, d)
        layer = lambda e, l=l: l
        h = norm_modulate(x, rows, g_norm_mix[l:l + 1], mod4, 0, tm=TM)
        u = matmul(h, w_in, dense_big, w_index=layer, tm=TM_BIG, tn=D_FOURIER, out_dtype=BF16,
                   n=D_FOURIER)
        qkvs = matmul_qkv(h, w_in, dense_big, w_index=layer, tm=TM_BIG, tn=512, col0=D_FOURIER,
                          n=3 * d_attn, dilations=dilations, q_cols=d_attn,
                          q_scale=HEAD_DIM ** -0.5 * LOG2_E)

        group_inputs = []
        for gi, ((n_rows, seq), row0) in enumerate(zip(groups, group_row0)):
            pq = fourier_channel_stage(u, ab, l, n_rows=n_rows, seq=seq, row_block0=row0 // TM, tm=TM)
            f = fourier_sequence_stage(dft[gi], pq, n_rows=n_rows, seq=seq, tm=TM)
            branches = [attention_branch(qkv, slopes, window=window, dilation=dilation, n_rows=n_rows,
                                         seq=seq, row0=row0, n_heads=n_heads)
                        for qkv, (window, dilation) in zip(qkvs, DILATED_BRANCHES)]
            group_inputs.append((f, [b[0] for b in branches], [b[1] for b in branches]))
        y = merge_and_norm(group_inputs, g_out_f[l:l + 1], g_out_a[l:l + 1], n_heads=n_heads)
        x = matmul_gated_residual(y, w_out, x, mod4, 2, dense_big, w_index=layer, tm=TM_BIG, tn=512)

        j = l // 2
        if l % 2 == 0:
            h = norm_modulate(x, rows, g_norm_ff[l:l + 1], mod4, 3, tm=TM)
            act = matmul_swiglu(h, w_ff1, w_ff3, dense_big, w_index=lambda e, j=j: j, tm=TM_BIG, tn=512)
            x = matmul_gated_residual(act, w_ff2, x, mod4, 5, dense, w_index=lambda e, j=j: j,
                                      tm=TM, tn=512)
        else:
            h, route = norm_modulate(x, rows, g_norm_ff[l:l + 1], mod4, 3, out_dtype=F32,
                                     w_router=w_router[j], tm=TM)
            row_token, pos, tile_expert, used = moe_dispatch_plan(route, n_experts, TM)
            routed = Tiling(tile_expert, tile_expert, used)
            xs = gather_rows(h, row_token, tm=TM)
            expert_w = lambda e, j=j: j * n_experts + e
            act = matmul_swiglu(xs, w_e1.reshape(-1, d, d_ff_e), w_e3.reshape(-1, d, d_ff_e), routed,
                                w_index=expert_w, tm=TM, tn=1024)
            o_sorted = matmul(act, w_e2.reshape(-1, d_ff_e, d), routed, w_index=expert_w,
                              tm=TM, tn=512, out_dtype=F32, single_buffer_w=True)
            pos_tiles = pos.reshape(t // TM_COMBINE, TM_COMBINE, TOP_K).transpose(0, 2, 1)
            pos_tiles = pos_tiles.reshape(t // TM_COMBINE, 1, TOP_K * TM_COMBINE)
            x = moe_combine(o_sorted, pos_tiles, route, x, mod4, 5, rows_c, tm=TM_COMBINE)

    g = g_final.reshape(1, d)
    outs = []
    for (n_rows, seq), row0 in zip(groups, group_row0):
        outs.append(final_norm(x, g, row0=row0, n_rows=n_rows * seq, tm=TM).reshape(n_rows, seq, d))
    return tuple(outs)
```

```python
import functools
import math

import numpy as np
import jax
import jax.numpy as jnp
from jax import lax
from jax.experimental import pallas as pl
from jax.experimental.pallas import tpu as pltpu

HEAD_DIM = 128
FOURIER_GROUP = 128
N_FOURIER_GROUPS = 4
D_FOURIER = FOURIER_GROUP * N_FOURIER_GROUPS
DILATED_BRANCHES = ((128, 1), (512, 4), (2048, 16))
TOP_K = 2
EPS = 1e-6
NEG_INF = -1e30
LOG2_E = math.log2(math.e)
LANES = 128
VMEM_LIMIT = 56 * 1024 * 1024
VMEM_LIMIT_BIG = 62 * 1024 * 1024
ATTN_BLOCK_BYTES = 4 * 1024 * 1024

BF16 = jnp.bfloat16
F32 = jnp.float32


def _params(n_axes, vmem=VMEM_LIMIT):
    return pltpu.CompilerParams(
        dimension_semantics=("arbitrary",) * n_axes, vmem_limit_bytes=vmem)


def _ada_kernel(c_ref, w_ref, b_ref, o_ref):
    c = c_ref[...]
    a = (c * jax.nn.sigmoid(c)).astype(BF16)
    acc = jnp.dot(a, w_ref[...].astype(BF16), preferred_element_type=F32)
    o_ref[...] = acc + b_ref[...]


def ada_modulation(c, w_ada, b_ada, tn=1024):
    n_layers, d, n = w_ada.shape
    r8 = c.shape[0]
    return pl.pallas_call(
        _ada_kernel,
        grid=(n_layers, n // tn),
        in_specs=[
            pl.BlockSpec((r8, d), lambda l, j: (0, 0)),
            pl.BlockSpec((None, d, tn), lambda l, j: (l, 0, j)),
            pl.BlockSpec((None, 1, tn), lambda l, j: (l, 0, j)),
        ],
        out_specs=pl.BlockSpec((None, r8, tn), lambda l, j: (l, 0, j)),
        out_shape=jax.ShapeDtypeStruct((n_layers, r8, n), F32),
        compiler_params=_params(2),
        name="ada_modulation",
    )(c, w_ada, b_ada.reshape(n_layers, 1, n))


def _norm_mod(x, g, sc, sh):
    y = x * lax.rsqrt(jnp.mean(x * x, axis=-1, keepdims=True) + EPS)
    return (y * g) * (1.0 + sc) + sh


def _norm_mod_kernel(rows_ref, x_ref, g_ref, sc_ref, sh_ref, h_ref):
    del rows_ref
    h_ref[...] = _norm_mod(x_ref[...], g_ref[...], sc_ref[...], sh_ref[...]).astype(h_ref.dtype)


def _norm_mod_router_kernel(rows_ref, x_ref, g_ref, sc_ref, sh_ref, wr_ref, h_ref, route_ref,
                            stage_ref, *, n_experts):
    del rows_ref
    h = _norm_mod(x_ref[...], g_ref[...], sc_ref[...], sh_ref[...])
    tm = h.shape[0]
    per = h_ref.shape[0] // SLAB_STRIDE
    for a in range(SLAB_STRIDE):
        for b in range(per // tm):
            c = SLAB_STRIDE * b + a
            stage_ref[a, pl.ds(b, tm, stride=per // tm), :] = h[:, c * LANES:(c + 1) * LANES]
    for a in range(SLAB_STRIDE):
        h_ref[pl.ds(a, per, stride=SLAB_STRIDE), :] = stage_ref[a]
    w = wr_ref[...]
    h_hi, w_hi = h.astype(BF16), w.astype(BF16)
    h_lo = (h - h_hi.astype(F32)).astype(BF16)
    w_lo = (w - w_hi.astype(F32)).astype(BF16)
    logits = (jnp.dot(h_hi, w_hi, preferred_element_type=F32)
              + (jnp.dot(h_hi, w_lo, preferred_element_type=F32)
                 + jnp.dot(h_lo, w_hi, preferred_element_type=F32)))
    lane = lax.broadcasted_iota(jnp.int32, logits.shape, 1)
    logits = jnp.where(lane < n_experts, logits, -jnp.inf)
    m1 = jnp.max(logits, axis=-1, keepdims=True)
    i1 = jnp.min(jnp.where(logits == m1, lane, LANES), axis=-1, keepdims=True)
    rest = jnp.where(lane == i1, -jnp.inf, logits)
    m2 = jnp.max(rest, axis=-1, keepdims=True)
    i2 = jnp.min(jnp.where(rest == m2, lane, LANES), axis=-1, keepdims=True)
    e2 = jnp.exp(m2 - m1)
    den = 1.0 + e2
    route = jnp.where(lane == 0, i1.astype(F32), 0.0)
    route = jnp.where(lane == 1, i2.astype(F32), route)
    route = jnp.where(lane == 2, 1.0 / den, route)
    route = jnp.where(lane == 3, e2 / den, route)
    route_ref[...] = route


def norm_modulate(x, rows, g, mod4, layer_part, *, out_dtype=BF16, w_router=None, tm=512):
    t, d = x.shape
    sh_part, sc_part = layer_part, layer_part + 1
    in_specs = [
        pl.BlockSpec((tm, d), lambda i, rows: (i, 0)),
        pl.BlockSpec((1, d), lambda i, rows: (0, 0)),
        pl.BlockSpec((None, None, 1, d), lambda i, rows: (rows[i], sc_part, 0, 0)),
        pl.BlockSpec((None, None, 1, d), lambda i, rows: (rows[i], sh_part, 0, 0)),
    ]
    h_spec = pl.BlockSpec((tm, d), lambda i, rows: (i, 0))
    h_shape = jax.ShapeDtypeStruct((t, d), out_dtype)
    if w_router is None:
        return pl.pallas_call(
            _norm_mod_kernel,
            grid_spec=pltpu.PrefetchScalarGridSpec(
                num_scalar_prefetch=1, grid=(t // tm,), in_specs=in_specs, out_specs=h_spec),
            out_shape=h_shape,
            compiler_params=_params(1),
            name="norm_modulate",
        )(rows, x, g, mod4, mod4)
    n_experts = w_router.shape[1]
    wr = jnp.pad(w_router, ((0, 0), (0, LANES - n_experts)))
    in_specs.append(pl.BlockSpec((d, LANES), lambda i, rows: (0, 0)))
    chunks = d // LANES
    assert chunks % SLAB_STRIDE == 0
    h_flat, route = pl.pallas_call(
        functools.partial(_norm_mod_router_kernel, n_experts=n_experts),
        grid_spec=pltpu.PrefetchScalarGridSpec(
            num_scalar_prefetch=1, grid=(t // tm,), in_specs=in_specs,
            out_specs=[pl.BlockSpec((tm * chunks, LANES), lambda i, rows: (i, 0)),
                       pl.BlockSpec((tm, LANES), lambda i, rows: (i, 0))],
            scratch_shapes=[pltpu.VMEM((SLAB_STRIDE, tm * chunks // SLAB_STRIDE, LANES), F32)]),
        out_shape=[jax.ShapeDtypeStruct((t * chunks, LANES), F32),
                   jax.ShapeDtypeStruct((t, LANES), F32)],
        compiler_params=_params(1),
        name="norm_modulate_router",
    )(rows, x, g, mod4, mod4, wr)
    return h_flat.reshape(t, chunks, LANES), route


CAST_ROWS = 512


def _cast_weights_if_changed(te_ref, w_refs, wb_refs):
    i = pl.program_id(1)
    changed = jnp.logical_or(i == 0, te_ref[i] != te_ref[jnp.maximum(i - 1, 0)])

    @pl.when(changed)
    def _():
        k = w_refs[0].shape[0]
        step = math.gcd(k, CAST_ROWS)

        def body(c, carry):
            r0 = pl.multiple_of(c * step, step)
            for w_ref, wb_ref in zip(w_refs, wb_refs):
                wb_ref[pl.ds(r0, step), :] = w_ref[pl.ds(r0, step), :].astype(BF16)
            return carry

        lax.fori_loop(0, k // step, body, 0)


def _mm_plain_kernel(te_ref, rows_ref, used_ref, a_ref, w_ref, o_ref, wb_ref):
    del rows_ref
    _cast_weights_if_changed(te_ref, (w_ref,), (wb_ref,))

    @pl.when(pl.program_id(1) < used_ref[0])
    def _():
        acc = jnp.dot(a_ref[...], wb_ref[...], preferred_element_type=F32)
        o_ref[...] = acc.astype(o_ref.dtype)

    @pl.when(pl.program_id(1) >= used_ref[0])
    def _():
        o_ref[...] = jnp.zeros_like(o_ref)


def _mm_qkv_kernel(te_ref, rows_ref, used_ref, a_ref, w_ref, *rest, dilations, q_chunks, q_scale):
    del rows_ref, used_ref
    out_refs = rest[:len(dilations)]
    wb_ref, acc_ref, stage_ref = rest[len(dilations):]
    _cast_weights_if_changed(te_ref, (w_ref,), (wb_ref,))
    acc = jnp.dot(a_ref[...], wb_ref[...], preferred_element_type=F32)
    acc = acc * jnp.where(pl.program_id(0) < q_chunks, q_scale, 1.0).astype(F32)
    tm, tn = acc.shape
    slabs = tn // LANES
    for c in range(slabs):
        acc_ref[c] = acc[:, c * LANES:(c + 1) * LANES]
    cur_ref, cur_d = acc_ref, 1
    for idx, (o_ref, d) in enumerate(zip(out_refs, dilations)):
        if d == 1:
            o_ref[0] = acc.astype(o_ref.dtype)
            continue
        f = d // cur_d
        keep = idx + 1 < len(dilations)
        for rp in range(cur_d):
            for b in range(f):
                r = b * cur_d + rp
                for c in range(slabs):
                    piece = cur_ref[c, pl.ds(rp * (tm // cur_d) + b, tm // d, stride=f), :]
                    o_ref[r, :, c * LANES:(c + 1) * LANES] = piece.astype(o_ref.dtype)
                    if keep:
                        stage_ref[c, pl.ds(r * (tm // d), tm // d), :] = piece
        cur_ref, cur_d = stage_ref, d


def _mm_swiglu_kernel(te_ref, rows_ref, used_ref, a_ref, w1_ref, w3_ref, o_ref, wb1_ref, wb3_ref):
    del rows_ref
    _cast_weights_if_changed(te_ref, (w1_ref, w3_ref), (wb1_ref, wb3_ref))

    @pl.when(pl.program_id(1) < used_ref[0])
    def _():
        a = a_ref[...]
        g = jnp.dot(a, wb1_ref[...], preferred_element_type=F32)
        u = jnp.dot(a, wb3_ref[...], preferred_element_type=F32)
        o_ref[...] = ((g * jax.nn.sigmoid(g)) * u).astype(o_ref.dtype)

    @pl.when(pl.program_id(1) >= used_ref[0])
    def _():
        o_ref[...] = jnp.zeros_like(o_ref)


def _mm_resid_kernel(te_ref, rows_ref, used_ref, a_ref, w_ref, x_ref, gate_ref, o_ref, wb_ref):
    del rows_ref, used_ref
    _cast_weights_if_changed(te_ref, (w_ref,), (wb_ref,))
    acc = jnp.dot(a_ref[...], wb_ref[...], preferred_element_type=F32)
    o_ref[...] = x_ref[...] + gate_ref[...] * acc


def _a_spec(tm, k):
    return pl.BlockSpec((tm, k), lambda j, i, te, rows, used: (jnp.minimum(i, used[0] - 1), 0))


def _w_spec(k, tn, w_index, col_block0=0, single_buffer=False):
    mode = dict(pipeline_mode=pl.Buffered(1)) if single_buffer else {}
    return pl.BlockSpec((None, k, tn),
                        lambda j, i, te, rows, used: (w_index(te[i]), 0, col_block0 + j), **mode)


def _out_spec(tm, tn):
    return pl.BlockSpec((tm, tn), lambda j, i, te, rows, used: (i, j))


class Tiling:
    def __init__(self, tile_expert, rows, used):
        self.args = (tile_expert, rows, used)


def matmul_qkv(a, w, tiling, *, w_index, tm, tn, col0, n, dilations, q_cols, q_scale):
    m, k = a.shape
    assert sum(d > 1 for d in dilations) <= 2
    return pl.pallas_call(
        functools.partial(_mm_qkv_kernel, dilations=tuple(dilations), q_chunks=q_cols // tn,
                          q_scale=q_scale),
        grid_spec=pltpu.PrefetchScalarGridSpec(
            num_scalar_prefetch=3, grid=(n // tn, m // tm),
            in_specs=[_a_spec(tm, k), _w_spec(k, tn, w_index, col0 // tn)],
            out_specs=[pl.BlockSpec((d, tm // d, tn), lambda j, i, te, rows, used: (0, i, j))
                       for d in dilations],
            scratch_shapes=[pltpu.VMEM((k, tn), BF16)]
                           + [pltpu.VMEM((tn // LANES, tm, LANES), F32)] * 2),
        out_shape=[jax.ShapeDtypeStruct((d, m // d, n), BF16) for d in dilations],
        compiler_params=_params(2),
        name="matmul_qkv",
    )(*tiling.args, a, w)


def matmul(a, w, tiling, *, w_index, tm, tn, out_dtype, n=None, vmem=VMEM_LIMIT):
    m, k = a.shape
    n = w.shape[-1] if n is None else n
    return pl.pallas_call(
        _mm_plain_kernel,
        grid_spec=pltpu.PrefetchScalarGridSpec(
            num_scalar_prefetch=3, grid=(n // tn, m // tm),
            in_specs=[_a_spec(tm, k), _w_spec(k, tn, w_index)],
            out_specs=_out_spec(tm, tn),
            scratch_shapes=[pltpu.VMEM((k, tn), BF16)]),
        out_shape=jax.ShapeDtypeStruct((m, n), out_dtype),
        compiler_params=_params(2, vmem),
        name="matmul",
    )(*tiling.args, a, w)


def matmul_swiglu(a, w1, w3, tiling, *, w_index, tm, tn):
    m, k = a.shape
    n = w1.shape[-1]
    return pl.pallas_call(
        _mm_swiglu_kernel,
        grid_spec=pltpu.PrefetchScalarGridSpec(
            num_scalar_prefetch=3, grid=(n // tn, m // tm),
            in_specs=[_a_spec(tm, k), _w_spec(k, tn, w_index), _w_spec(k, tn, w_index)],
            out_specs=_out_spec(tm, tn),
            scratch_shapes=[pltpu.VMEM((k, tn), BF16), pltpu.VMEM((k, tn), BF16)]),
        out_shape=jax.ShapeDtypeStruct((m, n), BF16),
        compiler_params=_params(2),
        name="matmul_swiglu",
    )(*tiling.args, a, w1, w3)


def matmul_gated_residual(a, w, x, mod4, gate_part, tiling, *, w_index, tm, tn):
    m, k = a.shape
    n = w.shape[-1]
    return pl.pallas_call(
        _mm_resid_kernel,
        grid_spec=pltpu.PrefetchScalarGridSpec(
            num_scalar_prefetch=3, grid=(n // tn, m // tm),
            in_specs=[_a_spec(tm, k), _w_spec(k, tn, w_index), _out_spec(tm, tn),
                      pl.BlockSpec((None, None, 1, tn),
                                   lambda j, i, te, rows, used: (rows[i], gate_part, 0, j))],
            out_specs=_out_spec(tm, tn),
            scratch_shapes=[pltpu.VMEM((k, tn), BF16)]),
        out_shape=jax.ShapeDtypeStruct((m, n), F32),
        compiler_params=_params(2),
        name="matmul_gated_residual",
    )(*tiling.args, a, w, x, mod4)


def _dft_cos_sin(n):
    j = lax.broadcasted_iota(jnp.int32, (n, n), 0)
    k = lax.broadcasted_iota(jnp.int32, (n, n), 1)
    ang = ((j * k) % n).astype(F32) * (2.0 * math.pi / n)
    return jnp.cos(ang), jnp.sin(ang)


def _fourier_ab_kernel(c_ref, s_ref, w_ref, o_ref):
    w = w_ref[...]
    scale = FOURIER_GROUP ** -0.5
    a = jnp.dot(c_ref[...], w, preferred_element_type=F32, precision=lax.Precision.HIGHEST)
    b = jnp.dot(s_ref[...], w, preferred_element_type=F32, precision=lax.Precision.HIGHEST)
    o_ref[:, :FOURIER_GROUP] = (a * scale).astype(o_ref.dtype)
    o_ref[:, FOURIER_GROUP:] = (b * scale).astype(o_ref.dtype)


def fourier_ab(w_fmix):
    n_layers, n_groups, c, _ = w_fmix.shape
    cos_g, sin_g = _dft_cos_sin(c)
    return pl.pallas_call(
        _fourier_ab_kernel,
        grid=(n_layers, n_groups),
        in_specs=[pl.BlockSpec((c, c), lambda l, g: (0, 0)),
                  pl.BlockSpec((c, c), lambda l, g: (0, 0)),
                  pl.BlockSpec((None, None, c, c), lambda l, g: (l, g, 0, 0))],
        out_specs=pl.BlockSpec((None, None, c, 2 * c), lambda l, g: (l, g, 0, 0)),
        out_shape=jax.ShapeDtypeStruct((n_layers, n_groups, c, 2 * c), BF16),
        compiler_params=_params(2),
        name="fourier_ab",
    )(cos_g, sin_g, w_fmix)


def _fourier_channel_kernel(u_ref, ab_ref, pq_ref):
    c = FOURIER_GROUP
    for g in range(N_FOURIER_GROUPS):
        pq = jnp.dot(u_ref[:, g * c:(g + 1) * c], ab_ref[g], preferred_element_type=F32)
        pq_ref[0, :, g * c:(g + 1) * c] = pq[:, :c].astype(pq_ref.dtype)
        pq_ref[1, :, g * c:(g + 1) * c] = pq[:, c:].astype(pq_ref.dtype)


def fourier_channel_stage(u, ab, layer, *, n_rows, seq, row_block0, tm=512):
    tiles = seq // tm
    return pl.pallas_call(
        _fourier_channel_kernel,
        grid=(n_rows, tiles),
        in_specs=[pl.BlockSpec((tm, D_FOURIER), lambda b, i: (row_block0 + b * tiles + i, 0)),
                  pl.BlockSpec((None, N_FOURIER_GROUPS, FOURIER_GROUP, 2 * FOURIER_GROUP),
                               lambda b, i: (layer, 0, 0, 0))],
        out_specs=pl.BlockSpec((None, 2, tm, D_FOURIER), lambda b, i: (b, 0, i, 0)),
        out_shape=jax.ShapeDtypeStruct((n_rows, 2, seq, D_FOURIER), BF16),
        compiler_params=_params(2),
        name="fourier_channel_stage",
    )(u, ab)


def _fourier_seq_kernel(cs_ref, pq_ref, o_ref, *, scale):
    o_ref[...] = jnp.dot(cs_ref[...], pq_ref[...], preferred_element_type=F32) * scale


def fourier_sequence_stage(cs, pq, *, n_rows, seq, tm=512):
    tiles = seq // tm
    return pl.pallas_call(
        functools.partial(_fourier_seq_kernel, scale=seq ** -0.5),
        grid=(n_rows, tiles),
        in_specs=[pl.BlockSpec((tm, 2 * seq), lambda b, i: (i, 0)),
                  pl.BlockSpec((None, 2 * seq, D_FOURIER), lambda b, i: (b, 0, 0))],
        out_specs=pl.BlockSpec((tm, D_FOURIER), lambda b, i: (b * tiles + i, 0)),
        out_shape=jax.ShapeDtypeStruct((n_rows * seq, D_FOURIER), F32),
        compiler_params=_params(2),
        name="fourier_sequence_stage",
    )(cs, pq.reshape(n_rows, 2 * seq, D_FOURIER))


DFT_SPLIT = 64


def _dft_kernel(t1_ref, t2_ref, o_ref):
    seq = t2_ref.shape[2]
    c1, s1 = t1_ref[0], t1_ref[1]
    c2, s2 = t2_ref[0], t2_ref[1]
    o_ref[:, :seq] = (c1 * c2 - s1 * s2).astype(o_ref.dtype)
    o_ref[:, seq:] = (-(s1 * c2 + c1 * s2)).astype(o_ref.dtype)


def dft_matrix(seq):
    k = jnp.arange(seq, dtype=jnp.int32)

    def table(mult):
        ang = ((mult[:, None] * k[None, :]) % seq).astype(F32) * (2.0 * math.pi / seq)
        return jnp.stack([jnp.cos(ang), jnp.sin(ang)])

    n_coarse = seq // DFT_SPLIT
    t1 = table(jnp.arange(n_coarse, dtype=jnp.int32) * DFT_SPLIT).reshape(2, n_coarse, 1, seq)
    t2 = table(jnp.arange(DFT_SPLIT, dtype=jnp.int32))
    return pl.pallas_call(
        _dft_kernel,
        grid=(n_coarse,),
        in_specs=[pl.BlockSpec((2, None, 1, seq), lambda a: (0, a, 0, 0)),
                  pl.BlockSpec((2, DFT_SPLIT, seq), lambda a: (0, 0, 0))],
        out_specs=pl.BlockSpec((DFT_SPLIT, 2 * seq), lambda a: (a, 0)),
        out_shape=jax.ShapeDtypeStruct((seq, 2 * seq), BF16),
        compiler_params=_params(1),
        name="dft_matrix",
    )(t1, t2)


def _attn_kernel(slopes_ref, q_ref, k_ref, v_ref, o_ref, lse_ref, bias_ref, *, dilation, n_side,
                 bq, bk, hps, stack, unroll):
    n_res, sub = q_ref.shape[:2]
    n_cases = bias_ref.shape[0]
    blocks = sub // bq if stack == 1 else 1
    lane = lax.broadcasted_iota(jnp.int32, (bq, LANES), 1)

    @pl.when(jnp.logical_and(pl.program_id(1) == 0, pl.program_id(2) == 0))
    def _():
        row = lax.broadcasted_iota(jnp.int32, (bq, bk), 0)
        col = lax.broadcasted_iota(jnp.int32, (bq, bk), 1)
        for case in range(n_cases):
            dist = jnp.abs(col - row - case * n_side)
            valid = dist <= n_side
            if stack > 1:
                valid = jnp.logical_and(valid, row // sub == col // sub)
            dist_f = dist.astype(F32) * float(dilation)
            for hh in range(hps):
                slope = slopes_ref[pl.program_id(0) * hps + hh] * LOG2_E
                bias_ref[case, hh] = jnp.where(valid, -slope * dist_f, NEG_INF)

    def body(it, carry):
        if stack == 1:
            res = it // blocks
            q0 = pl.multiple_of((it % blocks) * bq, bq)
            ks = pl.multiple_of(jnp.clip(q0 - n_side, 0, sub - bk), n_side)
            case = (q0 - ks) // n_side
            q_at = lambda ref, cols: ref[res, pl.ds(q0, bq), cols]
            k_at = lambda ref, cols: ref[res, pl.ds(ks, bk), cols]
        else:
            case = 0
            res = pl.ds(pl.multiple_of(it * stack, stack), stack)
            q_at = k_at = lambda ref, cols: ref[res, :, cols].reshape(stack * sub, HEAD_DIM)
        lse_tile = jnp.zeros((bq, LANES), F32)
        for hh in range(hps):
            cols = slice(hh * HEAD_DIM, (hh + 1) * HEAD_DIM)
            q = q_at(q_ref, cols)
            k = k_at(k_ref, cols)
            v = k_at(v_ref, cols)
            s = lax.dot_general(q, k, (((1,), (1,)), ((), ())), preferred_element_type=F32)
            s = s + bias_ref[case, hh]
            m = jnp.max(s, axis=-1, keepdims=True)
            p = jnp.exp2(s - m)
            den = jnp.sum(p, axis=-1, keepdims=True)
            num = jnp.dot(p.astype(BF16), v, preferred_element_type=F32)
            o = (num / den).astype(o_ref.dtype)
            if stack == 1:
                o_ref[res, pl.ds(q0, bq), cols] = o
            else:
                o_ref[res, :, cols] = o.reshape(stack, sub, HEAD_DIM)
            lse_tile = jnp.where(lane == pl.program_id(0) * hps + hh, m + jnp.log2(den), lse_tile)
        if stack == 1:
            lse_ref[res, pl.ds(q0, bq), :] = lse_tile
        else:
            lse_ref[res] = lse_tile.reshape(stack, sub, LANES)
        return carry

    lax.fori_loop(0, n_res * blocks // stack, body, 0, unroll=unroll)


ATTN_CHAINS = 12
ATTN_KEY_BLOCK = 256


def attention_step_shape(sub, n_heads, dilation, rows_per_trip):
    def largest(n, unit):
        return max(c for c in range(1, n + 1) if n % c == 0 and c * unit <= ATTN_BLOCK_BYTES)

    hps = largest(n_heads, sub * HEAD_DIM * 2)
    n_res = largest(dilation, sub * hps * HEAD_DIM * 2)
    trips = n_res * sub // rows_per_trip
    unroll = max(u for u in range(1, trips + 1) if trips % u == 0 and u * hps <= max(ATTN_CHAINS, hps))
    return hps, n_res, unroll


def attention_branch(qkv, slopes, *, window, dilation, n_rows, seq, row0, n_heads):
    tg = n_rows * seq
    sub = seq // dilation
    n_side = (window // 2) // dilation
    if sub >= ATTN_KEY_BLOCK:
        stack, bq, bk, n_cases = 1, ATTN_KEY_BLOCK - 2 * n_side, ATTN_KEY_BLOCK, 3
    else:
        stack = ATTN_KEY_BLOCK // sub
        bq, bk, n_cases = ATTN_KEY_BLOCK, ATTN_KEY_BLOCK, 1
    hps, n_res, unroll = attention_step_shape(sub, n_heads, dilation, bq)
    assert n_res % stack == 0 and sub % (bq // stack) == 0
    gw = hps * HEAD_DIM
    groups = n_heads // hps
    d_attn = n_heads * HEAD_DIM
    rb0 = row0 // seq

    def in_map(part):
        return lambda g, b, r, slopes: (r, rb0 + b, part * groups + g)

    out_map = lambda g, b, r, slopes: (r, b, g)
    o, lse = pl.pallas_call(
        functools.partial(_attn_kernel, dilation=dilation, n_side=n_side, bq=bq, bk=bk, hps=hps,
                          stack=stack, unroll=unroll),
        grid_spec=pltpu.PrefetchScalarGridSpec(
            num_scalar_prefetch=1, grid=(groups, n_rows, dilation // n_res),
            in_specs=[pl.BlockSpec((n_res, sub, gw), in_map(part)) for part in range(3)],
            out_specs=[pl.BlockSpec((n_res, sub, gw), out_map),
                       pl.BlockSpec((n_res, sub, LANES), out_map)],
            scratch_shapes=[pltpu.VMEM((n_cases, hps, bq, bk), F32)]),
        out_shape=[jax.ShapeDtypeStruct((dilation, tg // dilation, d_attn), BF16),
                   jax.ShapeDtypeStruct((dilation, tg // dilation, groups * LANES), F32)],
        compiler_params=_params(3),
        name=f"attention_d{dilation}",
    )(slopes, qkv, qkv, qkv)
    return o, lse


def _rms(x, g):
    return x * lax.rsqrt(jnp.mean(x * x, axis=-1, keepdims=True) + EPS) * g


def _merge_tile(f_ref, o_refs, l_refs, gf_ref, ga_ref, y_ref, ya_ref, on_ref, ln_ref, n_heads):
    tm = y_ref.shape[0]
    lses = []
    for bi, (o_ref, l_ref) in enumerate(zip(o_refs, l_refs)):
        d = o_ref.shape[0]
        blocks = l_ref.shape[2] // LANES
        if d == 1:
            parts = [l_ref[0, :, g * LANES:(g + 1) * LANES] for g in range(blocks)]
        else:
            for r in range(d):
                rows = pl.ds(r, tm // d, stride=d)
                for h in range(n_heads):
                    on_ref[bi, h, rows, :] = o_ref[r, :, h * HEAD_DIM:(h + 1) * HEAD_DIM].astype(F32)
                for g in range(blocks):
                    ln_ref[bi, g, rows, :] = l_ref[r, :, g * LANES:(g + 1) * LANES]
            parts = [ln_ref[bi, g] for g in range(blocks)]
        lses.append(functools.reduce(lambda a, b: a + b, parts))

    top = functools.reduce(jnp.maximum, lses)
    ws = [jnp.exp2(l - top) for l in lses]
    inv = 1.0 / functools.reduce(lambda a, b: a + b, ws)
    ws = [w * inv for w in ws]
    for h in range(n_heads):
        cols = slice(h * HEAD_DIM, (h + 1) * HEAD_DIM)
        acc = None
        for bi, o_ref in enumerate(o_refs):
            o = o_ref[0, :, cols].astype(F32) if o_ref.shape[0] == 1 else on_ref[bi, h]
            term = o * ws[bi][:, h:h + 1]
            acc = term if acc is None else acc + term
        ya_ref[:, cols] = acc
    y_ref[:, :D_FOURIER] = _rms(f_ref[...], gf_ref[...]).astype(y_ref.dtype)
    y_ref[:, D_FOURIER:] = _rms(ya_ref[...], ga_ref[...]).astype(y_ref.dtype)


def _merge_kernel(*refs, n_heads, group_tiles):
    n_br = len(DILATED_BRANCHES)
    per_group = 1 + 2 * n_br
    n_groups = len(group_tiles)
    gf_ref, ga_ref, y_ref, ya_ref, on_ref, ln_ref = refs[n_groups * per_group:]
    i = pl.program_id(0)
    tile0 = 0
    for gi, tiles in enumerate(group_tiles):
        grp = refs[gi * per_group:(gi + 1) * per_group]

        @pl.when(jnp.logical_and(i >= tile0, i < tile0 + tiles))
        def _(grp=grp):
            _merge_tile(grp[0], grp[1:1 + n_br], grp[1 + n_br:], gf_ref, ga_ref, y_ref, ya_ref,
                        on_ref, ln_ref, n_heads)

        tile0 += tiles


def merge_and_norm(group_inputs, g_out_f, g_out_a, *, n_heads, tm=256):
    d_attn = n_heads * HEAD_DIM
    group_tiles = [g[0].shape[0] // tm for g in group_inputs]
    n_tiles = sum(group_tiles)
    n_br = len(DILATED_BRANCHES)
    in_specs, args = [], []
    tile0 = 0
    for (f, os_, ls_), tiles in zip(group_inputs, group_tiles):
        local = lambda i, tile0=tile0, tiles=tiles: jnp.clip(i - tile0, 0, tiles - 1)
        in_specs.append(pl.BlockSpec((tm, D_FOURIER), lambda i, local=local: (local(i), 0)))
        for arr in (*os_, *ls_):
            d = arr.shape[0]
            in_specs.append(pl.BlockSpec((d, tm // d, arr.shape[2]),
                                         lambda i, local=local: (0, local(i), 0)))
        args += [f, *os_, *ls_]
        tile0 += tiles
    fixed = lambda i: (0, 0)
    in_specs += [pl.BlockSpec((1, D_FOURIER), fixed), pl.BlockSpec((1, d_attn), fixed)]
    max_groups = max(l.shape[2] // LANES for g in group_inputs for l in g[2])
    return pl.pallas_call(
        functools.partial(_merge_kernel, n_heads=n_heads, group_tiles=tuple(group_tiles)),
        grid=(n_tiles,),
        in_specs=in_specs,
        out_specs=pl.BlockSpec((tm, D_FOURIER + d_attn), lambda i: (i, 0)),
        out_shape=jax.ShapeDtypeStruct((n_tiles * tm, D_FOURIER + d_attn), BF16),
        scratch_shapes=[pltpu.VMEM((tm, d_attn), F32),
                        pltpu.VMEM((n_br, n_heads, tm, HEAD_DIM), F32),
                        pltpu.VMEM((n_br, max_groups, tm, LANES), F32)],
        compiler_params=_params(1),
        name="merge_and_norm",
    )(*args, g_out_f, g_out_a)


def _dispatch_kernel(ends_ref, used_ref, pos_ref, h_ref, o_ref, zero_ref, sem, zero_sem,
                     *, tm, n_experts, n_tiles):
    i = pl.program_id(0)

    @pl.when(i == 0)
    def _():
        zero_ref[...] = jnp.zeros_like(zero_ref)

        def zero_tile(start):
            return pltpu.make_async_copy(zero_ref, o_ref.at[pl.ds(start, tm)], zero_sem)

        def expert_tiles(act):
            for e in range(n_experts):
                begin = ends_ref[e - 1] if e else 0

                @pl.when(ends_ref[e] > begin)
                def _(e=e):
                    act(zero_tile(ends_ref[e] - tm))

        def tail_tiles(act):
            def body(j, carry):
                act(zero_tile(j * tm))
                return carry
            lax.fori_loop(used_ref[0], n_tiles, body, 0)

        expert_tiles(lambda c: c.start())
        tail_tiles(lambda c: c.start())
        expert_tiles(lambda c: c.wait())
        tail_tiles(lambda c: c.wait())

    def copy(s):
        tok = i * tm + lax.rem(s, tm)
        return pltpu.make_async_copy(h_ref.at[tok], o_ref.at[pos_ref[0, s]], sem)

    def start(s, carry):
        copy(s).start()
        return carry

    def wait(s, carry):
        copy(s).wait()
        return carry

    lax.fori_loop(0, TOP_K * tm, start, 0, unroll=8)
    lax.fori_loop(0, TOP_K * tm, wait, 0, unroll=8)


def dispatch_rows(h3, pos_tiles, ends, used, *, n_rows, tm):
    t = h3.shape[0]
    n_experts = ends.shape[0]
    return pl.pallas_call(
        functools.partial(_dispatch_kernel, tm=tm, n_experts=n_experts, n_tiles=n_rows // tm),
        grid_spec=pltpu.PrefetchScalarGridSpec(
            num_scalar_prefetch=2, grid=(t // tm,),
            in_specs=[pl.BlockSpec((None, 1, TOP_K * tm), lambda i, ends, used: (i, 0, 0),
                                   memory_space=pltpu.SMEM),
                      pl.BlockSpec(memory_space=pl.ANY)],
            out_specs=pl.BlockSpec(memory_space=pl.ANY),
            scratch_shapes=[pltpu.VMEM((tm,) + h3.shape[1:], F32), pltpu.SemaphoreType.DMA(()),
                            pltpu.SemaphoreType.DMA(())]),
        out_shape=jax.ShapeDtypeStruct((n_rows,) + h3.shape[1:], F32),
        compiler_params=_params(1),
        name="moe_dispatch",
    )(ends, used, pos_tiles, h3)


SLAB_STRIDE = 4


def _rows_to_matrix_kernel(x_ref, o_ref, stage_ref):
    tm = o_ref.shape[0]
    per = x_ref.shape[0] // SLAB_STRIDE
    for a in range(SLAB_STRIDE):
        stage_ref[a] = x_ref[pl.ds(a, per, stride=SLAB_STRIDE), :]
    for a in range(SLAB_STRIDE):
        for b in range(per // tm):
            c = SLAB_STRIDE * b + a
            o_ref[:, c * LANES:(c + 1) * LANES] = (
                stage_ref[a, pl.ds(b, tm, stride=per // tm), :].astype(o_ref.dtype))


def rows_to_matrix(x3, *, tm):
    p, chunks, _ = x3.shape
    assert chunks % SLAB_STRIDE == 0
    return pl.pallas_call(
        _rows_to_matrix_kernel,
        grid=(p // tm,),
        in_specs=[pl.BlockSpec((tm * chunks, LANES), lambda i: (i, 0))],
        out_specs=pl.BlockSpec((tm, chunks * LANES), lambda i: (i, 0)),
        out_shape=jax.ShapeDtypeStruct((p, chunks * LANES), BF16),
        scratch_shapes=[pltpu.VMEM((SLAB_STRIDE, tm * chunks // SLAB_STRIDE, LANES), F32)],
        compiler_params=_params(1),
        name="moe_rows_to_matrix",
    )(x3.reshape(p * chunks, LANES))


def _combine_kernel(rows_ref, pos_ref, o_ref, route_ref, x_ref, gate_ref, out_ref, buf_ref, sem):
    del rows_ref
    tm = x_ref.shape[0]

    def copy(r):
        return pltpu.make_async_copy(o_ref.at[pl.ds(pos_ref[0, r], 1)],
                                     buf_ref.at[pl.ds(r, 1)], sem)

    def start(r, carry):
        copy(r).start()
        return carry

    def wait(r, carry):
        copy(r).wait()
        return carry

    lax.fori_loop(0, TOP_K * tm, start, 0, unroll=8)
    lax.fori_loop(0, TOP_K * tm, wait, 0, unroll=8)
    route = route_ref[...]
    y = route[:, 2:3] * buf_ref[pl.ds(0, tm), :] + route[:, 3:4] * buf_ref[pl.ds(tm, tm), :]
    out_ref[...] = x_ref[...] + gate_ref[...] * y


def moe_combine(o_sorted, pos, route, x, mod4, gate_part, rows, *, tm):
    t, d = x.shape
    return pl.pallas_call(
        _combine_kernel,
        grid_spec=pltpu.PrefetchScalarGridSpec(
            num_scalar_prefetch=1, grid=(t // tm,),
            in_specs=[pl.BlockSpec((None, 1, TOP_K * tm), lambda i, rows: (i, 0, 0),
                                   memory_space=pltpu.SMEM),
                      pl.BlockSpec(memory_space=pl.ANY),
                      pl.BlockSpec((tm, LANES), lambda i, rows: (i, 0)),
                      pl.BlockSpec((tm, d), lambda i, rows: (i, 0)),
                      pl.BlockSpec((None, None, 1, d), lambda i, rows: (rows[i], gate_part, 0, 0))],
            out_specs=pl.BlockSpec((tm, d), lambda i, rows: (i, 0)),
            scratch_shapes=[pltpu.VMEM((TOP_K * tm, d), F32), pltpu.SemaphoreType.DMA(())]),
        out_shape=jax.ShapeDtypeStruct((t, d), F32),
        compiler_params=_params(1),
        name="moe_combine",
    )(rows, pos, o_sorted, route, x, mod4)


def moe_dispatch_plan(route, n_experts, tm):
    t = route.shape[0]
    n_slots = t * TOP_K
    p = n_slots + n_experts * tm
    expert = route[:, :TOP_K].astype(jnp.int32).reshape(n_slots)
    onehot = (expert[:, None] == jnp.arange(n_experts)[None, :]).astype(jnp.int32)
    rank = jnp.take_along_axis(jnp.cumsum(onehot, axis=0) - onehot, expert[:, None], axis=1)[:, 0]
    counts = jnp.sum(onehot, axis=0)
    padded = ((counts + tm - 1) // tm) * tm
    ends = jnp.cumsum(padded).astype(jnp.int32)
    starts = ends - padded
    pos = starts[expert] + rank
    tile_start = jnp.arange(p // tm, dtype=jnp.int32) * tm
    tile_expert = jnp.sum((tile_start[:, None] >= ends[None, :]).astype(jnp.int32), axis=1)
    tile_expert = jnp.minimum(tile_expert, n_experts - 1)
    used = (ends[-1:] // tm).astype(jnp.int32)
    return p, pos.reshape(t, TOP_K), tile_expert, used, ends


def _final_norm_kernel(x_ref, g_ref, o_ref):
    o_ref[...] = _rms(x_ref[...], g_ref[...])


def final_norm(x, g, *, row0, n_rows, tm=512):
    d = x.shape[1]
    blk0 = row0 // tm
    return pl.pallas_call(
        _final_norm_kernel,
        grid=(n_rows // tm,),
        in_specs=[pl.BlockSpec((tm, d), lambda i: (blk0 + i, 0)),
                  pl.BlockSpec((1, d), lambda i: (0, 0))],
        out_specs=pl.BlockSpec((tm, d), lambda i: (i, 0)),
        out_shape=jax.ShapeDtypeStruct((n_rows, d), F32),
        compiler_params=_params(1),
        name="final_norm",
    )(x, g)


TM = 512
TM_BIG = 1024
TM_COMBINE = 256


def _tile_rows(groups, tm):
    rows = []
    base = 0
    for n_rows, seq in groups:
        for b in range(n_rows):
            rows += [base + b] * (seq // tm)
        base += n_rows
    return jnp.asarray(np.asarray(rows, np.int32))


def kernel(x_prompt, x_sample, c_prompt, c_sample, w_ada, b_ada, g_norm_mix, g_norm_ff, w_in, w_fmix,
           g_out_f, g_out_a, w_out, w_ff1, w_ff3, w_ff2, w_router, w_e1, w_e3, w_e2, g_final):
    depth, d, d_in = w_in.shape
    groups = [(x_prompt.shape[0], x_prompt.shape[1]), (x_sample.shape[0], x_sample.shape[1])]
    group_row0 = [0, groups[0][0] * groups[0][1]]
    t = sum(b * s for b, s in groups)
    n_req = sum(b for b, _ in groups)
    d_attn = d - D_FOURIER
    n_heads = d_attn // HEAD_DIM
    n_experts = w_router.shape[-1]
    d_ff_e = w_e1.shape[-1]
    dilations = [dil for _, dil in DILATED_BRANCHES]

    x = jnp.concatenate([x_prompt.reshape(-1, d), x_sample.reshape(-1, d)], axis=0)
    c = jnp.concatenate([c_prompt, c_sample], axis=0)
    c = jnp.pad(c, ((0, -n_req % 8), (0, 0)))
    mod = ada_modulation(c, w_ada, b_ada)
    rows = _tile_rows(groups, TM)
    rows_c = _tile_rows(groups, TM_COMBINE)
    dense = Tiling(jnp.zeros((t // TM,), jnp.int32), rows, jnp.full((1,), t // TM, jnp.int32))
    dense_big = Tiling(jnp.zeros((t // TM_BIG,), jnp.int32), _tile_rows(groups, TM_BIG),
                       jnp.full((1,), t // TM_BIG, jnp.int32))
    slopes = jnp.exp2(-8.0 * (jnp.arange(n_heads, dtype=F32) + 1.0) / n_heads)
    ab = fourier_ab(w_fmix)
    dft = [dft_matrix(seq) for _, seq in groups]

    for l in range(depth):
        mod4 = mod[l].reshape(mod.shape[1], 6, 1, d)
        layer = lambda e, l=l: l
        h = norm_modulate(x, rows, g_norm_mix[l:l + 1], mod4, 0, tm=TM)
        u = matmul(h, w_in, dense_big, w_index=layer, tm=TM_BIG, tn=D_FOURIER, out_dtype=BF16,
                   n=D_FOURIER)
        qkvs = matmul_qkv(h, w_in, dense_big, w_index=layer, tm=TM_BIG, tn=512, col0=D_FOURIER,
                          n=3 * d_attn, dilations=dilations, q_cols=d_attn,
                          q_scale=HEAD_DIM ** -0.5 * LOG2_E)

        group_inputs = []
        for gi, ((n_rows, seq), row0) in enumerate(zip(groups, group_row0)):
            pq = fourier_channel_stage(u, ab, l, n_rows=n_rows, seq=seq, row_block0=row0 // TM, tm=TM)
            f = fourier_sequence_stage(dft[gi], pq, n_rows=n_rows, seq=seq, tm=TM)
            branches = [attention_branch(qkv, slopes, window=window, dilation=dilation, n_rows=n_rows,
                                         seq=seq, row0=row0, n_heads=n_heads)
                        for qkv, (window, dilation) in zip(qkvs, DILATED_BRANCHES)]
            group_inputs.append((f, [b[0] for b in branches], [b[1] for b in branches]))
        y = merge_and_norm(group_inputs, g_out_f[l:l + 1], g_out_a[l:l + 1], n_heads=n_heads)
        x = matmul_gated_residual(y, w_out, x, mod4, 2, dense_big, w_index=layer, tm=TM_BIG, tn=1024)

        j = l // 2
        if l % 2 == 0:
            h = norm_modulate(x, rows, g_norm_ff[l:l + 1], mod4, 3, tm=TM)
            act = matmul_swiglu(h, w_ff1, w_ff3, dense_big, w_index=lambda e, j=j: j, tm=TM_BIG, tn=512)
            x = matmul_gated_residual(act, w_ff2, x, mod4, 5, dense, w_index=lambda e, j=j: j,
                                      tm=TM, tn=512)
        else:
            h, route = norm_modulate(x, rows, g_norm_ff[l:l + 1], mod4, 3, out_dtype=F32,
                                     w_router=w_router[j], tm=TM)
            n_sorted, pos, tile_expert, used, ends = moe_dispatch_plan(route, n_experts, TM)
            routed = Tiling(tile_expert, tile_expert, used)

            def choice_major(tm):
                tiles = pos.reshape(t // tm, tm, TOP_K).transpose(0, 2, 1)
                return tiles.reshape(t // tm, 1, TOP_K * tm)

            xs = rows_to_matrix(dispatch_rows(h, choice_major(TM), ends, used, n_rows=n_sorted, tm=TM),
                                tm=TM)
            expert_w = lambda e, j=j: j * n_experts + e
            act = matmul_swiglu(xs, w_e1.reshape(-1, d, d_ff_e), w_e3.reshape(-1, d, d_ff_e), routed,
                                w_index=expert_w, tm=TM, tn=1024)
            o_sorted = matmul(act, w_e2.reshape(-1, d_ff_e, d), routed, w_index=expert_w,
                              tm=TM, tn=512, out_dtype=F32, vmem=VMEM_LIMIT_BIG)
            x = moe_combine(o_sorted, choice_major(TM_COMBINE), route, x, mod4, 5, rows_c, tm=TM_COMBINE)

    g = g_final.reshape(1, d)
    outs = []
    for (n_rows, seq), row0 in zip(groups, group_row0):
        outs.append(final_norm(x, g, row0=row0, n_rows=n_rows * seq, tm=TM).reshape(n_rows, seq, d))
    return tuple(outs)
```

```python
import functools
import math

import numpy as np
import jax
import jax.numpy as jnp
from jax import lax
from jax.experimental import pallas as pl
from jax.experimental.pallas import tpu as pltpu

HEAD_DIM = 128
FOURIER_GROUP = 128
N_FOURIER_GROUPS = 4
D_FOURIER = FOURIER_GROUP * N_FOURIER_GROUPS
DILATED_BRANCHES = ((128, 1), (512, 4), (2048, 16))
TOP_K = 2
EPS = 1e-6
NEG_INF = -1e30
LOG2_E = math.log2(math.e)
LANES = 128
VMEM_LIMIT = 56 * 1024 * 1024
VMEM_LIMIT_BIG = 62 * 1024 * 1024
ATTN_BLOCK_BYTES = 4 * 1024 * 1024

BF16 = jnp.bfloat16
F32 = jnp.float32


def _params(n_axes, vmem=VMEM_LIMIT):
    return pltpu.CompilerParams(
        dimension_semantics=("arbitrary",) * n_axes, vmem_limit_bytes=vmem)


def _ada_kernel(c_ref, w_ref, b_ref, o_ref):
    c = c_ref[...]
    a = (c * jax.nn.sigmoid(c)).astype(BF16)
    acc = jnp.dot(a, w_ref[...].astype(BF16), preferred_element_type=F32)
    o_ref[...] = acc + b_ref[...]


def ada_modulation(c, w_ada, b_ada, tn=1024):
    n_layers, d, n = w_ada.shape
    r8 = c.shape[0]
    return pl.pallas_call(
        _ada_kernel,
        grid=(n_layers, n // tn),
        in_specs=[
            pl.BlockSpec((r8, d), lambda l, j: (0, 0)),
            pl.BlockSpec((None, d, tn), lambda l, j: (l, 0, j)),
            pl.BlockSpec((None, 1, tn), lambda l, j: (l, 0, j)),
        ],
        out_specs=pl.BlockSpec((None, r8, tn), lambda l, j: (l, 0, j)),
        out_shape=jax.ShapeDtypeStruct((n_layers, r8, n), F32),
        compiler_params=_params(2),
        name="ada_modulation",
    )(c, w_ada, b_ada.reshape(n_layers, 1, n))


def _norm_mod(x, g, sc, sh):
    y = x * lax.rsqrt(jnp.mean(x * x, axis=-1, keepdims=True) + EPS)
    return (y * g) * (1.0 + sc) + sh


def _norm_mod_kernel(rows_ref, x_ref, g_ref, sc_ref, sh_ref, h_ref):
    del rows_ref
    h_ref[...] = _norm_mod(x_ref[...], g_ref[...], sc_ref[...], sh_ref[...]).astype(h_ref.dtype)


def _norm_mod_router_kernel(rows_ref, x_ref, g_ref, sc_ref, sh_ref, wr_ref, h_ref, route_ref,
                            stage_ref, *, n_experts):
    del rows_ref
    h = _norm_mod(x_ref[...], g_ref[...], sc_ref[...], sh_ref[...])
    tm = h.shape[0]
    per = h_ref.shape[0] // SLAB_STRIDE
    for a in range(SLAB_STRIDE):
        for b in range(per // tm):
            c = SLAB_STRIDE * b + a
            stage_ref[a, pl.ds(b, tm, stride=per // tm), :] = h[:, c * LANES:(c + 1) * LANES]
    for a in range(SLAB_STRIDE):
        h_ref[pl.ds(a, per, stride=SLAB_STRIDE), :] = stage_ref[a]
    w = wr_ref[...]
    h_hi, w_hi = h.astype(BF16), w.astype(BF16)
    h_lo = (h - h_hi.astype(F32)).astype(BF16)
    w_lo = (w - w_hi.astype(F32)).astype(BF16)
    logits = (jnp.dot(h_hi, w_hi, preferred_element_type=F32)
              + (jnp.dot(h_hi, w_lo, preferred_element_type=F32)
                 + jnp.dot(h_lo, w_hi, preferred_element_type=F32)))
    lane = lax.broadcasted_iota(jnp.int32, logits.shape, 1)
    logits = jnp.where(lane < n_experts, logits, -jnp.inf)
    m1 = jnp.max(logits, axis=-1, keepdims=True)
    i1 = jnp.min(jnp.where(logits == m1, lane, LANES), axis=-1, keepdims=True)
    rest = jnp.where(lane == i1, -jnp.inf, logits)
    m2 = jnp.max(rest, axis=-1, keepdims=True)
    i2 = jnp.min(jnp.where(rest == m2, lane, LANES), axis=-1, keepdims=True)
    e2 = jnp.exp(m2 - m1)
    den = 1.0 + e2
    route = jnp.where(lane == 0, i1.astype(F32), 0.0)
    route = jnp.where(lane == 1, i2.astype(F32), route)
    route = jnp.where(lane == 2, 1.0 / den, route)
    route = jnp.where(lane == 3, e2 / den, route)
    route_ref[...] = route


def norm_modulate(x, rows, g, mod4, layer_part, *, out_dtype=BF16, w_router=None, tm=512):
    t, d = x.shape
    sh_part, sc_part = layer_part, layer_part + 1
    in_specs = [
        pl.BlockSpec((tm, d), lambda i, rows: (i, 0)),
        pl.BlockSpec((1, d), lambda i, rows: (0, 0)),
        pl.BlockSpec((None, None, 1, d), lambda i, rows: (rows[i], sc_part, 0, 0)),
        pl.BlockSpec((None, None, 1, d), lambda i, rows: (rows[i], sh_part, 0, 0)),
    ]
    h_spec = pl.BlockSpec((tm, d), lambda i, rows: (i, 0))
    h_shape = jax.ShapeDtypeStruct((t, d), out_dtype)
    if w_router is None:
        return pl.pallas_call(
            _norm_mod_kernel,
            grid_spec=pltpu.PrefetchScalarGridSpec(
                num_scalar_prefetch=1, grid=(t // tm,), in_specs=in_specs, out_specs=h_spec),
            out_shape=h_shape,
            compiler_params=_params(1),
            name="norm_modulate",
        )(rows, x, g, mod4, mod4)
    n_experts = w_router.shape[1]
    wr = jnp.pad(w_router, ((0, 0), (0, LANES - n_experts)))
    in_specs.append(pl.BlockSpec((d, LANES), lambda i, rows: (0, 0)))
    chunks = d // LANES
    assert chunks % SLAB_STRIDE == 0
    h_flat, route = pl.pallas_call(
        functools.partial(_norm_mod_router_kernel, n_experts=n_experts),
        grid_spec=pltpu.PrefetchScalarGridSpec(
            num_scalar_prefetch=1, grid=(t // tm,), in_specs=in_specs,
            out_specs=[pl.BlockSpec((tm * chunks, LANES), lambda i, rows: (i, 0)),
                       pl.BlockSpec((tm, LANES), lambda i, rows: (i, 0))],
            scratch_shapes=[pltpu.VMEM((SLAB_STRIDE, tm * chunks // SLAB_STRIDE, LANES), F32)]),
        out_shape=[jax.ShapeDtypeStruct((t * chunks, LANES), F32),
                   jax.ShapeDtypeStruct((t, LANES), F32)],
        compiler_params=_params(1),
        name="norm_modulate_router",
    )(rows, x, g, mod4, mod4, wr)
    return h_flat.reshape(t, chunks, LANES), route


CAST_ROWS = 512


def _cast_weights_if_changed(te_ref, w_refs, wb_refs):
    i = pl.program_id(1)
    changed = jnp.logical_or(i == 0, te_ref[i] != te_ref[jnp.maximum(i - 1, 0)])

    @pl.when(changed)
    def _():
        k = w_refs[0].shape[0]
        step = math.gcd(k, CAST_ROWS)

        def body(c, carry):
            r0 = pl.multiple_of(c * step, step)
            for w_ref, wb_ref in zip(w_refs, wb_refs):
                wb_ref[pl.ds(r0, step), :] = w_ref[pl.ds(r0, step), :].astype(BF16)
            return carry

        lax.fori_loop(0, k // step, body, 0)


def _mm_plain_kernel(te_ref, rows_ref, used_ref, a_ref, w_ref, o_ref, wb_ref):
    del rows_ref
    _cast_weights_if_changed(te_ref, (w_ref,), (wb_ref,))

    @pl.when(pl.program_id(1) < used_ref[0])
    def _():
        acc = jnp.dot(a_ref[...], wb_ref[...], preferred_element_type=F32)
        o_ref[...] = acc.astype(o_ref.dtype)

    @pl.when(pl.program_id(1) >= used_ref[0])
    def _():
        o_ref[...] = jnp.zeros_like(o_ref)


def _mm_qkv_kernel(te_ref, rows_ref, used_ref, a_ref, w_ref, *rest, dilations, q_chunks, q_scale):
    del rows_ref, used_ref
    out_refs = rest[:len(dilations)]
    wb_ref, acc_ref, stage_ref = rest[len(dilations):]
    _cast_weights_if_changed(te_ref, (w_ref,), (wb_ref,))
    acc = jnp.dot(a_ref[...], wb_ref[...], preferred_element_type=F32)
    acc = acc * jnp.where(pl.program_id(0) < q_chunks, q_scale, 1.0).astype(F32)
    tm, tn = acc.shape
    slabs = tn // LANES
    for c in range(slabs):
        acc_ref[c] = acc[:, c * LANES:(c + 1) * LANES]
    cur_ref, cur_d = acc_ref, 1
    for idx, (o_ref, d) in enumerate(zip(out_refs, dilations)):
        if d == 1:
            o_ref[0] = acc.astype(o_ref.dtype)
            continue
        f = d // cur_d
        keep = idx + 1 < len(dilations)
        for rp in range(cur_d):
            for b in range(f):
                r = b * cur_d + rp
                for c in range(slabs):
                    piece = cur_ref[c, pl.ds(rp * (tm // cur_d) + b, tm // d, stride=f), :]
                    o_ref[r, :, c * LANES:(c + 1) * LANES] = piece.astype(o_ref.dtype)
                    if keep:
                        stage_ref[c, pl.ds(r * (tm // d), tm // d), :] = piece
        cur_ref, cur_d = stage_ref, d


def _mm_swiglu_kernel(te_ref, rows_ref, used_ref, a_ref, w1_ref, w3_ref, o_ref, wb1_ref, wb3_ref):
    del rows_ref
    _cast_weights_if_changed(te_ref, (w1_ref, w3_ref), (wb1_ref, wb3_ref))

    @pl.when(pl.program_id(1) < used_ref[0])
    def _():
        a = a_ref[...]
        g = jnp.dot(a, wb1_ref[...], preferred_element_type=F32)
        u = jnp.dot(a, wb3_ref[...], preferred_element_type=F32)
        o_ref[...] = ((g * jax.nn.sigmoid(g)) * u).astype(o_ref.dtype)

    @pl.when(pl.program_id(1) >= used_ref[0])
    def _():
        o_ref[...] = jnp.zeros_like(o_ref)


def _mm_resid_kernel(te_ref, rows_ref, used_ref, a_ref, w_ref, x_ref, gate_ref, o_ref, wb_ref):
    del rows_ref, used_ref
    _cast_weights_if_changed(te_ref, (w_ref,), (wb_ref,))
    acc = jnp.dot(a_ref[...], wb_ref[...], preferred_element_type=F32)
    o_ref[...] = x_ref[...] + gate_ref[...] * acc


def _a_spec(tm, k):
    return pl.BlockSpec((tm, k), lambda j, i, te, rows, used: (jnp.minimum(i, used[0] - 1), 0))


def _w_spec(k, tn, w_index, col_block0=0, single_buffer=False):
    mode = dict(pipeline_mode=pl.Buffered(1)) if single_buffer else {}
    return pl.BlockSpec((None, k, tn),
                        lambda j, i, te, rows, used: (w_index(te[i]), 0, col_block0 + j), **mode)


def _out_spec(tm, tn):
    return pl.BlockSpec((tm, tn), lambda j, i, te, rows, used: (i, j))


class Tiling:
    def __init__(self, tile_expert, rows, used):
        self.args = (tile_expert, rows, used)


def matmul_qkv(a, w, tiling, *, w_index, tm, tn, col0, n, dilations, q_cols, q_scale):
    m, k = a.shape
    assert sum(d > 1 for d in dilations) <= 2
    return pl.pallas_call(
        functools.partial(_mm_qkv_kernel, dilations=tuple(dilations), q_chunks=q_cols // tn,
                          q_scale=q_scale),
        grid_spec=pltpu.PrefetchScalarGridSpec(
            num_scalar_prefetch=3, grid=(n // tn, m // tm),
            in_specs=[_a_spec(tm, k), _w_spec(k, tn, w_index, col0 // tn)],
            out_specs=[pl.BlockSpec((d, tm // d, tn), lambda j, i, te, rows, used: (0, i, j))
                       for d in dilations],
            scratch_shapes=[pltpu.VMEM((k, tn), BF16)]
                           + [pltpu.VMEM((tn // LANES, tm, LANES), F32)] * 2),
        out_shape=[jax.ShapeDtypeStruct((d, m // d, n), BF16) for d in dilations],
        compiler_params=_params(2),
        name="matmul_qkv",
    )(*tiling.args, a, w)


def matmul(a, w, tiling, *, w_index, tm, tn, out_dtype, n=None, vmem=VMEM_LIMIT):
    m, k = a.shape
    n = w.shape[-1] if n is None else n
    return pl.pallas_call(
        _mm_plain_kernel,
        grid_spec=pltpu.PrefetchScalarGridSpec(
            num_scalar_prefetch=3, grid=(n // tn, m // tm),
            in_specs=[_a_spec(tm, k), _w_spec(k, tn, w_index)],
            out_specs=_out_spec(tm, tn),
            scratch_shapes=[pltpu.VMEM((k, tn), BF16)]),
        out_shape=jax.ShapeDtypeStruct((m, n), out_dtype),
        compiler_params=_params(2, vmem),
        name="matmul",
    )(*tiling.args, a, w)


def matmul_swiglu(a, w1, w3, tiling, *, w_index, tm, tn):
    m, k = a.shape
    n = w1.shape[-1]
    return pl.pallas_call(
        _mm_swiglu_kernel,
        grid_spec=pltpu.PrefetchScalarGridSpec(
            num_scalar_prefetch=3, grid=(n // tn, m // tm),
            in_specs=[_a_spec(tm, k), _w_spec(k, tn, w_index), _w_spec(k, tn, w_index)],
            out_specs=_out_spec(tm, tn),
            scratch_shapes=[pltpu.VMEM((k, tn), BF16), pltpu.VMEM((k, tn), BF16)]),
        out_shape=jax.ShapeDtypeStruct((m, n), BF16),
        compiler_params=_params(2),
        name="matmul_swiglu",
    )(*tiling.args, a, w1, w3)


def matmul_gated_residual(a, w, x, mod4, gate_part, tiling, *, w_index, tm, tn):
    m, k = a.shape
    n = w.shape[-1]
    return pl.pallas_call(
        _mm_resid_kernel,
        grid_spec=pltpu.PrefetchScalarGridSpec(
            num_scalar_prefetch=3, grid=(n // tn, m // tm),
            in_specs=[_a_spec(tm, k), _w_spec(k, tn, w_index), _out_spec(tm, tn),
                      pl.BlockSpec((None, None, 1, tn),
                                   lambda j, i, te, rows, used: (rows[i], gate_part, 0, j))],
            out_specs=_out_spec(tm, tn),
            scratch_shapes=[pltpu.VMEM((k, tn), BF16)]),
        out_shape=jax.ShapeDtypeStruct((m, n), F32),
        compiler_params=_params(2),
        name="matmul_gated_residual",
    )(*tiling.args, a, w, x, mod4)


def _dft_cos_sin(n):
    j = lax.broadcasted_iota(jnp.int32, (n, n), 0)
    k = lax.broadcasted_iota(jnp.int32, (n, n), 1)
    ang = ((j * k) % n).astype(F32) * (2.0 * math.pi / n)
    return jnp.cos(ang), jnp.sin(ang)


def _fourier_ab_kernel(c_ref, s_ref, w_ref, o_ref):
    w = w_ref[...]
    scale = FOURIER_GROUP ** -0.5
    a = jnp.dot(c_ref[...], w, preferred_element_type=F32, precision=lax.Precision.HIGHEST)
    b = jnp.dot(s_ref[...], w, preferred_element_type=F32, precision=lax.Precision.HIGHEST)
    o_ref[:, :FOURIER_GROUP] = (a * scale).astype(o_ref.dtype)
    o_ref[:, FOURIER_GROUP:] = (b * scale).astype(o_ref.dtype)


def fourier_ab(w_fmix):
    n_layers, n_groups, c, _ = w_fmix.shape
    cos_g, sin_g = _dft_cos_sin(c)
    return pl.pallas_call(
        _fourier_ab_kernel,
        grid=(n_layers, n_groups),
        in_specs=[pl.BlockSpec((c, c), lambda l, g: (0, 0)),
                  pl.BlockSpec((c, c), lambda l, g: (0, 0)),
                  pl.BlockSpec((None, None, c, c), lambda l, g: (l, g, 0, 0))],
        out_specs=pl.BlockSpec((None, None, c, 2 * c), lambda l, g: (l, g, 0, 0)),
        out_shape=jax.ShapeDtypeStruct((n_layers, n_groups, c, 2 * c), BF16),
        compiler_params=_params(2),
        name="fourier_ab",
    )(cos_g, sin_g, w_fmix)


def _fourier_channel_kernel(u_ref, ab_ref, pq_ref):
    c = FOURIER_GROUP
    for g in range(N_FOURIER_GROUPS):
        pq = jnp.dot(u_ref[:, g * c:(g + 1) * c], ab_ref[g], preferred_element_type=F32)
        pq_ref[0, :, g * c:(g + 1) * c] = pq[:, :c].astype(pq_ref.dtype)
        pq_ref[1, :, g * c:(g + 1) * c] = pq[:, c:].astype(pq_ref.dtype)


def fourier_channel_stage(u, ab, layer, *, n_rows, seq, row_block0, tm=512):
    tiles = seq // tm
    return pl.pallas_call(
        _fourier_channel_kernel,
        grid=(n_rows, tiles),
        in_specs=[pl.BlockSpec((tm, D_FOURIER), lambda b, i: (row_block0 + b * tiles + i, 0)),
                  pl.BlockSpec((None, N_FOURIER_GROUPS, FOURIER_GROUP, 2 * FOURIER_GROUP),
                               lambda b, i: (layer, 0, 0, 0))],
        out_specs=pl.BlockSpec((None, 2, tm, D_FOURIER), lambda b, i: (b, 0, i, 0)),
        out_shape=jax.ShapeDtypeStruct((n_rows, 2, seq, D_FOURIER), BF16),
        compiler_params=_params(2),
        name="fourier_channel_stage",
    )(u, ab)


def _fourier_seq_kernel(cs_ref, pq_ref, o_ref, *, scale):
    o_ref[...] = jnp.dot(cs_ref[...], pq_ref[...], preferred_element_type=F32) * scale


def fourier_sequence_stage(cs, pq, *, n_rows, seq, tm=512):
    tiles = seq // tm
    return pl.pallas_call(
        functools.partial(_fourier_seq_kernel, scale=seq ** -0.5),
        grid=(n_rows, tiles),
        in_specs=[pl.BlockSpec((tm, 2 * seq), lambda b, i: (i, 0)),
                  pl.BlockSpec((None, 2 * seq, D_FOURIER), lambda b, i: (b, 0, 0))],
        out_specs=pl.BlockSpec((tm, D_FOURIER), lambda b, i: (b * tiles + i, 0)),
        out_shape=jax.ShapeDtypeStruct((n_rows * seq, D_FOURIER), F32),
        compiler_params=_params(2),
        name="fourier_sequence_stage",
    )(cs, pq.reshape(n_rows, 2 * seq, D_FOURIER))


DFT_SPLIT = 64


def _dft_kernel(t1_ref, t2_ref, o_ref):
    seq = t2_ref.shape[2]
    c1, s1 = t1_ref[0], t1_ref[1]
    c2, s2 = t2_ref[0], t2_ref[1]
    o_ref[:, :seq] = (c1 * c2 - s1 * s2).astype(o_ref.dtype)
    o_ref[:, seq:] = (-(s1 * c2 + c1 * s2)).astype(o_ref.dtype)


def dft_matrix(seq):
    k = jnp.arange(seq, dtype=jnp.int32)

    def table(mult):
        ang = ((mult[:, None] * k[None, :]) % seq).astype(F32) * (2.0 * math.pi / seq)
        return jnp.stack([jnp.cos(ang), jnp.sin(ang)])

    n_coarse = seq // DFT_SPLIT
    t1 = table(jnp.arange(n_coarse, dtype=jnp.int32) * DFT_SPLIT).reshape(2, n_coarse, 1, seq)
    t2 = table(jnp.arange(DFT_SPLIT, dtype=jnp.int32))
    return pl.pallas_call(
        _dft_kernel,
        grid=(n_coarse,),
        in_specs=[pl.BlockSpec((2, None, 1, seq), lambda a: (0, a, 0, 0)),
                  pl.BlockSpec((2, DFT_SPLIT, seq), lambda a: (0, 0, 0))],
        out_specs=pl.BlockSpec((DFT_SPLIT, 2 * seq), lambda a: (a, 0)),
        out_shape=jax.ShapeDtypeStruct((seq, 2 * seq), BF16),
        compiler_params=_params(1),
        name="dft_matrix",
    )(t1, t2)


def _attn_kernel(slopes_ref, q_ref, k_ref, v_ref, o_ref, lse_ref, bias_ref, *, dilation, n_side,
                 bq, bk, hps, stack, unroll):
    n_res, sub = q_ref.shape[:2]
    n_cases = bias_ref.shape[0]
    blocks = sub // bq if stack == 1 else 1
    lane = lax.broadcasted_iota(jnp.int32, (bq, LANES), 1)

    @pl.when(jnp.logical_and(pl.program_id(1) == 0, pl.program_id(2) == 0))
    def _():
        row = lax.broadcasted_iota(jnp.int32, (bq, bk), 0)
        col = lax.broadcasted_iota(jnp.int32, (bq, bk), 1)
        for case in range(n_cases):
            dist = jnp.abs(col - row - case * n_side)
            valid = dist <= n_side
            if stack > 1:
                valid = jnp.logical_and(valid, row // sub == col // sub)
            dist_f = dist.astype(F32) * float(dilation)
            for hh in range(hps):
                slope = slopes_ref[pl.program_id(0) * hps + hh] * LOG2_E
                bias_ref[case, hh] = jnp.where(valid, -slope * dist_f, NEG_INF)

    def body(it, carry):
        if stack == 1:
            res = it // blocks
            q0 = pl.multiple_of((it % blocks) * bq, bq)
            ks = pl.multiple_of(jnp.clip(q0 - n_side, 0, sub - bk), n_side)
            case = (q0 - ks) // n_side
            q_at = lambda ref, cols: ref[res, pl.ds(q0, bq), cols]
            k_at = lambda ref, cols: ref[res, pl.ds(ks, bk), cols]
        else:
            case = 0
            res = pl.ds(pl.multiple_of(it * stack, stack), stack)
            q_at = k_at = lambda ref, cols: ref[res, :, cols].reshape(stack * sub, HEAD_DIM)
        lse_tile = jnp.zeros((bq, LANES), F32)
        for hh in range(hps):
            cols = slice(hh * HEAD_DIM, (hh + 1) * HEAD_DIM)
            q = q_at(q_ref, cols)
            k = k_at(k_ref, cols)
            v = k_at(v_ref, cols)
            s = lax.dot_general(q, k, (((1,), (1,)), ((), ())), preferred_element_type=F32)
            s = s + bias_ref[case, hh]
            m = jnp.max(s, axis=-1, keepdims=True)
            p = jnp.exp2(s - m)
            den = jnp.sum(p, axis=-1, keepdims=True)
            num = jnp.dot(p.astype(BF16), v, preferred_element_type=F32)
            o = (num / den).astype(o_ref.dtype)
            if stack == 1:
                o_ref[res, pl.ds(q0, bq), cols] = o
            else:
                o_ref[res, :, cols] = o.reshape(stack, sub, HEAD_DIM)
            lse_tile = jnp.where(lane == pl.program_id(0) * hps + hh, m + jnp.log2(den), lse_tile)
        if stack == 1:
            lse_ref[res, pl.ds(q0, bq), :] = lse_tile
        else:
            lse_ref[res] = lse_tile.reshape(stack, sub, LANES)
        return carry

    lax.fori_loop(0, n_res * blocks // stack, body, 0, unroll=unroll)


ATTN_CHAINS = 12
ATTN_KEY_BLOCK = 256


def attention_step_shape(sub, n_heads, dilation, rows_per_trip):
    def largest(n, unit):
        return max(c for c in range(1, n + 1) if n % c == 0 and c * unit <= ATTN_BLOCK_BYTES)

    hps = largest(n_heads, sub * HEAD_DIM * 2)
    n_res = largest(dilation, sub * hps * HEAD_DIM * 2)
    trips = n_res * sub // rows_per_trip
    unroll = max(u for u in range(1, trips + 1) if trips % u == 0 and u * hps <= max(ATTN_CHAINS, hps))
    return hps, n_res, unroll


def attention_branch(qkv, slopes, *, window, dilation, n_rows, seq, row0, n_heads):
    tg = n_rows * seq
    sub = seq // dilation
    n_side = (window // 2) // dilation
    if sub >= ATTN_KEY_BLOCK:
        stack, bq, bk, n_cases = 1, ATTN_KEY_BLOCK - 2 * n_side, ATTN_KEY_BLOCK, 3
    else:
        stack = ATTN_KEY_BLOCK // sub
        bq, bk, n_cases = ATTN_KEY_BLOCK, ATTN_KEY_BLOCK, 1
    hps, n_res, unroll = attention_step_shape(sub, n_heads, dilation, bq)
    assert n_res % stack == 0 and sub % (bq // stack) == 0
    gw = hps * HEAD_DIM
    groups = n_heads // hps
    d_attn = n_heads * HEAD_DIM
    rb0 = row0 // seq

    def in_map(part):
        return lambda g, b, r, slopes: (r, rb0 + b, part * groups + g)

    out_map = lambda g, b, r, slopes: (r, b, g)
    o, lse = pl.pallas_call(
        functools.partial(_attn_kernel, dilation=dilation, n_side=n_side, bq=bq, bk=bk, hps=hps,
                          stack=stack, unroll=unroll),
        grid_spec=pltpu.PrefetchScalarGridSpec(
            num_scalar_prefetch=1, grid=(groups, n_rows, dilation // n_res),
            in_specs=[pl.BlockSpec((n_res, sub, gw), in_map(part)) for part in range(3)],
            out_specs=[pl.BlockSpec((n_res, sub, gw), out_map),
                       pl.BlockSpec((n_res, sub, LANES), out_map)],
            scratch_shapes=[pltpu.VMEM((n_cases, hps, bq, bk), F32)]),
        out_shape=[jax.ShapeDtypeStruct((dilation, tg // dilation, d_attn), BF16),
                   jax.ShapeDtypeStruct((dilation, tg // dilation, groups * LANES), F32)],
        compiler_params=_params(3),
        name=f"attention_d{dilation}",
    )(slopes, qkv, qkv, qkv)
    return o, lse


def _rms(x, g):
    return x * lax.rsqrt(jnp.mean(x * x, axis=-1, keepdims=True) + EPS) * g


def _merge_tile(f_ref, o_refs, l_refs, gf_ref, ga_ref, y_ref, ya_ref, on_ref, ln_ref, n_heads):
    tm = y_ref.shape[0]
    lses = []
    for bi, (o_ref, l_ref) in enumerate(zip(o_refs, l_refs)):
        d = o_ref.shape[0]
        blocks = l_ref.shape[2] // LANES
        if d == 1:
            parts = [l_ref[0, :, g * LANES:(g + 1) * LANES] for g in range(blocks)]
        else:
            for r in range(d):
                rows = pl.ds(r, tm // d, stride=d)
                for h in range(n_heads):
                    on_ref[bi, h, rows, :] = o_ref[r, :, h * HEAD_DIM:(h + 1) * HEAD_DIM].astype(F32)
                for g in range(blocks):
                    ln_ref[bi, g, rows, :] = l_ref[r, :, g * LANES:(g + 1) * LANES]
            parts = [ln_ref[bi, g] for g in range(blocks)]
        lses.append(functools.reduce(lambda a, b: a + b, parts))

    top = functools.reduce(jnp.maximum, lses)
    ws = [jnp.exp2(l - top) for l in lses]
    inv = 1.0 / functools.reduce(lambda a, b: a + b, ws)
    ws = [w * inv for w in ws]
    for h in range(n_heads):
        cols = slice(h * HEAD_DIM, (h + 1) * HEAD_DIM)
        acc = None
        for bi, o_ref in enumerate(o_refs):
            o = o_ref[0, :, cols].astype(F32) if o_ref.shape[0] == 1 else on_ref[bi, h]
            term = o * ws[bi][:, h:h + 1]
            acc = term if acc is None else acc + term
        ya_ref[:, cols] = acc
    y_ref[:, :D_FOURIER] = _rms(f_ref[...], gf_ref[...]).astype(y_ref.dtype)
    y_ref[:, D_FOURIER:] = _rms(ya_ref[...], ga_ref[...]).astype(y_ref.dtype)


def _merge_kernel(*refs, n_heads, group_tiles):
    n_br = len(DILATED_BRANCHES)
    per_group = 1 + 2 * n_br
    n_groups = len(group_tiles)
    gf_ref, ga_ref, y_ref, ya_ref, on_ref, ln_ref = refs[n_groups * per_group:]
    i = pl.program_id(0)
    tile0 = 0
    for gi, tiles in enumerate(group_tiles):
        grp = refs[gi * per_group:(gi + 1) * per_group]

        @pl.when(jnp.logical_and(i >= tile0, i < tile0 + tiles))
        def _(grp=grp):
            _merge_tile(grp[0], grp[1:1 + n_br], grp[1 + n_br:], gf_ref, ga_ref, y_ref, ya_ref,
                        on_ref, ln_ref, n_heads)

        tile0 += tiles


def merge_and_norm(group_inputs, g_out_f, g_out_a, *, n_heads, tm=256):
    d_attn = n_heads * HEAD_DIM
    group_tiles = [g[0].shape[0] // tm for g in group_inputs]
    n_tiles = sum(group_tiles)
    n_br = len(DILATED_BRANCHES)
    in_specs, args = [], []
    tile0 = 0
    for (f, os_, ls_), tiles in zip(group_inputs, group_tiles):
        local = lambda i, tile0=tile0, tiles=tiles: jnp.clip(i - tile0, 0, tiles - 1)
        in_specs.append(pl.BlockSpec((tm, D_FOURIER), lambda i, local=local: (local(i), 0)))
        for arr in (*os_, *ls_):
            d = arr.shape[0]
            in_specs.append(pl.BlockSpec((d, tm // d, arr.shape[2]),
                                         lambda i, local=local: (0, local(i), 0)))
        args += [f, *os_, *ls_]
        tile0 += tiles
    fixed = lambda i: (0, 0)
    in_specs += [pl.BlockSpec((1, D_FOURIER), fixed), pl.BlockSpec((1, d_attn), fixed)]
    max_groups = max(l.shape[2] // LANES for g in group_inputs for l in g[2])
    return pl.pallas_call(
        functools.partial(_merge_kernel, n_heads=n_heads, group_tiles=tuple(group_tiles)),
        grid=(n_tiles,),
        in_specs=in_specs,
        out_specs=pl.BlockSpec((tm, D_FOURIER + d_attn), lambda i: (i, 0)),
        out_shape=jax.ShapeDtypeStruct((n_tiles * tm, D_FOURIER + d_attn), BF16),
        scratch_shapes=[pltpu.VMEM((tm, d_attn), F32),
                        pltpu.VMEM((n_br, n_heads, tm, HEAD_DIM), F32),
                        pltpu.VMEM((n_br, max_groups, tm, LANES), F32)],
        compiler_params=_params(1),
        name="merge_and_norm",
    )(*args, g_out_f, g_out_a)


def _dispatch_kernel(ends_ref, used_ref, pos_ref, h_ref, o_ref, zero_ref, sem, zero_sem,
                     *, tm, n_experts, n_tiles):
    i = pl.program_id(0)

    @pl.when(i == 0)
    def _():
        zero_ref[...] = jnp.zeros_like(zero_ref)

        def zero_tile(start):
            return pltpu.make_async_copy(zero_ref, o_ref.at[pl.ds(start, tm)], zero_sem)

        def expert_tiles(act):
            for e in range(n_experts):
                begin = ends_ref[e - 1] if e else 0

                @pl.when(ends_ref[e] > begin)
                def _(e=e):
                    act(zero_tile(ends_ref[e] - tm))

        def tail_tiles(act):
            def body(j, carry):
                act(zero_tile(j * tm))
                return carry
            lax.fori_loop(used_ref[0], n_tiles, body, 0)

        expert_tiles(lambda c: c.start())
        tail_tiles(lambda c: c.start())
        expert_tiles(lambda c: c.wait())
        tail_tiles(lambda c: c.wait())

    def copy(s):
        tok = lax.rem(s, tm)
        return pltpu.make_async_copy(h_ref.at[tok], o_ref.at[pos_ref[0, s]], sem)

    def start(s, carry):
        copy(s).start()
        return carry

    def wait(s, carry):
        copy(s).wait()
        return carry

    lax.fori_loop(0, TOP_K * tm, start, 0, unroll=8)
    lax.fori_loop(0, TOP_K * tm, wait, 0, unroll=8)


def dispatch_rows(h3, pos_tiles, ends, used, *, n_rows, tm):
    t = h3.shape[0]
    n_experts = ends.shape[0]
    return pl.pallas_call(
        functools.partial(_dispatch_kernel, tm=tm, n_experts=n_experts, n_tiles=n_rows // tm),
        grid_spec=pltpu.PrefetchScalarGridSpec(
            num_scalar_prefetch=2, grid=(t // tm,),
            in_specs=[pl.BlockSpec((None, 1, TOP_K * tm), lambda i, ends, used: (i, 0, 0),
                                   memory_space=pltpu.SMEM),
                      pl.BlockSpec((tm,) + h3.shape[1:], lambda i, ends, used: (i, 0, 0))],
            out_specs=pl.BlockSpec(memory_space=pl.ANY),
            scratch_shapes=[pltpu.VMEM((tm,) + h3.shape[1:], F32), pltpu.SemaphoreType.DMA(()),
                            pltpu.SemaphoreType.DMA(())]),
        out_shape=jax.ShapeDtypeStruct((n_rows,) + h3.shape[1:], F32),
        compiler_params=_params(1),
        name="moe_dispatch",
    )(ends, used, pos_tiles, h3)


SLAB_STRIDE = 4


def _rows_to_matrix_kernel(x_ref, o_ref, stage_ref):
    tm = o_ref.shape[0]
    per = x_ref.shape[0] // SLAB_STRIDE
    for a in range(SLAB_STRIDE):
        stage_ref[a] = x_ref[pl.ds(a, per, stride=SLAB_STRIDE), :]
    for a in range(SLAB_STRIDE):
        for b in range(per // tm):
            c = SLAB_STRIDE * b + a
            o_ref[:, c * LANES:(c + 1) * LANES] = (
                stage_ref[a, pl.ds(b, tm, stride=per // tm), :].astype(o_ref.dtype))


def rows_to_matrix(x3, *, tm):
    p, chunks, _ = x3.shape
    assert chunks % SLAB_STRIDE == 0
    return pl.pallas_call(
        _rows_to_matrix_kernel,
        grid=(p // tm,),
        in_specs=[pl.BlockSpec((tm * chunks, LANES), lambda i: (i, 0))],
        out_specs=pl.BlockSpec((tm, chunks * LANES), lambda i: (i, 0)),
        out_shape=jax.ShapeDtypeStruct((p, chunks * LANES), BF16),
        scratch_shapes=[pltpu.VMEM((SLAB_STRIDE, tm * chunks // SLAB_STRIDE, LANES), F32)],
        compiler_params=_params(1),
        name="moe_rows_to_matrix",
    )(x3.reshape(p * chunks, LANES))


def _combine_kernel(rows_ref, pos_ref, o_ref, route_ref, x_ref, gate_ref, out_ref, buf_ref, sem):
    del rows_ref
    tm = x_ref.shape[0]

    def copy(r):
        return pltpu.make_async_copy(o_ref.at[pl.ds(pos_ref[0, r], 1)],
                                     buf_ref.at[pl.ds(r, 1)], sem)

    def start(r, carry):
        copy(r).start()
        return carry

    def wait(r, carry):
        copy(r).wait()
        return carry

    lax.fori_loop(0, TOP_K * tm, start, 0, unroll=8)
    lax.fori_loop(0, TOP_K * tm, wait, 0, unroll=8)
    route = route_ref[...]
    y = route[:, 2:3] * buf_ref[pl.ds(0, tm), :] + route[:, 3:4] * buf_ref[pl.ds(tm, tm), :]
    out_ref[...] = x_ref[...] + gate_ref[...] * y


def moe_combine(o_sorted, pos, route, x, mod4, gate_part, rows, *, tm):
    t, d = x.shape
    return pl.pallas_call(
        _combine_kernel,
        grid_spec=pltpu.PrefetchScalarGridSpec(
            num_scalar_prefetch=1, grid=(t // tm,),
            in_specs=[pl.BlockSpec((None, 1, TOP_K * tm), lambda i, rows: (i, 0, 0),
                                   memory_space=pltpu.SMEM),
                      pl.BlockSpec(memory_space=pl.ANY),
                      pl.BlockSpec((tm, LANES), lambda i, rows: (i, 0)),
                      pl.BlockSpec((tm, d), lambda i, rows: (i, 0)),
                      pl.BlockSpec((None, None, 1, d), lambda i, rows: (rows[i], gate_part, 0, 0))],
            out_specs=pl.BlockSpec((tm, d), lambda i, rows: (i, 0)),
            scratch_shapes=[pltpu.VMEM((TOP_K * tm, d), F32), pltpu.SemaphoreType.DMA(())]),
        out_shape=jax.ShapeDtypeStruct((t, d), F32),
        compiler_params=_params(1),
        name="moe_combine",
    )(rows, pos, o_sorted, route, x, mod4)


def moe_dispatch_plan(route, n_experts, tm):
    t = route.shape[0]
    n_slots = t * TOP_K
    p = n_slots + n_experts * tm
    expert = route[:, :TOP_K].astype(jnp.int32).reshape(n_slots)
    onehot = (expert[:, None] == jnp.arange(n_experts)[None, :]).astype(jnp.int32)
    rank = jnp.take_along_axis(jnp.cumsum(onehot, axis=0) - onehot, expert[:, None], axis=1)[:, 0]
    counts = jnp.sum(onehot, axis=0)
    padded = ((counts + tm - 1) // tm) * tm
    ends = jnp.cumsum(padded).astype(jnp.int32)
    starts = ends - padded
    pos = starts[expert] + rank
    tile_start = jnp.arange(p // tm, dtype=jnp.int32) * tm
    tile_expert = jnp.sum((tile_start[:, None] >= ends[None, :]).astype(jnp.int32), axis=1)
    tile_expert = jnp.minimum(tile_expert, n_experts - 1)
    used = (ends[-1:] // tm).astype(jnp.int32)
    return p, pos.reshape(t, TOP_K), tile_expert, used, ends


def _final_norm_kernel(x_ref, g_ref, o_ref):
    o_ref[...] = _rms(x_ref[...], g_ref[...])


def final_norm(x, g, *, row0, n_rows, tm=512):
    d = x.shape[1]
    blk0 = row0 // tm
    return pl.pallas_call(
        _final_norm_kernel,
        grid=(n_rows // tm,),
        in_specs=[pl.BlockSpec((tm, d), lambda i: (blk0 + i, 0)),
                  pl.BlockSpec((1, d), lambda i: (0, 0))],
        out_specs=pl.BlockSpec((tm, d), lambda i: (i, 0)),
        out_shape=jax.ShapeDtypeStruct((n_rows, d), F32),
        compiler_params=_params(1),
        name="final_norm",
    )(x, g)


TM = 512
TM_BIG = 1024
TM_COMBINE = 256


def _tile_rows(groups, tm):
    rows = []
    base = 0
    for n_rows, seq in groups:
        for b in range(n_rows):
            rows += [base + b] * (seq // tm)
        base += n_rows
    return jnp.asarray(np.asarray(rows, np.int32))


def kernel(x_prompt, x_sample, c_prompt, c_sample, w_ada, b_ada, g_norm_mix, g_norm_ff, w_in, w_fmix,
           g_out_f, g_out_a, w_out, w_ff1, w_ff3, w_ff2, w_router, w_e1, w_e3, w_e2, g_final):
    depth, d, d_in = w_in.shape
    groups = [(x_prompt.shape[0], x_prompt.shape[1]), (x_sample.shape[0], x_sample.shape[1])]
    group_row0 = [0, groups[0][0] * groups[0][1]]
    t = sum(b * s for b, s in groups)
    n_req = sum(b for b, _ in groups)
    d_attn = d - D_FOURIER
    n_heads = d_attn // HEAD_DIM
    n_experts = w_router.shape[-1]
    d_ff_e = w_e1.shape[-1]
    dilations = [dil for _, dil in DILATED_BRANCHES]

    x = jnp.concatenate([x_prompt.reshape(-1, d), x_sample.reshape(-1, d)], axis=0)
    c = jnp.concatenate([c_prompt, c_sample], axis=0)
    c = jnp.pad(c, ((0, -n_req % 8), (0, 0)))
    mod = ada_modulation(c, w_ada, b_ada)
    rows = _tile_rows(groups, TM)
    rows_c = _tile_rows(groups, TM_COMBINE)
    dense = Tiling(jnp.zeros((t // TM,), jnp.int32), rows, jnp.full((1,), t // TM, jnp.int32))
    dense_big = Tiling(jnp.zeros((t // TM_BIG,), jnp.int32), _tile_rows(groups, TM_BIG),
                       jnp.full((1,), t // TM_BIG, jnp.int32))
    slopes = jnp.exp2(-8.0 * (jnp.arange(n_heads, dtype=F32) + 1.0) / n_heads)
    ab = fourier_ab(w_fmix)
    dft = [dft_matrix(seq) for _, seq in groups]

    for l in range(depth):
        mod4 = mod[l].reshape(mod.shape[1], 6, 1, d)
        layer = lambda e, l=l: l
        h = norm_modulate(x, rows, g_norm_mix[l:l + 1], mod4, 0, tm=TM)
        u = matmul(h, w_in, dense_big, w_index=layer, tm=TM_BIG, tn=D_FOURIER, out_dtype=BF16,
                   n=D_FOURIER)
        qkvs = matmul_qkv(h, w_in, dense_big, w_index=layer, tm=TM_BIG, tn=512, col0=D_FOURIER,
                          n=3 * d_attn, dilations=dilations, q_cols=d_attn,
                          q_scale=HEAD_DIM ** -0.5 * LOG2_E)

        group_inputs = []
        for gi, ((n_rows, seq), row0) in enumerate(zip(groups, group_row0)):
            pq = fourier_channel_stage(u, ab, l, n_rows=n_rows, seq=seq, row_block0=row0 // TM, tm=TM)
            f = fourier_sequence_stage(dft[gi], pq, n_rows=n_rows, seq=seq, tm=TM)
            branches = [attention_branch(qkv, slopes, window=window, dilation=dilation, n_rows=n_rows,
                                         seq=seq, row0=row0, n_heads=n_heads)
                        for qkv, (window, dilation) in zip(qkvs, DILATED_BRANCHES)]
            group_inputs.append((f, [b[0] for b in branches], [b[1] for b in branches]))
        y = merge_and_norm(group_inputs, g_out_f[l:l + 1], g_out_a[l:l + 1], n_heads=n_heads)
        x = matmul_gated_residual(y, w_out, x, mod4, 2, dense_big, w_index=layer, tm=TM_BIG, tn=1024)

        j = l // 2
        if l % 2 == 0:
            h = norm_modulate(x, rows, g_norm_ff[l:l + 1], mod4, 3, tm=TM)
            act = matmul_swiglu(h, w_ff1, w_ff3, dense_big, w_index=lambda e, j=j: j, tm=TM_BIG, tn=512)
            x = matmul_gated_residual(act, w_ff2, x, mod4, 5, dense, w_index=lambda e, j=j: j,
                                      tm=TM, tn=512)
        else:
            h, route = norm_modulate(x, rows, g_norm_ff[l:l + 1], mod4, 3, out_dtype=F32,
                                     w_router=w_router[j], tm=TM)
            n_sorted, pos, tile_expert, used, ends = moe_dispatch_plan(route, n_experts, TM)
            routed = Tiling(tile_expert, tile_expert, used)

            def choice_major(tm):
                tiles = pos.reshape(t // tm, tm, TOP_K).transpose(0, 2, 1)
                return tiles.reshape(t // tm, 1, TOP_K * tm)

            xs = rows_to_matrix(dispatch_rows(h, choice_major(TM), ends, used, n_rows=n_sorted, tm=TM),
                                tm=TM)
            expert_w = lambda e, j=j: j * n_experts + e
            act = matmul_swiglu(xs, w_e1.reshape(-1, d, d_ff_e), w_e3.reshape(-1, d, d_ff_e), routed,
                                w_index=expert_w, tm=TM, tn=1024)
            o_sorted = matmul(act, w_e2.reshape(-1, d_ff_e, d), routed, w_index=expert_w,
                              tm=TM, tn=512, out_dtype=F32, vmem=VMEM_LIMIT_BIG)
            x = moe_combine(o_sorted, choice_major(TM_COMBINE), route, x, mod4, 5, rows_c, tm=TM_COMBINE)

    g = g_final.reshape(1, d)
    outs = []
    for (n_rows, seq), row0 in zip(groups, group_row0):
        outs.append(final_norm(x, g, row0=row0, n_rows=n_rows * seq, tm=TM).reshape(n_rows, seq, d))
    return tuple(outs)
```

```python
import functools
import math

import numpy as np
import jax
import jax.numpy as jnp
from jax import lax
from jax.experimental import pallas as pl
from jax.experimental.pallas import tpu as pltpu

HEAD_DIM = 128
FOURIER_GROUP = 128
N_FOURIER_GROUPS = 4
D_FOURIER = FOURIER_GROUP * N_FOURIER_GROUPS
DILATED_BRANCHES = ((128, 1), (512, 4), (2048, 16))
TOP_K = 2
EPS = 1e-6
NEG_INF = -1e30
LOG2_E = math.log2(math.e)
LANES = 128
VMEM_LIMIT = 56 * 1024 * 1024
VMEM_LIMIT_BIG = 62 * 1024 * 1024
ATTN_BLOCK_BYTES = 4 * 1024 * 1024

BF16 = jnp.bfloat16
F32 = jnp.float32


def _params(n_axes, vmem=VMEM_LIMIT):
    return pltpu.CompilerParams(
        dimension_semantics=("arbitrary",) * n_axes, vmem_limit_bytes=vmem)


def _ada_kernel(c_ref, w_ref, b_ref, o_ref):
    c = c_ref[...]
    a = (c * jax.nn.sigmoid(c)).astype(BF16)
    acc = jnp.dot(a, w_ref[...].astype(BF16), preferred_element_type=F32)
    o_ref[...] = acc + b_ref[...]


def ada_modulation(c, w_ada, b_ada, tn=1024):
    n_layers, d, n = w_ada.shape
    r8 = c.shape[0]
    return pl.pallas_call(
        _ada_kernel,
        grid=(n_layers, n // tn),
        in_specs=[
            pl.BlockSpec((r8, d), lambda l, j: (0, 0)),
            pl.BlockSpec((None, d, tn), lambda l, j: (l, 0, j)),
            pl.BlockSpec((None, 1, tn), lambda l, j: (l, 0, j)),
        ],
        out_specs=pl.BlockSpec((None, r8, tn), lambda l, j: (l, 0, j)),
        out_shape=jax.ShapeDtypeStruct((n_layers, r8, n), F32),
        compiler_params=_params(2),
        name="ada_modulation",
    )(c, w_ada, b_ada.reshape(n_layers, 1, n))


def _norm_mod(x, g, sc, sh):
    y = x * lax.rsqrt(jnp.mean(x * x, axis=-1, keepdims=True) + EPS)
    return (y * g) * (1.0 + sc) + sh


def _norm_mod_kernel(rows_ref, x_ref, g_ref, sc_ref, sh_ref, h_ref):
    del rows_ref
    h_ref[...] = _norm_mod(x_ref[...], g_ref[...], sc_ref[...], sh_ref[...]).astype(h_ref.dtype)


def _norm_mod_router_kernel(rows_ref, x_ref, g_ref, sc_ref, sh_ref, wr_ref, h_ref, route_ref,
                            stage_ref, *, n_experts):
    del rows_ref
    h = _norm_mod(x_ref[...], g_ref[...], sc_ref[...], sh_ref[...])
    tm = h.shape[0]
    per = h_ref.shape[0] // SLAB_STRIDE
    for a in range(SLAB_STRIDE):
        for b in range(per // tm):
            c = SLAB_STRIDE * b + a
            stage_ref[a, pl.ds(b, tm, stride=per // tm), :] = h[:, c * LANES:(c + 1) * LANES]
    for a in range(SLAB_STRIDE):
        h_ref[pl.ds(a, per, stride=SLAB_STRIDE), :] = stage_ref[a]
    w = wr_ref[...]
    h_hi, w_hi = h.astype(BF16), w.astype(BF16)
    h_lo = (h - h_hi.astype(F32)).astype(BF16)
    w_lo = (w - w_hi.astype(F32)).astype(BF16)
    logits = (jnp.dot(h_hi, w_hi, preferred_element_type=F32)
              + (jnp.dot(h_hi, w_lo, preferred_element_type=F32)
                 + jnp.dot(h_lo, w_hi, preferred_element_type=F32)))
    lane = lax.broadcasted_iota(jnp.int32, logits.shape, 1)
    logits = jnp.where(lane < n_experts, logits, -jnp.inf)
    m1 = jnp.max(logits, axis=-1, keepdims=True)
    i1 = jnp.min(jnp.where(logits == m1, lane, LANES), axis=-1, keepdims=True)
    rest = jnp.where(lane == i1, -jnp.inf, logits)
    m2 = jnp.max(rest, axis=-1, keepdims=True)
    i2 = jnp.min(jnp.where(rest == m2, lane, LANES), axis=-1, keepdims=True)
    e2 = jnp.exp(m2 - m1)
    den = 1.0 + e2
    route = jnp.where(lane == 0, i1.astype(F32), 0.0)
    route = jnp.where(lane == 1, i2.astype(F32), route)
    route = jnp.where(lane == 2, 1.0 / den, route)
    route = jnp.where(lane == 3, e2 / den, route)
    route_ref[...] = route


def norm_modulate(x, rows, g, mod4, layer_part, *, out_dtype=BF16, w_router=None, tm=512):
    t, d = x.shape
    sh_part, sc_part = layer_part, layer_part + 1
    in_specs = [
        pl.BlockSpec((tm, d), lambda i, rows: (i, 0)),
        pl.BlockSpec((1, d), lambda i, rows: (0, 0)),
        pl.BlockSpec((None, None, 1, d), lambda i, rows: (rows[i], sc_part, 0, 0)),
        pl.BlockSpec((None, None, 1, d), lambda i, rows: (rows[i], sh_part, 0, 0)),
    ]
    h_spec = pl.BlockSpec((tm, d), lambda i, rows: (i, 0))
    h_shape = jax.ShapeDtypeStruct((t, d), out_dtype)
    if w_router is None:
        return pl.pallas_call(
            _norm_mod_kernel,
            grid_spec=pltpu.PrefetchScalarGridSpec(
                num_scalar_prefetch=1, grid=(t // tm,), in_specs=in_specs, out_specs=h_spec),
            out_shape=h_shape,
            compiler_params=_params(1),
            name="norm_modulate",
        )(rows, x, g, mod4, mod4)
    n_experts = w_router.shape[1]
    wr = jnp.pad(w_router, ((0, 0), (0, LANES - n_experts)))
    in_specs.append(pl.BlockSpec((d, LANES), lambda i, rows: (0, 0)))
    chunks = d // LANES
    assert chunks % SLAB_STRIDE == 0
    h_flat, route = pl.pallas_call(
        functools.partial(_norm_mod_router_kernel, n_experts=n_experts),
        grid_spec=pltpu.PrefetchScalarGridSpec(
            num_scalar_prefetch=1, grid=(t // tm,), in_specs=in_specs,
            out_specs=[pl.BlockSpec((tm * chunks, LANES), lambda i, rows: (i, 0)),
                       pl.BlockSpec((tm, LANES), lambda i, rows: (i, 0))],
            scratch_shapes=[pltpu.VMEM((SLAB_STRIDE, tm * chunks // SLAB_STRIDE, LANES), F32)]),
        out_shape=[jax.ShapeDtypeStruct((t * chunks, LANES), F32),
                   jax.ShapeDtypeStruct((t, LANES), F32)],
        compiler_params=_params(1),
        name="norm_modulate_router",
    )(rows, x, g, mod4, mod4, wr)
    return h_flat.reshape(t, chunks, LANES), route


CAST_ROWS = 512


def _cast_weights_if_changed(te_ref, w_refs, wb_refs):
    i = pl.program_id(1)
    changed = jnp.logical_or(i == 0, te_ref[i] != te_ref[jnp.maximum(i - 1, 0)])

    @pl.when(changed)
    def _():
        k = w_refs[0].shape[0]
        step = math.gcd(k, CAST_ROWS)

        def body(c, carry):
            r0 = pl.multiple_of(c * step, step)
            for w_ref, wb_ref in zip(w_refs, wb_refs):
                wb_ref[pl.ds(r0, step), :] = w_ref[pl.ds(r0, step), :].astype(BF16)
            return carry

        lax.fori_loop(0, k // step, body, 0)


ROW_PARTS = 4


def _for_valid_rows(valid_ref, o_ref, compute):
    tm = o_ref.shape[0]
    quantum = tm // ROW_PARTS
    quanta = (valid_ref[pl.program_id(1)] + quantum - 1) // quantum
    for n in range(ROW_PARTS + 1):
        rows = n * quantum

        @pl.when(quanta == n)
        def _(rows=rows):
            if rows:
                o_ref[:rows, :] = compute(rows).astype(o_ref.dtype)
            if rows < tm:
                o_ref[rows:, :] = jnp.zeros((tm - rows, o_ref.shape[1]), o_ref.dtype)


def _mm_plain_kernel(te_ref, rows_ref, used_ref, valid_ref, a_ref, w_ref, o_ref, wb_ref):
    del rows_ref, used_ref
    _cast_weights_if_changed(te_ref, (w_ref,), (wb_ref,))
    _for_valid_rows(valid_ref, o_ref, lambda rows: jnp.dot(
        a_ref[:rows, :], wb_ref[...], preferred_element_type=F32))


def _mm_qkv_kernel(te_ref, rows_ref, used_ref, valid_ref, a_ref, w_ref, *rest, dilations, q_chunks,
                   q_scale):
    del rows_ref, used_ref, valid_ref
    out_refs = rest[:len(dilations)]
    wb_ref, acc_ref, stage_ref = rest[len(dilations):]
    _cast_weights_if_changed(te_ref, (w_ref,), (wb_ref,))
    acc = jnp.dot(a_ref[...], wb_ref[...], preferred_element_type=F32)
    acc = acc * jnp.where(pl.program_id(0) < q_chunks, q_scale, 1.0).astype(F32)
    tm, tn = acc.shape
    slabs = tn // LANES
    for c in range(slabs):
        acc_ref[c] = acc[:, c * LANES:(c + 1) * LANES]
    cur_ref, cur_d = acc_ref, 1
    for idx, (o_ref, d) in enumerate(zip(out_refs, dilations)):
        if d == 1:
            o_ref[0] = acc.astype(o_ref.dtype)
            continue
        f = d // cur_d
        keep = idx + 1 < len(dilations)
        for rp in range(cur_d):
            for b in range(f):
                r = b * cur_d + rp
                for c in range(slabs):
                    piece = cur_ref[c, pl.ds(rp * (tm // cur_d) + b, tm // d, stride=f), :]
                    o_ref[r, :, c * LANES:(c + 1) * LANES] = piece.astype(o_ref.dtype)
                    if keep:
                        stage_ref[c, pl.ds(r * (tm // d), tm // d), :] = piece
        cur_ref, cur_d = stage_ref, d


def _mm_swiglu_kernel(te_ref, rows_ref, used_ref, valid_ref, a_ref, w1_ref, w3_ref, o_ref, wb1_ref,
                      wb3_ref):
    del rows_ref, used_ref
    _cast_weights_if_changed(te_ref, (w1_ref, w3_ref), (wb1_ref, wb3_ref))

    def compute(rows):
        a = a_ref[:rows, :]
        g = jnp.dot(a, wb1_ref[...], preferred_element_type=F32)
        u = jnp.dot(a, wb3_ref[...], preferred_element_type=F32)
        return (g * jax.nn.sigmoid(g)) * u

    _for_valid_rows(valid_ref, o_ref, compute)


def _mm_resid_kernel(te_ref, rows_ref, used_ref, valid_ref, a_ref, w_ref, x_ref, gate_ref, o_ref,
                     wb_ref):
    del rows_ref, used_ref, valid_ref
    _cast_weights_if_changed(te_ref, (w_ref,), (wb_ref,))
    acc = jnp.dot(a_ref[...], wb_ref[...], preferred_element_type=F32)
    o_ref[...] = x_ref[...] + gate_ref[...] * acc


def _a_spec(tm, k):
    return pl.BlockSpec((tm, k), lambda j, i, te, rows, used, valid: (jnp.minimum(i, used[0] - 1), 0))


def _w_spec(k, tn, w_index, col_block0=0, single_buffer=False):
    mode = dict(pipeline_mode=pl.Buffered(1)) if single_buffer else {}
    return pl.BlockSpec((None, k, tn),
                        lambda j, i, te, rows, used, valid: (w_index(te[i]), 0, col_block0 + j), **mode)


def _out_spec(tm, tn):
    return pl.BlockSpec((tm, tn), lambda j, i, te, rows, used, valid: (i, j))


class Tiling:
    def __init__(self, tile_expert, rows, used, valid):
        self.args = (tile_expert, rows, used, valid)


def matmul_qkv(a, w, tiling, *, w_index, tm, tn, col0, n, dilations, q_cols, q_scale):
    m, k = a.shape
    assert sum(d > 1 for d in dilations) <= 2
    return pl.pallas_call(
        functools.partial(_mm_qkv_kernel, dilations=tuple(dilations), q_chunks=q_cols // tn,
                          q_scale=q_scale),
        grid_spec=pltpu.PrefetchScalarGridSpec(
            num_scalar_prefetch=4, grid=(n // tn, m // tm),
            in_specs=[_a_spec(tm, k), _w_spec(k, tn, w_index, col0 // tn)],
            out_specs=[pl.BlockSpec((d, tm // d, tn), lambda j, i, te, rows, used, valid: (0, i, j))
                       for d in dilations],
            scratch_shapes=[pltpu.VMEM((k, tn), BF16)]
                           + [pltpu.VMEM((tn // LANES, tm, LANES), F32)] * 2),
        out_shape=[jax.ShapeDtypeStruct((d, m // d, n), BF16) for d in dilations],
        compiler_params=_params(2),
        name="matmul_qkv",
    )(*tiling.args, a, w)


def matmul(a, w, tiling, *, w_index, tm, tn, out_dtype, n=None, vmem=VMEM_LIMIT):
    m, k = a.shape
    n = w.shape[-1] if n is None else n
    return pl.pallas_call(
        _mm_plain_kernel,
        grid_spec=pltpu.PrefetchScalarGridSpec(
            num_scalar_prefetch=4, grid=(n // tn, m // tm),
            in_specs=[_a_spec(tm, k), _w_spec(k, tn, w_index)],
            out_specs=_out_spec(tm, tn),
            scratch_shapes=[pltpu.VMEM((k, tn), BF16)]),
        out_shape=jax.ShapeDtypeStruct((m, n), out_dtype),
        compiler_params=_params(2, vmem),
        name="matmul",
    )(*tiling.args, a, w)


def matmul_swiglu(a, w1, w3, tiling, *, w_index, tm, tn):
    m, k = a.shape
    n = w1.shape[-1]
    return pl.pallas_call(
        _mm_swiglu_kernel,
        grid_spec=pltpu.PrefetchScalarGridSpec(
            num_scalar_prefetch=4, grid=(n // tn, m // tm),
            in_specs=[_a_spec(tm, k), _w_spec(k, tn, w_index), _w_spec(k, tn, w_index)],
            out_specs=_out_spec(tm, tn),
            scratch_shapes=[pltpu.VMEM((k, tn), BF16), pltpu.VMEM((k, tn), BF16)]),
        out_shape=jax.ShapeDtypeStruct((m, n), BF16),
        compiler_params=_params(2),
        name="matmul_swiglu",
    )(*tiling.args, a, w1, w3)


def matmul_gated_residual(a, w, x, mod4, gate_part, tiling, *, w_index, tm, tn):
    m, k = a.shape
    n = w.shape[-1]
    return pl.pallas_call(
        _mm_resid_kernel,
        grid_spec=pltpu.PrefetchScalarGridSpec(
            num_scalar_prefetch=4, grid=(n // tn, m // tm),
            in_specs=[_a_spec(tm, k), _w_spec(k, tn, w_index), _out_spec(tm, tn),
                      pl.BlockSpec((None, None, 1, tn),
                                   lambda j, i, te, rows, used, valid: (rows[i], gate_part, 0, j))],
            out_specs=_out_spec(tm, tn),
            scratch_shapes=[pltpu.VMEM((k, tn), BF16)]),
        out_shape=jax.ShapeDtypeStruct((m, n), F32),
        compiler_params=_params(2),
        name="matmul_gated_residual",
    )(*tiling.args, a, w, x, mod4)


def _dft_cos_sin(n):
    j = lax.broadcasted_iota(jnp.int32, (n, n), 0)
    k = lax.broadcasted_iota(jnp.int32, (n, n), 1)
    ang = ((j * k) % n).astype(F32) * (2.0 * math.pi / n)
    return jnp.cos(ang), jnp.sin(ang)


def _fourier_ab_kernel(c_ref, s_ref, w_ref, o_ref):
    w = w_ref[...]
    scale = FOURIER_GROUP ** -0.5
    a = jnp.dot(c_ref[...], w, preferred_element_type=F32, precision=lax.Precision.HIGHEST)
    b = jnp.dot(s_ref[...], w, preferred_element_type=F32, precision=lax.Precision.HIGHEST)
    o_ref[:, :FOURIER_GROUP] = (a * scale).astype(o_ref.dtype)
    o_ref[:, FOURIER_GROUP:] = (b * scale).astype(o_ref.dtype)


def fourier_ab(w_fmix):
    n_layers, n_groups, c, _ = w_fmix.shape
    cos_g, sin_g = _dft_cos_sin(c)
    return pl.pallas_call(
        _fourier_ab_kernel,
        grid=(n_layers, n_groups),
        in_specs=[pl.BlockSpec((c, c), lambda l, g: (0, 0)),
                  pl.BlockSpec((c, c), lambda l, g: (0, 0)),
                  pl.BlockSpec((None, None, c, c), lambda l, g: (l, g, 0, 0))],
        out_specs=pl.BlockSpec((None, None, c, 2 * c), lambda l, g: (l, g, 0, 0)),
        out_shape=jax.ShapeDtypeStruct((n_layers, n_groups, c, 2 * c), BF16),
        compiler_params=_params(2),
        name="fourier_ab",
    )(cos_g, sin_g, w_fmix)


def _fourier_channel_kernel(u_ref, ab_ref, pq_ref):
    c = FOURIER_GROUP
    for g in range(N_FOURIER_GROUPS):
        pq = jnp.dot(u_ref[:, g * c:(g + 1) * c], ab_ref[g], preferred_element_type=F32)
        pq_ref[0, :, g * c:(g + 1) * c] = pq[:, :c].astype(pq_ref.dtype)
        pq_ref[1, :, g * c:(g + 1) * c] = pq[:, c:].astype(pq_ref.dtype)


def fourier_channel_stage(u, ab, layer, *, n_rows, seq, row_block0, tm=512):
    tiles = seq // tm
    return pl.pallas_call(
        _fourier_channel_kernel,
        grid=(n_rows, tiles),
        in_specs=[pl.BlockSpec((tm, D_FOURIER), lambda b, i: (row_block0 + b * tiles + i, 0)),
                  pl.BlockSpec((None, N_FOURIER_GROUPS, FOURIER_GROUP, 2 * FOURIER_GROUP),
                               lambda b, i: (layer, 0, 0, 0))],
        out_specs=pl.BlockSpec((None, 2, tm, D_FOURIER), lambda b, i: (b, 0, i, 0)),
        out_shape=jax.ShapeDtypeStruct((n_rows, 2, seq, D_FOURIER), BF16),
        compiler_params=_params(2),
        name="fourier_channel_stage",
    )(u, ab)


def _fourier_seq_kernel(cs_ref, pq_ref, o_ref, *, scale):
    o_ref[...] = jnp.dot(cs_ref[...], pq_ref[...], preferred_element_type=F32) * scale


def fourier_sequence_stage(cs, pq, *, n_rows, seq, tm=512):
    tiles = seq // tm
    return pl.pallas_call(
        functools.partial(_fourier_seq_kernel, scale=seq ** -0.5),
        grid=(n_rows, tiles),
        in_specs=[pl.BlockSpec((tm, 2 * seq), lambda b, i: (i, 0)),
                  pl.BlockSpec((None, 2 * seq, D_FOURIER), lambda b, i: (b, 0, 0))],
        out_specs=pl.BlockSpec((tm, D_FOURIER), lambda b, i: (b * tiles + i, 0)),
        out_shape=jax.ShapeDtypeStruct((n_rows * seq, D_FOURIER), F32),
        compiler_params=_params(2),
        name="fourier_sequence_stage",
    )(cs, pq.reshape(n_rows, 2 * seq, D_FOURIER))


DFT_SPLIT = 64


def _dft_kernel(t1_ref, t2_ref, o_ref):
    seq = t2_ref.shape[2]
    c1, s1 = t1_ref[0], t1_ref[1]
    c2, s2 = t2_ref[0], t2_ref[1]
    o_ref[:, :seq] = (c1 * c2 - s1 * s2).astype(o_ref.dtype)
    o_ref[:, seq:] = (-(s1 * c2 + c1 * s2)).astype(o_ref.dtype)


def dft_matrix(seq):
    k = jnp.arange(seq, dtype=jnp.int32)

    def table(mult):
        ang = ((mult[:, None] * k[None, :]) % seq).astype(F32) * (2.0 * math.pi / seq)
        return jnp.stack([jnp.cos(ang), jnp.sin(ang)])

    n_coarse = seq // DFT_SPLIT
    t1 = table(jnp.arange(n_coarse, dtype=jnp.int32) * DFT_SPLIT).reshape(2, n_coarse, 1, seq)
    t2 = table(jnp.arange(DFT_SPLIT, dtype=jnp.int32))
    return pl.pallas_call(
        _dft_kernel,
        grid=(n_coarse,),
        in_specs=[pl.BlockSpec((2, None, 1, seq), lambda a: (0, a, 0, 0)),
                  pl.BlockSpec((2, DFT_SPLIT, seq), lambda a: (0, 0, 0))],
        out_specs=pl.BlockSpec((DFT_SPLIT, 2 * seq), lambda a: (a, 0)),
        out_shape=jax.ShapeDtypeStruct((seq, 2 * seq), BF16),
        compiler_params=_params(1),
        name="dft_matrix",
    )(t1, t2)


def _attn_kernel(slopes_ref, q_ref, k_ref, v_ref, o_ref, lse_ref, bias_ref, *, dilation, n_side,
                 bq, bk, hps, stack, unroll):
    n_res, sub = q_ref.shape[:2]
    n_cases = bias_ref.shape[0]
    blocks = sub // bq if stack == 1 else 1
    lane = lax.broadcasted_iota(jnp.int32, (bq, LANES), 1)

    @pl.when(jnp.logical_and(pl.program_id(1) == 0, pl.program_id(2) == 0))
    def _():
        row = lax.broadcasted_iota(jnp.int32, (bq, bk), 0)
        col = lax.broadcasted_iota(jnp.int32, (bq, bk), 1)
        for case in range(n_cases):
            dist = jnp.abs(col - row - case * n_side)
            valid = dist <= n_side
            if stack > 1:
                valid = jnp.logical_and(valid, row // sub == col // sub)
            dist_f = dist.astype(F32) * float(dilation)
            for hh in range(hps):
                slope = slopes_ref[pl.program_id(0) * hps + hh] * LOG2_E
                bias_ref[case, hh] = jnp.where(valid, -slope * dist_f, NEG_INF)

    def body(it, carry):
        if stack == 1:
            res = it // blocks
            q0 = pl.multiple_of((it % blocks) * bq, bq)
            ks = pl.multiple_of(jnp.clip(q0 - n_side, 0, sub - bk), n_side)
            case = (q0 - ks) // n_side
            q_at = lambda ref, cols: ref[res, pl.ds(q0, bq), cols]
            k_at = lambda ref, cols: ref[res, pl.ds(ks, bk), cols]
        else:
            case = 0
            res = pl.ds(pl.multiple_of(it * stack, stack), stack)
            q_at = k_at = lambda ref, cols: ref[res, :, cols].reshape(stack * sub, HEAD_DIM)
        lse_tile = jnp.zeros((bq, LANES), F32)
        for hh in range(hps):
            cols = slice(hh * HEAD_DIM, (hh + 1) * HEAD_DIM)
            q = q_at(q_ref, cols)
            k = k_at(k_ref, cols)
            v = k_at(v_ref, cols)
            s = lax.dot_general(q, k, (((1,), (1,)), ((), ())), preferred_element_type=F32)
            s = s + bias_ref[case, hh]
            m = jnp.max(s, axis=-1, keepdims=True)
            p = jnp.exp2(s - m)
            den = jnp.sum(p, axis=-1, keepdims=True)
            num = jnp.dot(p.astype(BF16), v, preferred_element_type=F32)
            o = (num / den).astype(o_ref.dtype)
            if stack == 1:
                o_ref[res, pl.ds(q0, bq), cols] = o
            else:
                o_ref[res, :, cols] = o.reshape(stack, sub, HEAD_DIM)
            lse_tile = jnp.where(lane == pl.program_id(0) * hps + hh, m + jnp.log2(den), lse_tile)
        if stack == 1:
            lse_ref[res, pl.ds(q0, bq), :] = lse_tile
        else:
            lse_ref[res] = lse_tile.reshape(stack, sub, LANES)
        return carry

    lax.fori_loop(0, n_res * blocks // stack, body, 0, unroll=unroll)


ATTN_CHAINS = 12
ATTN_KEY_BLOCK = 256


def attention_step_shape(sub, n_heads, dilation, rows_per_trip):
    def largest(n, unit):
        return max(c for c in range(1, n + 1) if n % c == 0 and c * unit <= ATTN_BLOCK_BYTES)

    hps = largest(n_heads, sub * HEAD_DIM * 2)
    n_res = largest(dilation, sub * hps * HEAD_DIM * 2)
    trips = n_res * sub // rows_per_trip
    unroll = max(u for u in range(1, trips + 1) if trips % u == 0 and u * hps <= max(ATTN_CHAINS, hps))
    return hps, n_res, unroll


def attention_branch(qkv, slopes, *, window, dilation, n_rows, seq, row0, n_heads):
    tg = n_rows * seq
    sub = seq // dilation
    n_side = (window // 2) // dilation
    if sub >= ATTN_KEY_BLOCK:
        stack, bq, bk, n_cases = 1, ATTN_KEY_BLOCK - 2 * n_side, ATTN_KEY_BLOCK, 3
    else:
        stack = ATTN_KEY_BLOCK // sub
        bq, bk, n_cases = ATTN_KEY_BLOCK, ATTN_KEY_BLOCK, 1
    hps, n_res, unroll = attention_step_shape(sub, n_heads, dilation, bq)
    assert n_res % stack == 0 and sub % (bq // stack) == 0
    gw = hps * HEAD_DIM
    groups = n_heads // hps
    d_attn = n_heads * HEAD_DIM
    rb0 = row0 // seq

    def in_map(part):
        return lambda g, b, r, slopes: (r, rb0 + b, part * groups + g)

    out_map = lambda g, b, r, slopes: (r, b, g)
    o, lse = pl.pallas_call(
        functools.partial(_attn_kernel, dilation=dilation, n_side=n_side, bq=bq, bk=bk, hps=hps,
                          stack=stack, unroll=unroll),
        grid_spec=pltpu.PrefetchScalarGridSpec(
            num_scalar_prefetch=1, grid=(groups, n_rows, dilation // n_res),
            in_specs=[pl.BlockSpec((n_res, sub, gw), in_map(part)) for part in range(3)],
            out_specs=[pl.BlockSpec((n_res, sub, gw), out_map),
                       pl.BlockSpec((n_res, sub, LANES), out_map)],
            scratch_shapes=[pltpu.VMEM((n_cases, hps, bq, bk), F32)]),
        out_shape=[jax.ShapeDtypeStruct((dilation, tg // dilation, d_attn), BF16),
                   jax.ShapeDtypeStruct((dilation, tg // dilation, groups * LANES), F32)],
        compiler_params=_params(3),
        name=f"attention_d{dilation}",
    )(slopes, qkv, qkv, qkv)
    return o, lse


def _rms(x, g):
    return x * lax.rsqrt(jnp.mean(x * x, axis=-1, keepdims=True) + EPS) * g


def _merge_tile(f_ref, o_refs, l_refs, gf_ref, ga_ref, y_ref, ya_ref, on_ref, ln_ref, tmp_ref, n_heads):
    tm = y_ref.shape[0]
    lses = []
    for bi, (o_ref, l_ref) in enumerate(zip(o_refs, l_refs)):
        d = o_ref.shape[0]
        blocks = l_ref.shape[2] // LANES
        if d == 1:
            parts = [l_ref[0, :, g * LANES:(g + 1) * LANES] for g in range(blocks)]
        else:
            def to_positions(src, dst, tmp):
                if d <= SLAB_STRIDE:
                    for r in range(d):
                        dst[pl.ds(r, tm // d, stride=d), :] = src(r)
                    return
                f = d // SLAB_STRIDE
                for a in range(SLAB_STRIDE):
                    for b in range(f):
                        tmp[a, pl.ds(b, tm // d, stride=f), :] = src(SLAB_STRIDE * b + a)
                    dst[pl.ds(a, tm // SLAB_STRIDE, stride=SLAB_STRIDE), :] = tmp[a]

            for h in range(n_heads):
                cols = slice(h * HEAD_DIM, (h + 1) * HEAD_DIM)
                to_positions(lambda r: o_ref[r, :, cols].astype(F32), on_ref.at[bi, h], tmp_ref.at[h])
            for g in range(blocks):
                cols = slice(g * LANES, (g + 1) * LANES)
                to_positions(lambda r: l_ref[r, :, cols], ln_ref.at[bi, g], tmp_ref.at[n_heads + g])
            parts = [ln_ref[bi, g] for g in range(blocks)]
        lses.append(functools.reduce(lambda a, b: a + b, parts))

    top = functools.reduce(jnp.maximum, lses)
    ws = [jnp.exp2(l - top) for l in lses]
    inv = 1.0 / functools.reduce(lambda a, b: a + b, ws)
    ws = [w * inv for w in ws]
    for h in range(n_heads):
        cols = slice(h * HEAD_DIM, (h + 1) * HEAD_DIM)
        acc = None
        for bi, o_ref in enumerate(o_refs):
            o = o_ref[0, :, cols].astype(F32) if o_ref.shape[0] == 1 else on_ref[bi, h]
            term = o * ws[bi][:, h:h + 1]
            acc = term if acc is None else acc + term
        ya_ref[:, cols] = acc
    y_ref[:, :D_FOURIER] = _rms(f_ref[...], gf_ref[...]).astype(y_ref.dtype)
    y_ref[:, D_FOURIER:] = _rms(ya_ref[...], ga_ref[...]).astype(y_ref.dtype)


def _merge_kernel(*refs, n_heads, group_tiles):
    n_br = len(DILATED_BRANCHES)
    per_group = 1 + 2 * n_br
    n_groups = len(group_tiles)
    gf_ref, ga_ref, y_ref, ya_ref, on_ref, ln_ref, tmp_ref = refs[n_groups * per_group:]
    i = pl.program_id(0)
    tile0 = 0
    for gi, tiles in enumerate(group_tiles):
        grp = refs[gi * per_group:(gi + 1) * per_group]

        @pl.when(jnp.logical_and(i >= tile0, i < tile0 + tiles))
        def _(grp=grp):
            _merge_tile(grp[0], grp[1:1 + n_br], grp[1 + n_br:], gf_ref, ga_ref, y_ref, ya_ref,
                        on_ref, ln_ref, tmp_ref, n_heads)

        tile0 += tiles


def merge_and_norm(group_inputs, g_out_f, g_out_a, *, n_heads, tm=256):
    d_attn = n_heads * HEAD_DIM
    group_tiles = [g[0].shape[0] // tm for g in group_inputs]
    n_tiles = sum(group_tiles)
    n_br = len(DILATED_BRANCHES)
    in_specs, args = [], []
    tile0 = 0
    for (f, os_, ls_), tiles in zip(group_inputs, group_tiles):
        local = lambda i, tile0=tile0, tiles=tiles: jnp.clip(i - tile0, 0, tiles - 1)
        in_specs.append(pl.BlockSpec((tm, D_FOURIER), lambda i, local=local: (local(i), 0)))
        for arr in (*os_, *ls_):
            d = arr.shape[0]
            in_specs.append(pl.BlockSpec((d, tm // d, arr.shape[2]),
                                         lambda i, local=local: (0, local(i), 0)))
        args += [f, *os_, *ls_]
        tile0 += tiles
    fixed = lambda i: (0, 0)
    in_specs += [pl.BlockSpec((1, D_FOURIER), fixed), pl.BlockSpec((1, d_attn), fixed)]
    max_groups = max(l.shape[2] // LANES for g in group_inputs for l in g[2])
    return pl.pallas_call(
        functools.partial(_merge_kernel, n_heads=n_heads, group_tiles=tuple(group_tiles)),
        grid=(n_tiles,),
        in_specs=in_specs,
        out_specs=pl.BlockSpec((tm, D_FOURIER + d_attn), lambda i: (i, 0)),
        out_shape=jax.ShapeDtypeStruct((n_tiles * tm, D_FOURIER + d_attn), BF16),
        scratch_shapes=[pltpu.VMEM((tm, d_attn), F32),
                        pltpu.VMEM((n_br, n_heads, tm, HEAD_DIM), F32),
                        pltpu.VMEM((n_br, max_groups, tm, LANES), F32),
                        pltpu.VMEM((n_heads + max_groups, SLAB_STRIDE, tm // SLAB_STRIDE, LANES), F32)],
        compiler_params=_params(1),
        name="merge_and_norm",
    )(*args, g_out_f, g_out_a)


def _dispatch_kernel(ends_ref, used_ref, pos_ref, h_ref, o_ref, zero_ref, sem, zero_sem,
                     *, tm, n_experts, n_tiles):
    i = pl.program_id(0)

    @pl.when(i == 0)
    def _():
        zero_ref[...] = jnp.zeros_like(zero_ref)

        def zero_tile(start):
            return pltpu.make_async_copy(zero_ref, o_ref.at[pl.ds(start, tm)], zero_sem)

        def expert_tiles(act):
            for e in range(n_experts):
                begin = ends_ref[e - 1] if e else 0

                @pl.when(ends_ref[e] > begin)
                def _(e=e):
                    act(zero_tile(ends_ref[e] - tm))

        def tail_tiles(act):
            def body(j, carry):
                act(zero_tile(j * tm))
                return carry
            lax.fori_loop(used_ref[0], n_tiles, body, 0)

        expert_tiles(lambda c: c.start())
        tail_tiles(lambda c: c.start())
        expert_tiles(lambda c: c.wait())
        tail_tiles(lambda c: c.wait())

    def copy(s):
        tok = lax.rem(s, tm)
        return pltpu.make_async_copy(h_ref.at[tok], o_ref.at[pos_ref[0, s]], sem)

    def start(s, carry):
        copy(s).start()
        return carry

    def wait(s, carry):
        copy(s).wait()
        return carry

    lax.fori_loop(0, TOP_K * tm, start, 0, unroll=8)
    lax.fori_loop(0, TOP_K * tm, wait, 0, unroll=8)


def dispatch_rows(h3, pos_tiles, ends, used, *, n_rows, tm):
    t = h3.shape[0]
    n_experts = ends.shape[0]
    return pl.pallas_call(
        functools.partial(_dispatch_kernel, tm=tm, n_experts=n_experts, n_tiles=n_rows // tm),
        grid_spec=pltpu.PrefetchScalarGridSpec(
            num_scalar_prefetch=2, grid=(t // tm,),
            in_specs=[pl.BlockSpec((None, 1, TOP_K * tm), lambda i, ends, used: (i, 0, 0),
                                   memory_space=pltpu.SMEM),
                      pl.BlockSpec((tm,) + h3.shape[1:], lambda i, ends, used: (i, 0, 0))],
            out_specs=pl.BlockSpec(memory_space=pl.ANY),
            scratch_shapes=[pltpu.VMEM((tm,) + h3.shape[1:], F32), pltpu.SemaphoreType.DMA(()),
                            pltpu.SemaphoreType.DMA(())]),
        out_shape=jax.ShapeDtypeStruct((n_rows,) + h3.shape[1:], F32),
        compiler_params=_params(1),
        name="moe_dispatch",
    )(ends, used, pos_tiles, h3)


SLAB_STRIDE = 4


def _rows_to_matrix_kernel(x_ref, o_ref, stage_ref):
    tm = o_ref.shape[0]
    per = x_ref.shape[0] // SLAB_STRIDE
    for a in range(SLAB_STRIDE):
        stage_ref[a] = x_ref[pl.ds(a, per, stride=SLAB_STRIDE), :]
    for a in range(SLAB_STRIDE):
        for b in range(per // tm):
            c = SLAB_STRIDE * b + a
            o_ref[:, c * LANES:(c + 1) * LANES] = (
                stage_ref[a, pl.ds(b, tm, stride=per // tm), :].astype(o_ref.dtype))


def rows_to_matrix(x3, *, tm):
    p, chunks, _ = x3.shape
    assert chunks % SLAB_STRIDE == 0
    return pl.pallas_call(
        _rows_to_matrix_kernel,
        grid=(p // tm,),
        in_specs=[pl.BlockSpec((tm * chunks, LANES), lambda i: (i, 0))],
        out_specs=pl.BlockSpec((tm, chunks * LANES), lambda i: (i, 0)),
        out_shape=jax.ShapeDtypeStruct((p, chunks * LANES), BF16),
        scratch_shapes=[pltpu.VMEM((SLAB_STRIDE, tm * chunks // SLAB_STRIDE, LANES), F32)],
        compiler_params=_params(1),
        name="moe_rows_to_matrix",
    )(x3.reshape(p * chunks, LANES))


def _combine_kernel(rows_ref, pos_ref, o_ref, route_ref, x_ref, gate_ref, out_ref, buf_ref, sem):
    del rows_ref
    tm = x_ref.shape[0]

    def copy(r):
        return pltpu.make_async_copy(o_ref.at[pl.ds(pos_ref[0, r], 1)],
                                     buf_ref.at[pl.ds(r, 1)], sem)

    def start(r, carry):
        copy(r).start()
        return carry

    def wait(r, carry):
        copy(r).wait()
        return carry

    lax.fori_loop(0, TOP_K * tm, start, 0, unroll=8)
    lax.fori_loop(0, TOP_K * tm, wait, 0, unroll=8)
    route = route_ref[...]
    y = route[:, 2:3] * buf_ref[pl.ds(0, tm), :] + route[:, 3:4] * buf_ref[pl.ds(tm, tm), :]
    out_ref[...] = x_ref[...] + gate_ref[...] * y


def moe_combine(o_sorted, pos, route, x, mod4, gate_part, rows, *, tm):
    t, d = x.shape
    return pl.pallas_call(
        _combine_kernel,
        grid_spec=pltpu.PrefetchScalarGridSpec(
            num_scalar_prefetch=1, grid=(t // tm,),
            in_specs=[pl.BlockSpec((None, 1, TOP_K * tm), lambda i, rows: (i, 0, 0),
                                   memory_space=pltpu.SMEM),
                      pl.BlockSpec(memory_space=pl.ANY),
                      pl.BlockSpec((tm, LANES), lambda i, rows: (i, 0)),
                      pl.BlockSpec((tm, d), lambda i, rows: (i, 0)),
                      pl.BlockSpec((None, None, 1, d), lambda i, rows: (rows[i], gate_part, 0, 0))],
            out_specs=pl.BlockSpec((tm, d), lambda i, rows: (i, 0)),
            scratch_shapes=[pltpu.VMEM((TOP_K * tm, d), F32), pltpu.SemaphoreType.DMA(())]),
        out_shape=jax.ShapeDtypeStruct((t, d), F32),
        compiler_params=_params(1),
        name="moe_combine",
    )(rows, pos, o_sorted, route, x, mod4)


def moe_dispatch_plan(route, n_experts, tm):
    t = route.shape[0]
    n_slots = t * TOP_K
    p = n_slots + n_experts * tm
    expert = route[:, :TOP_K].astype(jnp.int32).reshape(n_slots)
    onehot = (expert[:, None] == jnp.arange(n_experts)[None, :]).astype(jnp.int32)
    rank = jnp.take_along_axis(jnp.cumsum(onehot, axis=0) - onehot, expert[:, None], axis=1)[:, 0]
    counts = jnp.sum(onehot, axis=0)
    padded = ((counts + tm - 1) // tm) * tm
    ends = jnp.cumsum(padded).astype(jnp.int32)
    starts = ends - padded
    pos = starts[expert] + rank
    tile_start = jnp.arange(p // tm, dtype=jnp.int32) * tm
    tile_expert = jnp.sum((tile_start[:, None] >= ends[None, :]).astype(jnp.int32), axis=1)
    tile_expert = jnp.minimum(tile_expert, n_experts - 1)
    used = (ends[-1:] // tm).astype(jnp.int32)
    valid = jnp.clip((starts + counts)[tile_expert] - tile_start, 0, tm).astype(jnp.int32)
    return p, pos.reshape(t, TOP_K), tile_expert, used, ends, valid


def _final_norm_kernel(x_ref, g_ref, o_ref):
    o_ref[...] = _rms(x_ref[...], g_ref[...])


def final_norm(x, g, *, row0, n_rows, tm=512):
    d = x.shape[1]
    blk0 = row0 // tm
    return pl.pallas_call(
        _final_norm_kernel,
        grid=(n_rows // tm,),
        in_specs=[pl.BlockSpec((tm, d), lambda i: (blk0 + i, 0)),
                  pl.BlockSpec((1, d), lambda i: (0, 0))],
        out_specs=pl.BlockSpec((tm, d), lambda i: (i, 0)),
        out_shape=jax.ShapeDtypeStruct((n_rows, d), F32),
        compiler_params=_params(1),
        name="final_norm",
    )(x, g)


TM = 512
TM_BIG = 1024
TM_COMBINE = 256


def _tile_rows(groups, tm):
    rows = []
    base = 0
    for n_rows, seq in groups:
        for b in range(n_rows):
            rows += [base + b] * (seq // tm)
        base += n_rows
    return jnp.asarray(np.asarray(rows, np.int32))


def kernel(x_prompt, x_sample, c_prompt, c_sample, w_ada, b_ada, g_norm_mix, g_norm_ff, w_in, w_fmix,
           g_out_f, g_out_a, w_out, w_ff1, w_ff3, w_ff2, w_router, w_e1, w_e3, w_e2, g_final):
    depth, d, d_in = w_in.shape
    groups = [(x_prompt.shape[0], x_prompt.shape[1]), (x_sample.shape[0], x_sample.shape[1])]
    group_row0 = [0, groups[0][0] * groups[0][1]]
    t = sum(b * s for b, s in groups)
    n_req = sum(b for b, _ in groups)
    d_attn = d - D_FOURIER
    n_heads = d_attn // HEAD_DIM
    n_experts = w_router.shape[-1]
    d_ff_e = w_e1.shape[-1]
    dilations = [dil for _, dil in DILATED_BRANCHES]

    x = jnp.concatenate([x_prompt.reshape(-1, d), x_sample.reshape(-1, d)], axis=0)
    c = jnp.concatenate([c_prompt, c_sample], axis=0)
    c = jnp.pad(c, ((0, -n_req % 8), (0, 0)))
    mod = ada_modulation(c, w_ada, b_ada)
    rows = _tile_rows(groups, TM)
    rows_c = _tile_rows(groups, TM_COMBINE)
    def dense_tiling(tm):
        tiles = t // tm
        return Tiling(jnp.zeros((tiles,), jnp.int32), _tile_rows(groups, tm),
                      jnp.full((1,), tiles, jnp.int32), jnp.full((tiles,), tm, jnp.int32))

    dense, dense_big = dense_tiling(TM), dense_tiling(TM_BIG)
    slopes = jnp.exp2(-8.0 * (jnp.arange(n_heads, dtype=F32) + 1.0) / n_heads)
    ab = fourier_ab(w_fmix)
    dft = [dft_matrix(seq) for _, seq in groups]

    for l in range(depth):
        mod4 = mod[l].reshape(mod.shape[1], 6, 1, d)
        layer = lambda e, l=l: l
        h = norm_modulate(x, rows, g_norm_mix[l:l + 1], mod4, 0, tm=TM)
        u = matmul(h, w_in, dense_big, w_index=layer, tm=TM_BIG, tn=D_FOURIER, out_dtype=BF16,
                   n=D_FOURIER)
        qkvs = matmul_qkv(h, w_in, dense_big, w_index=layer, tm=TM_BIG, tn=512, col0=D_FOURIER,
                          n=3 * d_attn, dilations=dilations, q_cols=d_attn,
                          q_scale=HEAD_DIM ** -0.5 * LOG2_E)

        group_inputs = []
        for gi, ((n_rows, seq), row0) in enumerate(zip(groups, group_row0)):
            pq = fourier_channel_stage(u, ab, l, n_rows=n_rows, seq=seq, row_block0=row0 // TM, tm=TM)
            f = fourier_sequence_stage(dft[gi], pq, n_rows=n_rows, seq=seq, tm=TM)
            branches = [attention_branch(qkv, slopes, window=window, dilation=dilation, n_rows=n_rows,
                                         seq=seq, row0=row0, n_heads=n_heads)
                        for qkv, (window, dilation) in zip(qkvs, DILATED_BRANCHES)]
            group_inputs.append((f, [b[0] for b in branches], [b[1] for b in branches]))
        y = merge_and_norm(group_inputs, g_out_f[l:l + 1], g_out_a[l:l + 1], n_heads=n_heads)
        x = matmul_gated_residual(y, w_out, x, mod4, 2, dense_big, w_index=layer, tm=TM_BIG, tn=1024)

        j = l // 2
        if l % 2 == 0:
            h = norm_modulate(x, rows, g_norm_ff[l:l + 1], mod4, 3, tm=TM)
            act = matmul_swiglu(h, w_ff1, w_ff3, dense_big, w_index=lambda e, j=j: j, tm=TM_BIG, tn=512)
            x = matmul_gated_residual(act, w_ff2, x, mod4, 5, dense, w_index=lambda e, j=j: j,
                                      tm=TM, tn=512)
        else:
            h, route = norm_modulate(x, rows, g_norm_ff[l:l + 1], mod4, 3, out_dtype=F32,
                                     w_router=w_router[j], tm=TM)
            n_sorted, pos, tile_expert, used, ends, valid = moe_dispatch_plan(route, n_experts, TM)
            routed = Tiling(tile_expert, tile_expert, used, valid)

            def choice_major(tm):
                tiles = pos.reshape(t // tm, tm, TOP_K).transpose(0, 2, 1)
                return tiles.reshape(t // tm, 1, TOP_K * tm)

            xs = rows_to_matrix(dispatch_rows(h, choice_major(TM), ends, used, n_rows=n_sorted, tm=TM),
                                tm=TM)
            expert_w = lambda e, j=j: j * n_experts + e
            act = matmul_swiglu(xs, w_e1.reshape(-1, d, d_ff_e), w_e3.reshape(-1, d, d_ff_e), routed,
                                w_index=expert_w, tm=TM, tn=1024)
            o_sorted = matmul(act, w_e2.reshape(-1, d_ff_e, d), routed, w_index=expert_w,
                              tm=TM, tn=512, out_dtype=F32, vmem=VMEM_LIMIT_BIG)
            x = moe_combine(o_sorted, choice_major(TM_COMBINE), route, x, mod4, 5, rows_c, tm=TM_COMBINE)

    g = g_final.reshape(1, d)
    outs = []
    for (n_rows, seq), row0 in zip(groups, group_row0):
        outs.append(final_norm(x, g, row0=row0, n_rows=n_rows * seq, tm=TM).reshape(n_rows, seq, d))
    return tuple(outs)
```

```python
import functools
import math

import numpy as np
import jax
import jax.numpy as jnp
from jax import lax
from jax.experimental import pallas as pl
from jax.experimental.pallas import tpu as pltpu

HEAD_DIM = 128
FOURIER_GROUP = 128
N_FOURIER_GROUPS = 4
D_FOURIER = FOURIER_GROUP * N_FOURIER_GROUPS
DILATED_BRANCHES = ((128, 1), (512, 4), (2048, 16))
TOP_K = 2
EPS = 1e-6
NEG_INF = -1e30
LOG2_E = math.log2(math.e)
LANES = 128
VMEM_LIMIT = 56 * 1024 * 1024
VMEM_LIMIT_BIG = 62 * 1024 * 1024
ATTN_BLOCK_BYTES = 4 * 1024 * 1024

BF16 = jnp.bfloat16
F32 = jnp.float32


def _params(n_axes, vmem=VMEM_LIMIT):
    return pltpu.CompilerParams(
        dimension_semantics=("arbitrary",) * n_axes, vmem_limit_bytes=vmem)


def _ada_kernel(c_ref, w_ref, b_ref, o_ref):
    c = c_ref[...]
    a = (c * jax.nn.sigmoid(c)).astype(BF16)
    acc = jnp.dot(a, w_ref[...].astype(BF16), preferred_element_type=F32)
    o_ref[...] = acc + b_ref[...]


def ada_modulation(c, w_ada, b_ada, tn=1024):
    n_layers, d, n = w_ada.shape
    r8 = c.shape[0]
    return pl.pallas_call(
        _ada_kernel,
        grid=(n_layers, n // tn),
        in_specs=[
            pl.BlockSpec((r8, d), lambda l, j: (0, 0)),
            pl.BlockSpec((None, d, tn), lambda l, j: (l, 0, j)),
            pl.BlockSpec((None, 1, tn), lambda l, j: (l, 0, j)),
        ],
        out_specs=pl.BlockSpec((None, r8, tn), lambda l, j: (l, 0, j)),
        out_shape=jax.ShapeDtypeStruct((n_layers, r8, n), F32),
        compiler_params=_params(2),
        name="ada_modulation",
    )(c, w_ada, b_ada.reshape(n_layers, 1, n))


def _norm_mod(x, g, sc, sh):
    y = x * lax.rsqrt(jnp.mean(x * x, axis=-1, keepdims=True) + EPS)
    return (y * g) * (1.0 + sc) + sh


def _norm_mod_kernel(rows_ref, x_ref, g_ref, sc_ref, sh_ref, h_ref):
    del rows_ref
    h_ref[...] = _norm_mod(x_ref[...], g_ref[...], sc_ref[...], sh_ref[...]).astype(h_ref.dtype)


def _norm_mod_router_kernel(rows_ref, x_ref, g_ref, sc_ref, sh_ref, wr_ref, h_ref, route_ref,
                            totals_ref, stage_ref, counts_ref, earlier_ref, *, n_experts):
    del rows_ref
    h = _norm_mod(x_ref[...], g_ref[...], sc_ref[...], sh_ref[...])
    tm = h.shape[0]
    per = h_ref.shape[0] // SLAB_STRIDE
    for a in range(SLAB_STRIDE):
        for b in range(per // tm):
            c = SLAB_STRIDE * b + a
            stage_ref[a, pl.ds(b, tm, stride=per // tm), :] = h[:, c * LANES:(c + 1) * LANES]
    for a in range(SLAB_STRIDE):
        h_ref[pl.ds(a, per, stride=SLAB_STRIDE), :] = stage_ref[a]
    w = wr_ref[...]
    h_hi, w_hi = h.astype(BF16), w.astype(BF16)
    h_lo = (h - h_hi.astype(F32)).astype(BF16)
    w_lo = (w - w_hi.astype(F32)).astype(BF16)
    logits = (jnp.dot(h_hi, w_hi, preferred_element_type=F32)
              + (jnp.dot(h_hi, w_lo, preferred_element_type=F32)
                 + jnp.dot(h_lo, w_hi, preferred_element_type=F32)))
    lane = lax.broadcasted_iota(jnp.int32, logits.shape, 1)
    logits = jnp.where(lane < n_experts, logits, -jnp.inf)
    m1 = jnp.max(logits, axis=-1, keepdims=True)
    i1 = jnp.min(jnp.where(logits == m1, lane, LANES), axis=-1, keepdims=True)
    rest = jnp.where(lane == i1, -jnp.inf, logits)
    m2 = jnp.max(rest, axis=-1, keepdims=True)
    i2 = jnp.min(jnp.where(rest == m2, lane, LANES), axis=-1, keepdims=True)
    e2 = jnp.exp(m2 - m1)
    den = 1.0 + e2
    @pl.when(pl.program_id(0) == 0)
    def _():
        counts_ref[...] = jnp.zeros_like(counts_ref)
        earlier_ref[...] = (lax.broadcasted_iota(jnp.int32, earlier_ref.shape, 1)
                            < lax.broadcasted_iota(jnp.int32, earlier_ref.shape, 0)).astype(BF16)

    pick1, pick2 = lane == i1, lane == i2
    picked = jnp.logical_or(pick1, pick2)
    before = counts_ref[0:1, :] + jnp.dot(earlier_ref[...], picked.astype(BF16),
                                          preferred_element_type=F32)
    rank1 = jnp.sum(jnp.where(pick1, before, 0.0), axis=-1, keepdims=True)
    rank2 = jnp.sum(jnp.where(pick2, before, 0.0), axis=-1, keepdims=True)
    counts_ref[...] = counts_ref[...] + jnp.sum(picked.astype(F32), axis=0, keepdims=True)
    totals_ref[...] = counts_ref[...]

    route = jnp.where(lane == 0, i1.astype(F32), 0.0)
    route = jnp.where(lane == 1, i2.astype(F32), route)
    route = jnp.where(lane == 2, 1.0 / den, route)
    route = jnp.where(lane == 3, e2 / den, route)
    route = jnp.where(lane == 4, rank1, route)
    route = jnp.where(lane == 5, rank2, route)
    route_ref[...] = route


def norm_modulate(x, rows, g, mod4, layer_part, *, out_dtype=BF16, w_router=None, tm=512):
    t, d = x.shape
    sh_part, sc_part = layer_part, layer_part + 1
    in_specs = [
        pl.BlockSpec((tm, d), lambda i, rows: (i, 0)),
        pl.BlockSpec((1, d), lambda i, rows: (0, 0)),
        pl.BlockSpec((None, None, 1, d), lambda i, rows: (rows[i], sc_part, 0, 0)),
        pl.BlockSpec((None, None, 1, d), lambda i, rows: (rows[i], sh_part, 0, 0)),
    ]
    h_spec = pl.BlockSpec((tm, d), lambda i, rows: (i, 0))
    h_shape = jax.ShapeDtypeStruct((t, d), out_dtype)
    if w_router is None:
        return pl.pallas_call(
            _norm_mod_kernel,
            grid_spec=pltpu.PrefetchScalarGridSpec(
                num_scalar_prefetch=1, grid=(t // tm,), in_specs=in_specs, out_specs=h_spec),
            out_shape=h_shape,
            compiler_params=_params(1),
            name="norm_modulate",
        )(rows, x, g, mod4, mod4)
    n_experts = w_router.shape[1]
    wr = jnp.pad(w_router, ((0, 0), (0, LANES - n_experts)))
    in_specs.append(pl.BlockSpec((d, LANES), lambda i, rows: (0, 0)))
    chunks = d // LANES
    assert chunks % SLAB_STRIDE == 0
    h_flat, route, totals = pl.pallas_call(
        functools.partial(_norm_mod_router_kernel, n_experts=n_experts),
        grid_spec=pltpu.PrefetchScalarGridSpec(
            num_scalar_prefetch=1, grid=(t // tm,), in_specs=in_specs,
            out_specs=[pl.BlockSpec((tm * chunks, LANES), lambda i, rows: (i, 0)),
                       pl.BlockSpec((tm, LANES), lambda i, rows: (i, 0)),
                       pl.BlockSpec((8, LANES), lambda i, rows: (0, 0))],
            scratch_shapes=[pltpu.VMEM((SLAB_STRIDE, tm * chunks // SLAB_STRIDE, LANES), F32),
                            pltpu.VMEM((8, LANES), F32), pltpu.VMEM((tm, tm), BF16)]),
        out_shape=[jax.ShapeDtypeStruct((t * chunks, LANES), F32),
                   jax.ShapeDtypeStruct((t, LANES), F32),
                   jax.ShapeDtypeStruct((8, LANES), F32)],
        compiler_params=_params(1),
        name="norm_modulate_router",
    )(rows, x, g, mod4, mod4, wr)
    return h_flat.reshape(t, chunks, LANES), route, totals[0, :n_experts].astype(jnp.int32)


CAST_ROWS = 512


def _cast_weights_if_changed(te_ref, w_refs, wb_refs):
    i = pl.program_id(1)
    changed = jnp.logical_or(i == 0, te_ref[i] != te_ref[jnp.maximum(i - 1, 0)])

    @pl.when(changed)
    def _():
        k = w_refs[0].shape[0]
        step = math.gcd(k, CAST_ROWS)

        def body(c, carry):
            r0 = pl.multiple_of(c * step, step)
            for w_ref, wb_ref in zip(w_refs, wb_refs):
                wb_ref[pl.ds(r0, step), :] = w_ref[pl.ds(r0, step), :].astype(BF16)
            return carry

        lax.fori_loop(0, k // step, body, 0)


ROW_PARTS = 4


def _for_valid_rows(valid_ref, o_ref, compute):
    tm = o_ref.shape[0]
    quantum = tm // ROW_PARTS
    quanta = (valid_ref[pl.program_id(1)] + quantum - 1) // quantum
    for n in range(ROW_PARTS + 1):
        rows = n * quantum

        @pl.when(quanta == n)
        def _(rows=rows):
            if rows:
                o_ref[:rows, :] = compute(rows).astype(o_ref.dtype)
            if rows < tm:
                o_ref[rows:, :] = jnp.zeros((tm - rows, o_ref.shape[1]), o_ref.dtype)


def _mm_plain_kernel(te_ref, rows_ref, used_ref, valid_ref, a_ref, w_ref, o_ref, wb_ref):
    del rows_ref, used_ref
    _cast_weights_if_changed(te_ref, (w_ref,), (wb_ref,))
    _for_valid_rows(valid_ref, o_ref, lambda rows: jnp.dot(
        a_ref[:rows, :], wb_ref[...], preferred_element_type=F32))


def _mm_qkv_kernel(te_ref, rows_ref, used_ref, valid_ref, a_ref, w_ref, *rest, dilations, q_chunks,
                   q_scale):
    del rows_ref, used_ref, valid_ref
    out_refs = rest[:len(dilations)]
    wb_ref, acc_ref, stage_ref = rest[len(dilations):]
    _cast_weights_if_changed(te_ref, (w_ref,), (wb_ref,))
    acc = jnp.dot(a_ref[...], wb_ref[...], preferred_element_type=F32)
    acc = acc * jnp.where(pl.program_id(0) < q_chunks, q_scale, 1.0).astype(F32)
    tm, tn = acc.shape
    slabs = tn // LANES
    for c in range(slabs):
        acc_ref[c] = acc[:, c * LANES:(c + 1) * LANES]
    cur_ref, cur_d = acc_ref, 1
    for idx, (o_ref, d) in enumerate(zip(out_refs, dilations)):
        if d == 1:
            o_ref[0] = acc.astype(o_ref.dtype)
            continue
        f = d // cur_d
        keep = idx + 1 < len(dilations)
        for rp in range(cur_d):
            for b in range(f):
                r = b * cur_d + rp
                for c in range(slabs):
                    piece = cur_ref[c, pl.ds(rp * (tm // cur_d) + b, tm // d, stride=f), :]
                    o_ref[r, :, c * LANES:(c + 1) * LANES] = piece.astype(o_ref.dtype)
                    if keep:
                        stage_ref[c, pl.ds(r * (tm // d), tm // d), :] = piece
        cur_ref, cur_d = stage_ref, d


def _mm_swiglu_kernel(te_ref, rows_ref, used_ref, valid_ref, a_ref, w1_ref, w3_ref, o_ref, wb1_ref,
                      wb3_ref):
    del rows_ref, used_ref
    _cast_weights_if_changed(te_ref, (w1_ref, w3_ref), (wb1_ref, wb3_ref))

    def compute(rows):
        a = a_ref[:rows, :]
        g = jnp.dot(a, wb1_ref[...], preferred_element_type=F32)
        u = jnp.dot(a, wb3_ref[...], preferred_element_type=F32)
        return (g * jax.nn.sigmoid(g)) * u

    _for_valid_rows(valid_ref, o_ref, compute)


def _mm_resid_kernel(te_ref, rows_ref, used_ref, valid_ref, a_ref, w_ref, x_ref, gate_ref, o_ref,
                     wb_ref):
    del rows_ref, used_ref, valid_ref
    _cast_weights_if_changed(te_ref, (w_ref,), (wb_ref,))
    acc = jnp.dot(a_ref[...], wb_ref[...], preferred_element_type=F32)
    o_ref[...] = x_ref[...] + gate_ref[...] * acc


def _a_spec(tm, k):
    return pl.BlockSpec((tm, k), lambda j, i, te, rows, used, valid: (jnp.minimum(i, used[0] - 1), 0))


def _w_spec(k, tn, w_index, col_block0=0, single_buffer=False):
    mode = dict(pipeline_mode=pl.Buffered(1)) if single_buffer else {}
    return pl.BlockSpec((None, k, tn),
                        lambda j, i, te, rows, used, valid: (w_index(te[i]), 0, col_block0 + j), **mode)


def _out_spec(tm, tn):
    return pl.BlockSpec((tm, tn), lambda j, i, te, rows, used, valid: (i, j))


class Tiling:
    def __init__(self, tile_expert, rows, used, valid):
        self.args = (tile_expert, rows, used, valid)


def matmul_qkv(a, w, tiling, *, w_index, tm, tn, col0, n, dilations, q_cols, q_scale):
    m, k = a.shape
    assert sum(d > 1 for d in dilations) <= 2
    return pl.pallas_call(
        functools.partial(_mm_qkv_kernel, dilations=tuple(dilations), q_chunks=q_cols // tn,
                          q_scale=q_scale),
        grid_spec=pltpu.PrefetchScalarGridSpec(
            num_scalar_prefetch=4, grid=(n // tn, m // tm),
            in_specs=[_a_spec(tm, k), _w_spec(k, tn, w_index, col0 // tn)],
            out_specs=[pl.BlockSpec((d, tm // d, tn), lambda j, i, te, rows, used, valid: (0, i, j))
                       for d in dilations],
            scratch_shapes=[pltpu.VMEM((k, tn), BF16)]
                           + [pltpu.VMEM((tn // LANES, tm, LANES), F32)] * 2),
        out_shape=[jax.ShapeDtypeStruct((d, m // d, n), BF16) for d in dilations],
        compiler_params=_params(2),
        name="matmul_qkv",
    )(*tiling.args, a, w)


def matmul(a, w, tiling, *, w_index, tm, tn, out_dtype, n=None, vmem=VMEM_LIMIT):
    m, k = a.shape
    n = w.shape[-1] if n is None else n
    return pl.pallas_call(
        _mm_plain_kernel,
        grid_spec=pltpu.PrefetchScalarGridSpec(
            num_scalar_prefetch=4, grid=(n // tn, m // tm),
            in_specs=[_a_spec(tm, k), _w_spec(k, tn, w_index)],
            out_specs=_out_spec(tm, tn),
            scratch_shapes=[pltpu.VMEM((k, tn), BF16)]),
        out_shape=jax.ShapeDtypeStruct((m, n), out_dtype),
        compiler_params=_params(2, vmem),
        name="matmul",
    )(*tiling.args, a, w)


def matmul_swiglu(a, w1, w3, tiling, *, w_index, tm, tn):
    m, k = a.shape
    n = w1.shape[-1]
    return pl.pallas_call(
        _mm_swiglu_kernel,
        grid_spec=pltpu.PrefetchScalarGridSpec(
            num_scalar_prefetch=4, grid=(n // tn, m // tm),
            in_specs=[_a_spec(tm, k), _w_spec(k, tn, w_index), _w_spec(k, tn, w_index)],
            out_specs=_out_spec(tm, tn),
            scratch_shapes=[pltpu.VMEM((k, tn), BF16), pltpu.VMEM((k, tn), BF16)]),
        out_shape=jax.ShapeDtypeStruct((m, n), BF16),
        compiler_params=_params(2),
        name="matmul_swiglu",
    )(*tiling.args, a, w1, w3)


def matmul_gated_residual(a, w, x, mod4, gate_part, tiling, *, w_index, tm, tn):
    m, k = a.shape
    n = w.shape[-1]
    return pl.pallas_call(
        _mm_resid_kernel,
        grid_spec=pltpu.PrefetchScalarGridSpec(
            num_scalar_prefetch=4, grid=(n // tn, m // tm),
            in_specs=[_a_spec(tm, k), _w_spec(k, tn, w_index), _out_spec(tm, tn),
                      pl.BlockSpec((None, None, 1, tn),
                                   lambda j, i, te, rows, used, valid: (rows[i], gate_part, 0, j))],
            out_specs=_out_spec(tm, tn),
            scratch_shapes=[pltpu.VMEM((k, tn), BF16)]),
        out_shape=jax.ShapeDtypeStruct((m, n), F32),
        compiler_params=_params(2),
        name="matmul_gated_residual",
    )(*tiling.args, a, w, x, mod4)


def _dft_cos_sin(n):
    j = lax.broadcasted_iota(jnp.int32, (n, n), 0)
    k = lax.broadcasted_iota(jnp.int32, (n, n), 1)
    ang = ((j * k) % n).astype(F32) * (2.0 * math.pi / n)
    return jnp.cos(ang), jnp.sin(ang)


def _fourier_ab_kernel(c_ref, s_ref, w_ref, o_ref):
    w = w_ref[...]
    scale = FOURIER_GROUP ** -0.5
    a = jnp.dot(c_ref[...], w, preferred_element_type=F32, precision=lax.Precision.HIGHEST)
    b = jnp.dot(s_ref[...], w, preferred_element_type=F32, precision=lax.Precision.HIGHEST)
    o_ref[:, :FOURIER_GROUP] = (a * scale).astype(o_ref.dtype)
    o_ref[:, FOURIER_GROUP:] = (b * scale).astype(o_ref.dtype)


def fourier_ab(w_fmix):
    n_layers, n_groups, c, _ = w_fmix.shape
    cos_g, sin_g = _dft_cos_sin(c)
    return pl.pallas_call(
        _fourier_ab_kernel,
        grid=(n_layers, n_groups),
        in_specs=[pl.BlockSpec((c, c), lambda l, g: (0, 0)),
                  pl.BlockSpec((c, c), lambda l, g: (0, 0)),
                  pl.BlockSpec((None, None, c, c), lambda l, g: (l, g, 0, 0))],
        out_specs=pl.BlockSpec((None, None, c, 2 * c), lambda l, g: (l, g, 0, 0)),
        out_shape=jax.ShapeDtypeStruct((n_layers, n_groups, c, 2 * c), BF16),
        compiler_params=_params(2),
        name="fourier_ab",
    )(cos_g, sin_g, w_fmix)


def _fourier_channel_kernel(u_ref, ab_ref, pq_ref):
    c = FOURIER_GROUP
    for g in range(N_FOURIER_GROUPS):
        pq = jnp.dot(u_ref[:, g * c:(g + 1) * c], ab_ref[g], preferred_element_type=F32)
        pq_ref[0, :, g * c:(g + 1) * c] = pq[:, :c].astype(pq_ref.dtype)
        pq_ref[1, :, g * c:(g + 1) * c] = pq[:, c:].astype(pq_ref.dtype)


def fourier_channel_stage(u, ab, layer, *, n_rows, seq, row_block0, tm=512):
    tiles = seq // tm
    return pl.pallas_call(
        _fourier_channel_kernel,
        grid=(n_rows, tiles),
        in_specs=[pl.BlockSpec((tm, D_FOURIER), lambda b, i: (row_block0 + b * tiles + i, 0)),
                  pl.BlockSpec((None, N_FOURIER_GROUPS, FOURIER_GROUP, 2 * FOURIER_GROUP),
                               lambda b, i: (layer, 0, 0, 0))],
        out_specs=pl.BlockSpec((None, 2, tm, D_FOURIER), lambda b, i: (b, 0, i, 0)),
        out_shape=jax.ShapeDtypeStruct((n_rows, 2, seq, D_FOURIER), BF16),
        compiler_params=_params(2),
        name="fourier_channel_stage",
    )(u, ab)


def _fourier_seq_kernel(cs_ref, pq_ref, o_ref, *, scale):
    o_ref[...] = jnp.dot(cs_ref[...], pq_ref[...], preferred_element_type=F32) * scale


def fourier_sequence_stage(cs, pq, *, n_rows, seq, tm=512):
    tiles = seq // tm
    return pl.pallas_call(
        functools.partial(_fourier_seq_kernel, scale=seq ** -0.5),
        grid=(n_rows, tiles),
        in_specs=[pl.BlockSpec((tm, 2 * seq), lambda b, i: (i, 0)),
                  pl.BlockSpec((None, 2 * seq, D_FOURIER), lambda b, i: (b, 0, 0))],
        out_specs=pl.BlockSpec((tm, D_FOURIER), lambda b, i: (b * tiles + i, 0)),
        out_shape=jax.ShapeDtypeStruct((n_rows * seq, D_FOURIER), F32),
        compiler_params=_params(2),
        name="fourier_sequence_stage",
    )(cs, pq.reshape(n_rows, 2 * seq, D_FOURIER))


DFT_SPLIT = 64


def _dft_kernel(t1_ref, t2_ref, o_ref):
    seq = t2_ref.shape[2]
    c1, s1 = t1_ref[0], t1_ref[1]
    c2, s2 = t2_ref[0], t2_ref[1]
    o_ref[:, :seq] = (c1 * c2 - s1 * s2).astype(o_ref.dtype)
    o_ref[:, seq:] = (-(s1 * c2 + c1 * s2)).astype(o_ref.dtype)


def dft_matrix(seq):
    k = jnp.arange(seq, dtype=jnp.int32)

    def table(mult):
        ang = ((mult[:, None] * k[None, :]) % seq).astype(F32) * (2.0 * math.pi / seq)
        return jnp.stack([jnp.cos(ang), jnp.sin(ang)])

    n_coarse = seq // DFT_SPLIT
    t1 = table(jnp.arange(n_coarse, dtype=jnp.int32) * DFT_SPLIT).reshape(2, n_coarse, 1, seq)
    t2 = table(jnp.arange(DFT_SPLIT, dtype=jnp.int32))
    return pl.pallas_call(
        _dft_kernel,
        grid=(n_coarse,),
        in_specs=[pl.BlockSpec((2, None, 1, seq), lambda a: (0, a, 0, 0)),
                  pl.BlockSpec((2, DFT_SPLIT, seq), lambda a: (0, 0, 0))],
        out_specs=pl.BlockSpec((DFT_SPLIT, 2 * seq), lambda a: (a, 0)),
        out_shape=jax.ShapeDtypeStruct((seq, 2 * seq), BF16),
        compiler_params=_params(1),
        name="dft_matrix",
    )(t1, t2)


def _attn_kernel(slopes_ref, q_ref, k_ref, v_ref, o_ref, lse_ref, bias_ref, *, dilation, n_side,
                 bq, bk, hps, stack, unroll):
    n_res, sub = q_ref.shape[:2]
    n_cases = bias_ref.shape[0]
    blocks = sub // bq if stack == 1 else 1
    lane = lax.broadcasted_iota(jnp.int32, (bq, LANES), 1)

    @pl.when(jnp.logical_and(pl.program_id(1) == 0, pl.program_id(2) == 0))
    def _():
        row = lax.broadcasted_iota(jnp.int32, (bq, bk), 0)
        col = lax.broadcasted_iota(jnp.int32, (bq, bk), 1)
        for case in range(n_cases):
            dist = jnp.abs(col - row - case * n_side)
            valid = dist <= n_side
            if stack > 1:
                valid = jnp.logical_and(valid, row // sub == col // sub)
            dist_f = dist.astype(F32) * float(dilation)
            for hh in range(hps):
                slope = slopes_ref[pl.program_id(0) * hps + hh] * LOG2_E
                bias_ref[case, hh] = jnp.where(valid, -slope * dist_f, NEG_INF)

    def body(it, carry):
        if stack == 1:
            res = it // blocks
            q0 = pl.multiple_of((it % blocks) * bq, bq)
            ks = pl.multiple_of(jnp.clip(q0 - n_side, 0, sub - bk), n_side)
            case = (q0 - ks) // n_side
            q_at = lambda ref, cols: ref[res, pl.ds(q0, bq), cols]
            k_at = lambda ref, cols: ref[res, pl.ds(ks, bk), cols]
        else:
            case = 0
            res = pl.ds(pl.multiple_of(it * stack, stack), stack)
            q_at = k_at = lambda ref, cols: ref[res, :, cols].reshape(stack * sub, HEAD_DIM)
        lse_tile = jnp.zeros((bq, LANES), F32)
        for hh in range(hps):
            cols = slice(hh * HEAD_DIM, (hh + 1) * HEAD_DIM)
            q = q_at(q_ref, cols)
            k = k_at(k_ref, cols)
            v = k_at(v_ref, cols)
            s = lax.dot_general(q, k, (((1,), (1,)), ((), ())), preferred_element_type=F32)
            s = s + bias_ref[case, hh]
            m = jnp.max(s, axis=-1, keepdims=True)
            p = jnp.exp2(s - m)
            den = jnp.sum(p, axis=-1, keepdims=True)
            num = jnp.dot(p.astype(BF16), v, preferred_element_type=F32)
            o = (num / den).astype(o_ref.dtype)
            if stack == 1:
                o_ref[res, pl.ds(q0, bq), cols] = o
            else:
                o_ref[res, :, cols] = o.reshape(stack, sub, HEAD_DIM)
            lse_tile = jnp.where(lane == pl.program_id(0) * hps + hh, m + jnp.log2(den), lse_tile)
        if stack == 1:
            lse_ref[res, pl.ds(q0, bq), :] = lse_tile
        else:
            lse_ref[res] = lse_tile.reshape(stack, sub, LANES)
        return carry

    lax.fori_loop(0, n_res * blocks // stack, body, 0, unroll=unroll)


ATTN_CHAINS = 12
ATTN_KEY_BLOCK = 256


def attention_step_shape(sub, n_heads, dilation, rows_per_trip):
    def largest(n, unit):
        return max(c for c in range(1, n + 1) if n % c == 0 and c * unit <= ATTN_BLOCK_BYTES)

    hps = largest(n_heads, sub * HEAD_DIM * 2)
    n_res = largest(dilation, sub * hps * HEAD_DIM * 2)
    trips = n_res * sub // rows_per_trip
    unroll = max(u for u in range(1, trips + 1) if trips % u == 0 and u * hps <= max(ATTN_CHAINS, hps))
    return hps, n_res, unroll


def attention_branch(qkv, slopes, *, window, dilation, n_rows, seq, row0, n_heads):
    tg = n_rows * seq
    sub = seq // dilation
    n_side = (window // 2) // dilation
    if sub >= ATTN_KEY_BLOCK:
        stack, bq, bk, n_cases = 1, ATTN_KEY_BLOCK - 2 * n_side, ATTN_KEY_BLOCK, 3
    else:
        stack = ATTN_KEY_BLOCK // sub
        bq, bk, n_cases = ATTN_KEY_BLOCK, ATTN_KEY_BLOCK, 1
    hps, n_res, unroll = attention_step_shape(sub, n_heads, dilation, bq)
    assert n_res % stack == 0 and sub % (bq // stack) == 0
    gw = hps * HEAD_DIM
    groups = n_heads // hps
    d_attn = n_heads * HEAD_DIM
    rb0 = row0 // seq

    def in_map(part):
        return lambda g, b, r, slopes: (r, rb0 + b, part * groups + g)

    out_map = lambda g, b, r, slopes: (r, b, g)
    o, lse = pl.pallas_call(
        functools.partial(_attn_kernel, dilation=dilation, n_side=n_side, bq=bq, bk=bk, hps=hps,
                          stack=stack, unroll=unroll),
        grid_spec=pltpu.PrefetchScalarGridSpec(
            num_scalar_prefetch=1, grid=(groups, n_rows, dilation // n_res),
            in_specs=[pl.BlockSpec((n_res, sub, gw), in_map(part)) for part in range(3)],
            out_specs=[pl.BlockSpec((n_res, sub, gw), out_map),
                       pl.BlockSpec((n_res, sub, LANES), out_map)],
            scratch_shapes=[pltpu.VMEM((n_cases, hps, bq, bk), F32)]),
        out_shape=[jax.ShapeDtypeStruct((dilation, tg // dilation, d_attn), BF16),
                   jax.ShapeDtypeStruct((dilation, tg // dilation, groups * LANES), F32)],
        compiler_params=_params(3),
        name=f"attention_d{dilation}",
    )(slopes, qkv, qkv, qkv)
    return o, lse


def _rms(x, g):
    return x * lax.rsqrt(jnp.mean(x * x, axis=-1, keepdims=True) + EPS) * g


def _merge_tile(f_ref, o_refs, l_refs, gf_ref, ga_ref, y_ref, ya_ref, on_ref, ln_ref, tmp_ref, n_heads):
    tm = y_ref.shape[0]
    lses = []
    for bi, (o_ref, l_ref) in enumerate(zip(o_refs, l_refs)):
        d = o_ref.shape[0]
        blocks = l_ref.shape[2] // LANES
        if d == 1:
            parts = [l_ref[0, :, g * LANES:(g + 1) * LANES] for g in range(blocks)]
        else:
            def to_positions(src, dst, tmp):
                if d <= SLAB_STRIDE:
                    for r in range(d):
                        dst[pl.ds(r, tm // d, stride=d), :] = src(r)
                    return
                f = d // SLAB_STRIDE
                for a in range(SLAB_STRIDE):
                    for b in range(f):
                        tmp[a, pl.ds(b, tm // d, stride=f), :] = src(SLAB_STRIDE * b + a)
                    dst[pl.ds(a, tm // SLAB_STRIDE, stride=SLAB_STRIDE), :] = tmp[a]

            for h in range(n_heads):
                cols = slice(h * HEAD_DIM, (h + 1) * HEAD_DIM)
                to_positions(lambda r: o_ref[r, :, cols].astype(F32), on_ref.at[bi, h], tmp_ref.at[h])
            for g in range(blocks):
                cols = slice(g * LANES, (g + 1) * LANES)
                to_positions(lambda r: l_ref[r, :, cols], ln_ref.at[bi, g], tmp_ref.at[n_heads + g])
            parts = [ln_ref[bi, g] for g in range(blocks)]
        lses.append(functools.reduce(lambda a, b: a + b, parts))

    top = functools.reduce(jnp.maximum, lses)
    ws = [jnp.exp2(l - top) for l in lses]
    inv = 1.0 / functools.reduce(lambda a, b: a + b, ws)
    ws = [w * inv for w in ws]
    for h in range(n_heads):
        cols = slice(h * HEAD_DIM, (h + 1) * HEAD_DIM)
        acc = None
        for bi, o_ref in enumerate(o_refs):
            o = o_ref[0, :, cols].astype(F32) if o_ref.shape[0] == 1 else on_ref[bi, h]
            term = o * ws[bi][:, h:h + 1]
            acc = term if acc is None else acc + term
        ya_ref[:, cols] = acc
    y_ref[:, :D_FOURIER] = _rms(f_ref[...], gf_ref[...]).astype(y_ref.dtype)
    y_ref[:, D_FOURIER:] = _rms(ya_ref[...], ga_ref[...]).astype(y_ref.dtype)


def _merge_kernel(*refs, n_heads, group_tiles):
    n_br = len(DILATED_BRANCHES)
    per_group = 1 + 2 * n_br
    n_groups = len(group_tiles)
    gf_ref, ga_ref, y_ref, ya_ref, on_ref, ln_ref, tmp_ref = refs[n_groups * per_group:]
    i = pl.program_id(0)
    tile0 = 0
    for gi, tiles in enumerate(group_tiles):
        grp = refs[gi * per_group:(gi + 1) * per_group]

        @pl.when(jnp.logical_and(i >= tile0, i < tile0 + tiles))
        def _(grp=grp):
            _merge_tile(grp[0], grp[1:1 + n_br], grp[1 + n_br:], gf_ref, ga_ref, y_ref, ya_ref,
                        on_ref, ln_ref, tmp_ref, n_heads)

        tile0 += tiles


def merge_and_norm(group_inputs, g_out_f, g_out_a, *, n_heads, tm=256):
    d_attn = n_heads * HEAD_DIM
    group_tiles = [g[0].shape[0] // tm for g in group_inputs]
    n_tiles = sum(group_tiles)
    n_br = len(DILATED_BRANCHES)
    in_specs, args = [], []
    tile0 = 0
    for (f, os_, ls_), tiles in zip(group_inputs, group_tiles):
        local = lambda i, tile0=tile0, tiles=tiles: jnp.clip(i - tile0, 0, tiles - 1)
        in_specs.append(pl.BlockSpec((tm, D_FOURIER), lambda i, local=local: (local(i), 0)))
        for arr in (*os_, *ls_):
            d = arr.shape[0]
            in_specs.append(pl.BlockSpec((d, tm // d, arr.shape[2]),
                                         lambda i, local=local: (0, local(i), 0)))
        args += [f, *os_, *ls_]
        tile0 += tiles
    fixed = lambda i: (0, 0)
    in_specs += [pl.BlockSpec((1, D_FOURIER), fixed), pl.BlockSpec((1, d_attn), fixed)]
    max_groups = max(l.shape[2] // LANES for g in group_inputs for l in g[2])
    return pl.pallas_call(
        functools.partial(_merge_kernel, n_heads=n_heads, group_tiles=tuple(group_tiles)),
        grid=(n_tiles,),
        in_specs=in_specs,
        out_specs=pl.BlockSpec((tm, D_FOURIER + d_attn), lambda i: (i, 0)),
        out_shape=jax.ShapeDtypeStruct((n_tiles * tm, D_FOURIER + d_attn), BF16),
        scratch_shapes=[pltpu.VMEM((tm, d_attn), F32),
                        pltpu.VMEM((n_br, n_heads, tm, HEAD_DIM), F32),
                        pltpu.VMEM((n_br, max_groups, tm, LANES), F32),
                        pltpu.VMEM((n_heads + max_groups, SLAB_STRIDE, tm // SLAB_STRIDE, LANES), F32)],
        compiler_params=_params(1),
        name="merge_and_norm",
    )(*args, g_out_f, g_out_a)


def _dispatch_kernel(ends_ref, used_ref, pos_ref, h_ref, o_ref, zero_ref, sem, zero_sem,
                     *, tm, n_experts, n_tiles):
    i = pl.program_id(0)

    @pl.when(i == 0)
    def _():
        zero_ref[...] = jnp.zeros_like(zero_ref)

        def zero_tile(start):
            return pltpu.make_async_copy(zero_ref, o_ref.at[pl.ds(start, tm)], zero_sem)

        def expert_tiles(act):
            for e in range(n_experts):
                begin = ends_ref[e - 1] if e else 0

                @pl.when(ends_ref[e] > begin)
                def _(e=e):
                    act(zero_tile(ends_ref[e] - tm))

        def tail_tiles(act):
            def body(j, carry):
                act(zero_tile(j * tm))
                return carry
            lax.fori_loop(used_ref[0], n_tiles, body, 0)

        expert_tiles(lambda c: c.start())
        tail_tiles(lambda c: c.start())
        expert_tiles(lambda c: c.wait())
        tail_tiles(lambda c: c.wait())

    def copy(s):
        tok = lax.rem(s, tm)
        return pltpu.make_async_copy(h_ref.at[tok], o_ref.at[pos_ref[0, s]], sem)

    def start(s, carry):
        copy(s).start()
        return carry

    def wait(s, carry):
        copy(s).wait()
        return carry

    lax.fori_loop(0, TOP_K * tm, start, 0, unroll=8)
    lax.fori_loop(0, TOP_K * tm, wait, 0, unroll=8)


def dispatch_rows(h3, pos_tiles, ends, used, *, n_rows, tm):
    t = h3.shape[0]
    n_experts = ends.shape[0]
    return pl.pallas_call(
        functools.partial(_dispatch_kernel, tm=tm, n_experts=n_experts, n_tiles=n_rows // tm),
        grid_spec=pltpu.PrefetchScalarGridSpec(
            num_scalar_prefetch=2, grid=(t // tm,),
            in_specs=[pl.BlockSpec((None, 1, TOP_K * tm), lambda i, ends, used: (i, 0, 0),
                                   memory_space=pltpu.SMEM),
                      pl.BlockSpec((tm,) + h3.shape[1:], lambda i, ends, used: (i, 0, 0))],
            out_specs=pl.BlockSpec(memory_space=pl.ANY),
            scratch_shapes=[pltpu.VMEM((tm,) + h3.shape[1:], F32), pltpu.SemaphoreType.DMA(()),
                            pltpu.SemaphoreType.DMA(())]),
        out_shape=jax.ShapeDtypeStruct((n_rows,) + h3.shape[1:], F32),
        compiler_params=_params(1),
        name="moe_dispatch",
    )(ends, used, pos_tiles, h3)


SLAB_STRIDE = 4


def _rows_to_matrix_kernel(x_ref, o_ref, stage_ref):
    tm = o_ref.shape[0]
    per = x_ref.shape[0] // SLAB_STRIDE
    for a in range(SLAB_STRIDE):
        stage_ref[a] = x_ref[pl.ds(a, per, stride=SLAB_STRIDE), :]
    for a in range(SLAB_STRIDE):
        for b in range(per // tm):
            c = SLAB_STRIDE * b + a
            o_ref[:, c * LANES:(c + 1) * LANES] = (
                stage_ref[a, pl.ds(b, tm, stride=per // tm), :].astype(o_ref.dtype))


def rows_to_matrix(x3, *, tm):
    p, chunks, _ = x3.shape
    assert chunks % SLAB_STRIDE == 0
    return pl.pallas_call(
        _rows_to_matrix_kernel,
        grid=(p // tm,),
        in_specs=[pl.BlockSpec((tm * chunks, LANES), lambda i: (i, 0))],
        out_specs=pl.BlockSpec((tm, chunks * LANES), lambda i: (i, 0)),
        out_shape=jax.ShapeDtypeStruct((p, chunks * LANES), BF16),
        scratch_shapes=[pltpu.VMEM((SLAB_STRIDE, tm * chunks // SLAB_STRIDE, LANES), F32)],
        compiler_params=_params(1),
        name="moe_rows_to_matrix",
    )(x3.reshape(p * chunks, LANES))


def _combine_kernel(rows_ref, pos_ref, o_ref, route_ref, x_ref, gate_ref, *rest, group_tiles):
    del rows_ref
    buf_ref, sem = rest[-2:]
    tm = x_ref.shape[0]

    def copy(r):
        return pltpu.make_async_copy(o_ref.at[pl.ds(pos_ref[0, r], 1)],
                                     buf_ref.at[pl.ds(r, 1)], sem)

    def start(r, carry):
        copy(r).start()
        return carry

    def wait(r, carry):
        copy(r).wait()
        return carry

    lax.fori_loop(0, TOP_K * tm, start, 0, unroll=8)
    lax.fori_loop(0, TOP_K * tm, wait, 0, unroll=8)
    route = route_ref[...]
    y = route[:, 2:3] * buf_ref[pl.ds(0, tm), :] + route[:, 3:4] * buf_ref[pl.ds(tm, tm), :]
    res = x_ref[...] + gate_ref[...] * y
    if group_tiles is None:
        rest[0][...] = res
        return
    normed = _rms(res, rest[0][...])
    i = pl.program_id(0)
    tile0 = 0
    for out_ref, tiles in zip(rest[1:-2], group_tiles):
        @pl.when(jnp.logical_and(i >= tile0, i < tile0 + tiles))
        def _(out_ref=out_ref):
            out_ref[...] = normed

        tile0 += tiles


def moe_combine(o_sorted, pos, route, x, mod4, gate_part, rows, *, tm, final=None):
    t, d = x.shape
    in_specs = [pl.BlockSpec((None, 1, TOP_K * tm), lambda i, rows: (i, 0, 0), memory_space=pltpu.SMEM),
                pl.BlockSpec(memory_space=pl.ANY),
                pl.BlockSpec((tm, LANES), lambda i, rows: (i, 0)),
                pl.BlockSpec((tm, d), lambda i, rows: (i, 0)),
                pl.BlockSpec((None, None, 1, d), lambda i, rows: (rows[i], gate_part, 0, 0))]
    args = [rows, pos, o_sorted, route, x, mod4]
    if final is None:
        group_tiles = None
        out_specs = pl.BlockSpec((tm, d), lambda i, rows: (i, 0))
        out_shape = jax.ShapeDtypeStruct((t, d), F32)
    else:
        g_final, group_tokens = final
        group_tiles = tuple(n // tm for n in group_tokens)
        in_specs.append(pl.BlockSpec((1, d), lambda i, rows: (0, 0)))
        args.append(g_final)
        out_specs, out_shape, tile0 = [], [], 0
        for tiles in group_tiles:
            out_specs.append(pl.BlockSpec(
                (tm, d), lambda i, rows, tile0=tile0, tiles=tiles: (jnp.clip(i - tile0, 0, tiles - 1), 0)))
            out_shape.append(jax.ShapeDtypeStruct((tiles * tm, d), F32))
            tile0 += tiles
    return pl.pallas_call(
        functools.partial(_combine_kernel, group_tiles=group_tiles),
        grid_spec=pltpu.PrefetchScalarGridSpec(
            num_scalar_prefetch=1, grid=(t // tm,), in_specs=in_specs, out_specs=out_specs,
            scratch_shapes=[pltpu.VMEM((TOP_K * tm, d), F32), pltpu.SemaphoreType.DMA(())]),
        out_shape=out_shape,
        compiler_params=_params(1),
        name="moe_combine",
    )(*args)


def moe_dispatch_plan(route, counts, tm):
    t = route.shape[0]
    n_experts = counts.shape[0]
    p = t * TOP_K + n_experts * tm
    expert = route[:, :TOP_K].astype(jnp.int32)
    rank = route[:, 4:4 + TOP_K].astype(jnp.int32)
    padded = ((counts + tm - 1) // tm) * tm
    ends = jnp.cumsum(padded).astype(jnp.int32)
    starts = ends - padded
    is_expert = expert[:, :, None] == jnp.arange(n_experts)[None, None, :]
    pos = jnp.sum(jnp.where(is_expert, starts[None, None, :], 0), axis=-1) + rank
    tile_start = jnp.arange(p // tm, dtype=jnp.int32) * tm
    tile_expert = jnp.sum((tile_start[:, None] >= ends[None, :]).astype(jnp.int32), axis=1)
    tile_expert = jnp.minimum(tile_expert, n_experts - 1)
    used = (ends[-1:] // tm).astype(jnp.int32)
    valid = jnp.clip((starts + counts)[tile_expert] - tile_start, 0, tm).astype(jnp.int32)
    return p, pos, tile_expert, used, ends, valid


def _final_norm_kernel(x_ref, g_ref, o_ref):
    o_ref[...] = _rms(x_ref[...], g_ref[...])


def final_norm(x, g, *, row0, n_rows, tm=512):
    d = x.shape[1]
    blk0 = row0 // tm
    return pl.pallas_call(
        _final_norm_kernel,
        grid=(n_rows // tm,),
        in_specs=[pl.BlockSpec((tm, d), lambda i: (blk0 + i, 0)),
                  pl.BlockSpec((1, d), lambda i: (0, 0))],
        out_specs=pl.BlockSpec((tm, d), lambda i: (i, 0)),
        out_shape=jax.ShapeDtypeStruct((n_rows, d), F32),
        compiler_params=_params(1),
        name="final_norm",
    )(x, g)


TM = 512
TM_BIG = 1024
TM_COMBINE = 256


def _tile_rows(groups, tm):
    rows = []
    base = 0
    for n_rows, seq in groups:
        for b in range(n_rows):
            rows += [base + b] * (seq // tm)
        base += n_rows
    return jnp.asarray(np.asarray(rows, np.int32))


def kernel(x_prompt, x_sample, c_prompt, c_sample, w_ada, b_ada, g_norm_mix, g_norm_ff, w_in, w_fmix,
           g_out_f, g_out_a, w_out, w_ff1, w_ff3, w_ff2, w_router, w_e1, w_e3, w_e2, g_final):
    depth, d, d_in = w_in.shape
    groups = [(x_prompt.shape[0], x_prompt.shape[1]), (x_sample.shape[0], x_sample.shape[1])]
    group_row0 = [0, groups[0][0] * groups[0][1]]
    t = sum(b * s for b, s in groups)
    n_req = sum(b for b, _ in groups)
    d_attn = d - D_FOURIER
    n_heads = d_attn // HEAD_DIM
    n_experts = w_router.shape[-1]
    d_ff_e = w_e1.shape[-1]
    dilations = [dil for _, dil in DILATED_BRANCHES]

    x = jnp.concatenate([x_prompt.reshape(-1, d), x_sample.reshape(-1, d)], axis=0)
    c = jnp.concatenate([c_prompt, c_sample], axis=0)
    c = jnp.pad(c, ((0, -n_req % 8), (0, 0)))
    mod = ada_modulation(c, w_ada, b_ada)
    rows = _tile_rows(groups, TM)
    rows_c = _tile_rows(groups, TM_COMBINE)
    def dense_tiling(tm):
        tiles = t // tm
        return Tiling(jnp.zeros((tiles,), jnp.int32), _tile_rows(groups, tm),
                      jnp.full((1,), tiles, jnp.int32), jnp.full((tiles,), tm, jnp.int32))

    dense, dense_big = dense_tiling(TM), dense_tiling(TM_BIG)
    slopes = jnp.exp2(-8.0 * (jnp.arange(n_heads, dtype=F32) + 1.0) / n_heads)
    ab = fourier_ab(w_fmix)
    dft = [dft_matrix(seq) for _, seq in groups]

    for l in range(depth):
        mod4 = mod[l].reshape(mod.shape[1], 6, 1, d)
        layer = lambda e, l=l: l
        h = norm_modulate(x, rows, g_norm_mix[l:l + 1], mod4, 0, tm=TM)
        u = matmul(h, w_in, dense_big, w_index=layer, tm=TM_BIG, tn=D_FOURIER, out_dtype=BF16,
                   n=D_FOURIER)
        qkvs = matmul_qkv(h, w_in, dense_big, w_index=layer, tm=TM_BIG, tn=512, col0=D_FOURIER,
                          n=3 * d_attn, dilations=dilations, q_cols=d_attn,
                          q_scale=HEAD_DIM ** -0.5 * LOG2_E)

        group_inputs = []
        for gi, ((n_rows, seq), row0) in enumerate(zip(groups, group_row0)):
            pq = fourier_channel_stage(u, ab, l, n_rows=n_rows, seq=seq, row_block0=row0 // TM, tm=TM)
            f = fourier_sequence_stage(dft[gi], pq, n_rows=n_rows, seq=seq, tm=TM)
            branches = [attention_branch(qkv, slopes, window=window, dilation=dilation, n_rows=n_rows,
                                         seq=seq, row0=row0, n_heads=n_heads)
                        for qkv, (window, dilation) in zip(qkvs, DILATED_BRANCHES)]
            group_inputs.append((f, [b[0] for b in branches], [b[1] for b in branches]))
        y = merge_and_norm(group_inputs, g_out_f[l:l + 1], g_out_a[l:l + 1], n_heads=n_heads)
        x = matmul_gated_residual(y, w_out, x, mod4, 2, dense_big, w_index=layer, tm=TM_BIG, tn=1024)

        j = l // 2
        if l % 2 == 0:
            h = norm_modulate(x, rows, g_norm_ff[l:l + 1], mod4, 3, tm=TM)
            act = matmul_swiglu(h, w_ff1, w_ff3, dense_big, w_index=lambda e, j=j: j, tm=TM_BIG, tn=512)
            x = matmul_gated_residual(act, w_ff2, x, mod4, 5, dense, w_index=lambda e, j=j: j,
                                      tm=TM, tn=512)
        else:
            h, route, counts = norm_modulate(x, rows, g_norm_ff[l:l + 1], mod4, 3, out_dtype=F32,
                                             w_router=w_router[j], tm=TM)
            n_sorted, pos, tile_expert, used, ends, valid = moe_dispatch_plan(route, counts, TM)
            routed = Tiling(tile_expert, tile_expert, used, valid)

            def choice_major(tm):
                tiles = pos.reshape(t // tm, tm, TOP_K).transpose(0, 2, 1)
                return tiles.reshape(t // tm, 1, TOP_K * tm)

            xs = rows_to_matrix(dispatch_rows(h, choice_major(TM), ends, used, n_rows=n_sorted, tm=TM),
                                tm=TM)
            expert_w = lambda e, j=j: j * n_experts + e
            act = matmul_swiglu(xs, w_e1.reshape(-1, d, d_ff_e), w_e3.reshape(-1, d, d_ff_e), routed,
                                w_index=expert_w, tm=TM, tn=1024)
            o_sorted = matmul(act, w_e2.reshape(-1, d_ff_e, d), routed, w_index=expert_w,
                              tm=TM, tn=512, out_dtype=F32, vmem=VMEM_LIMIT_BIG)
            final = (g_final.reshape(1, d), [b * s for b, s in groups]) if l == depth - 1 else None
            x = moe_combine(o_sorted, choice_major(TM_COMBINE), route, x, mod4, 5, rows_c, tm=TM_COMBINE,
                            final=final)

    if isinstance(x, (list, tuple)):
        return tuple(xg.reshape(b, s, d) for xg, (b, s) in zip(x, groups))
    g = g_final.reshape(1, d)
    outs = []
    for (n_rows, seq), row0 in zip(groups, group_row0):
        outs.append(final_norm(x, g, row0=row0, n_rows=n_rows * seq, tm=TM).reshape(n_rows, seq, d))
    return tuple(outs)
```

```python
import functools
import math

import numpy as np
import jax
import jax.numpy as jnp
from jax import lax
from jax.experimental import pallas as pl
from jax.experimental.pallas import tpu as pltpu

HEAD_DIM = 128
FOURIER_GROUP = 128
N_FOURIER_GROUPS = 4
D_FOURIER = FOURIER_GROUP * N_FOURIER_GROUPS
DILATED_BRANCHES = ((128, 1), (512, 4), (2048, 16))
TOP_K = 2
EPS = 1e-6
NEG_INF = -1e30
LOG2_E = math.log2(math.e)
LANES = 128
VMEM_LIMIT = 56 * 1024 * 1024
VMEM_LIMIT_BIG = 62 * 1024 * 1024
ATTN_BLOCK_BYTES = 4 * 1024 * 1024

BF16 = jnp.bfloat16
F32 = jnp.float32


def _params(n_axes, vmem=VMEM_LIMIT):
    return pltpu.CompilerParams(
        dimension_semantics=("arbitrary",) * n_axes, vmem_limit_bytes=vmem)


def _ada_kernel(c_ref, w_ref, b_ref, o_ref):
    c = c_ref[...]
    a = (c * jax.nn.sigmoid(c)).astype(BF16)
    acc = jnp.dot(a, w_ref[...].astype(BF16), preferred_element_type=F32)
    o_ref[...] = acc + b_ref[...]


def ada_modulation(c, w_ada, b_ada, tn=1024):
    n_layers, d, n = w_ada.shape
    r8 = c.shape[0]
    return pl.pallas_call(
        _ada_kernel,
        grid=(n_layers, n // tn),
        in_specs=[
            pl.BlockSpec((r8, d), lambda l, j: (0, 0)),
            pl.BlockSpec((None, d, tn), lambda l, j: (l, 0, j)),
            pl.BlockSpec((None, 1, tn), lambda l, j: (l, 0, j)),
        ],
        out_specs=pl.BlockSpec((None, r8, tn), lambda l, j: (l, 0, j)),
        out_shape=jax.ShapeDtypeStruct((n_layers, r8, n), F32),
        compiler_params=_params(2),
        name="ada_modulation",
    )(c, w_ada, b_ada.reshape(n_layers, 1, n))


def _norm_mod(x, g, sc, sh):
    y = x * lax.rsqrt(jnp.mean(x * x, axis=-1, keepdims=True) + EPS)
    return (y * g) * (1.0 + sc) + sh


def _norm_mod_kernel(rows_ref, x_ref, g_ref, sc_ref, sh_ref, h_ref):
    del rows_ref
    h_ref[...] = _norm_mod(x_ref[...], g_ref[...], sc_ref[...], sh_ref[...]).astype(h_ref.dtype)


def _norm_mod_router_kernel(rows_ref, x_ref, g_ref, sc_ref, sh_ref, wr_ref, h_ref, route_ref,
                            totals_ref, stage_ref, counts_ref, earlier_ref, *, n_experts):
    del rows_ref
    h = _norm_mod(x_ref[...], g_ref[...], sc_ref[...], sh_ref[...])
    tm = h.shape[0]
    per = h_ref.shape[0] // SLAB_STRIDE
    for a in range(SLAB_STRIDE):
        for b in range(per // tm):
            c = SLAB_STRIDE * b + a
            stage_ref[a, pl.ds(b, tm, stride=per // tm), :] = h[:, c * LANES:(c + 1) * LANES]
    for a in range(SLAB_STRIDE):
        h_ref[pl.ds(a, per, stride=SLAB_STRIDE), :] = stage_ref[a]
    w = wr_ref[...]
    h_hi, w_hi = h.astype(BF16), w.astype(BF16)
    h_lo = (h - h_hi.astype(F32)).astype(BF16)
    w_lo = (w - w_hi.astype(F32)).astype(BF16)
    logits = (jnp.dot(h_hi, w_hi, preferred_element_type=F32)
              + (jnp.dot(h_hi, w_lo, preferred_element_type=F32)
                 + jnp.dot(h_lo, w_hi, preferred_element_type=F32)))
    lane = lax.broadcasted_iota(jnp.int32, logits.shape, 1)
    logits = jnp.where(lane < n_experts, logits, -jnp.inf)
    m1 = jnp.max(logits, axis=-1, keepdims=True)
    i1 = jnp.min(jnp.where(logits == m1, lane, LANES), axis=-1, keepdims=True)
    rest = jnp.where(lane == i1, -jnp.inf, logits)
    m2 = jnp.max(rest, axis=-1, keepdims=True)
    i2 = jnp.min(jnp.where(rest == m2, lane, LANES), axis=-1, keepdims=True)
    e2 = jnp.exp(m2 - m1)
    den = 1.0 + e2
    @pl.when(pl.program_id(0) == 0)
    def _():
        counts_ref[...] = jnp.zeros_like(counts_ref)
        earlier_ref[...] = (lax.broadcasted_iota(jnp.int32, earlier_ref.shape, 1)
                            < lax.broadcasted_iota(jnp.int32, earlier_ref.shape, 0)).astype(BF16)

    pick1, pick2 = lane == i1, lane == i2
    picked = jnp.logical_or(pick1, pick2)
    before = counts_ref[0:1, :] + jnp.dot(earlier_ref[...], picked.astype(BF16),
                                          preferred_element_type=F32)
    rank1 = jnp.sum(jnp.where(pick1, before, 0.0), axis=-1, keepdims=True)
    rank2 = jnp.sum(jnp.where(pick2, before, 0.0), axis=-1, keepdims=True)
    counts_ref[...] = counts_ref[...] + jnp.sum(picked.astype(F32), axis=0, keepdims=True)
    totals_ref[...] = counts_ref[...]

    route = jnp.where(lane == 0, i1.astype(F32), 0.0)
    route = jnp.where(lane == 1, i2.astype(F32), route)
    route = jnp.where(lane == 2, 1.0 / den, route)
    route = jnp.where(lane == 3, e2 / den, route)
    route = jnp.where(lane == 4, rank1, route)
    route = jnp.where(lane == 5, rank2, route)
    route_ref[...] = route


def _norm_mod_groups_kernel(rows_ref, *refs, group_tiles):
    del rows_ref
    n_groups = len(group_tiles)
    g_ref, sc_ref, sh_ref, h_ref, x_ref = refs[n_groups:]
    i = pl.program_id(0)
    tile0 = 0
    for xg_ref, tiles in zip(refs[:n_groups], group_tiles):
        @pl.when(jnp.logical_and(i >= tile0, i < tile0 + tiles))
        def _(xg_ref=xg_ref):
            x = xg_ref[...]
            x_ref[...] = x
            h_ref[...] = _norm_mod(x, g_ref[...], sc_ref[...], sh_ref[...]).astype(h_ref.dtype)

        tile0 += tiles


def norm_modulate_groups(x_groups, rows, g, mod4, layer_part, *, tm=512):
    d = x_groups[0].shape[1]
    group_tiles = tuple(xg.shape[0] // tm for xg in x_groups)
    n_tiles = sum(group_tiles)
    sh_part, sc_part = layer_part, layer_part + 1
    in_specs, tile0 = [], 0
    for tiles in group_tiles:
        in_specs.append(pl.BlockSpec(
            (tm, d), lambda i, rows, tile0=tile0, tiles=tiles: (jnp.clip(i - tile0, 0, tiles - 1), 0)))
        tile0 += tiles
    in_specs += [pl.BlockSpec((1, d), lambda i, rows: (0, 0)),
                 pl.BlockSpec((None, None, 1, d), lambda i, rows: (rows[i], sc_part, 0, 0)),
                 pl.BlockSpec((None, None, 1, d), lambda i, rows: (rows[i], sh_part, 0, 0))]
    row = lambda i, rows: (i, 0)
    return pl.pallas_call(
        functools.partial(_norm_mod_groups_kernel, group_tiles=group_tiles),
        grid_spec=pltpu.PrefetchScalarGridSpec(
            num_scalar_prefetch=1, grid=(n_tiles,), in_specs=in_specs,
            out_specs=[pl.BlockSpec((tm, d), row), pl.BlockSpec((tm, d), row)]),
        out_shape=[jax.ShapeDtypeStruct((n_tiles * tm, d), BF16),
                   jax.ShapeDtypeStruct((n_tiles * tm, d), F32)],
        compiler_params=_params(1),
        name="norm_modulate_groups",
    )(rows, *x_groups, g, mod4, mod4)


def norm_modulate(x, rows, g, mod4, layer_part, *, out_dtype=BF16, w_router=None, tm=512):
    t, d = x.shape
    sh_part, sc_part = layer_part, layer_part + 1
    in_specs = [
        pl.BlockSpec((tm, d), lambda i, rows: (i, 0)),
        pl.BlockSpec((1, d), lambda i, rows: (0, 0)),
        pl.BlockSpec((None, None, 1, d), lambda i, rows: (rows[i], sc_part, 0, 0)),
        pl.BlockSpec((None, None, 1, d), lambda i, rows: (rows[i], sh_part, 0, 0)),
    ]
    h_spec = pl.BlockSpec((tm, d), lambda i, rows: (i, 0))
    h_shape = jax.ShapeDtypeStruct((t, d), out_dtype)
    if w_router is None:
        return pl.pallas_call(
            _norm_mod_kernel,
            grid_spec=pltpu.PrefetchScalarGridSpec(
                num_scalar_prefetch=1, grid=(t // tm,), in_specs=in_specs, out_specs=h_spec),
            out_shape=h_shape,
            compiler_params=_params(1),
            name="norm_modulate",
        )(rows, x, g, mod4, mod4)
    n_experts = w_router.shape[1]
    wr = jnp.pad(w_router, ((0, 0), (0, LANES - n_experts)))
    in_specs.append(pl.BlockSpec((d, LANES), lambda i, rows: (0, 0)))
    chunks = d // LANES
    assert chunks % SLAB_STRIDE == 0
    h_flat, route, totals = pl.pallas_call(
        functools.partial(_norm_mod_router_kernel, n_experts=n_experts),
        grid_spec=pltpu.PrefetchScalarGridSpec(
            num_scalar_prefetch=1, grid=(t // tm,), in_specs=in_specs,
            out_specs=[pl.BlockSpec((tm * chunks, LANES), lambda i, rows: (i, 0)),
                       pl.BlockSpec((tm, LANES), lambda i, rows: (i, 0)),
                       pl.BlockSpec((8, LANES), lambda i, rows: (0, 0))],
            scratch_shapes=[pltpu.VMEM((SLAB_STRIDE, tm * chunks // SLAB_STRIDE, LANES), F32),
                            pltpu.VMEM((8, LANES), F32), pltpu.VMEM((tm, tm), BF16)]),
        out_shape=[jax.ShapeDtypeStruct((t * chunks, LANES), F32),
                   jax.ShapeDtypeStruct((t, LANES), F32),
                   jax.ShapeDtypeStruct((8, LANES), F32)],
        compiler_params=_params(1),
        name="norm_modulate_router",
    )(rows, x, g, mod4, mod4, wr)
    return h_flat.reshape(t, chunks, LANES), route, totals[0, :n_experts].astype(jnp.int32)


CAST_ROWS = 512


def _cast_weights_if_changed(te_ref, w_refs, wb_refs):
    i = pl.program_id(1)
    changed = jnp.logical_or(i == 0, te_ref[i] != te_ref[jnp.maximum(i - 1, 0)])

    @pl.when(changed)
    def _():
        k = w_refs[0].shape[0]
        step = math.gcd(k, CAST_ROWS)

        def body(c, carry):
            r0 = pl.multiple_of(c * step, step)
            for w_ref, wb_ref in zip(w_refs, wb_refs):
                wb_ref[pl.ds(r0, step), :] = w_ref[pl.ds(r0, step), :].astype(BF16)
            return carry

        lax.fori_loop(0, k // step, body, 0)


ROW_PARTS = 4


def _for_valid_rows(valid_ref, o_ref, compute):
    tm = o_ref.shape[0]
    quantum = tm // ROW_PARTS
    quanta = (valid_ref[pl.program_id(1)] + quantum - 1) // quantum
    for n in range(ROW_PARTS + 1):
        rows = n * quantum

        @pl.when(quanta == n)
        def _(rows=rows):
            if rows:
                o_ref[:rows, :] = compute(rows).astype(o_ref.dtype)
            if rows < tm:
                o_ref[rows:, :] = jnp.zeros((tm - rows, o_ref.shape[1]), o_ref.dtype)


def _mm_plain_kernel(te_ref, rows_ref, used_ref, valid_ref, a_ref, w_ref, o_ref, wb_ref):
    del rows_ref, used_ref
    _cast_weights_if_changed(te_ref, (w_ref,), (wb_ref,))
    _for_valid_rows(valid_ref, o_ref, lambda rows: jnp.dot(
        a_ref[:rows, :], wb_ref[...], preferred_element_type=F32))


def _mm_qkv_kernel(te_ref, rows_ref, used_ref, valid_ref, a_ref, w_ref, *rest, dilations, q_chunks,
                   q_scale):
    del rows_ref, used_ref, valid_ref
    out_refs = rest[:len(dilations)]
    wb_ref, acc_ref, stage_ref = rest[len(dilations):]
    _cast_weights_if_changed(te_ref, (w_ref,), (wb_ref,))
    acc = jnp.dot(a_ref[...], wb_ref[...], preferred_element_type=F32)
    acc = acc * jnp.where(pl.program_id(0) < q_chunks, q_scale, 1.0).astype(F32)
    tm, tn = acc.shape
    slabs = tn // LANES
    for c in range(slabs):
        acc_ref[c] = acc[:, c * LANES:(c + 1) * LANES]
    cur_ref, cur_d = acc_ref, 1
    for idx, (o_ref, d) in enumerate(zip(out_refs, dilations)):
        if d == 1:
            o_ref[0] = acc.astype(o_ref.dtype)
            continue
        f = d // cur_d
        keep = idx + 1 < len(dilations)
        for rp in range(cur_d):
            for b in range(f):
                r = b * cur_d + rp
                for c in range(slabs):
                    piece = cur_ref[c, pl.ds(rp * (tm // cur_d) + b, tm // d, stride=f), :]
                    o_ref[r, :, c * LANES:(c + 1) * LANES] = piece.astype(o_ref.dtype)
                    if keep:
                        stage_ref[c, pl.ds(r * (tm // d), tm // d), :] = piece
        cur_ref, cur_d = stage_ref, d


def _mm_swiglu_kernel(te_ref, rows_ref, used_ref, valid_ref, a_ref, w1_ref, w3_ref, o_ref, wb1_ref,
                      wb3_ref):
    del rows_ref, used_ref
    _cast_weights_if_changed(te_ref, (w1_ref, w3_ref), (wb1_ref, wb3_ref))

    def compute(rows):
        a = a_ref[:rows, :]
        g = jnp.dot(a, wb1_ref[...], preferred_element_type=F32)
        u = jnp.dot(a, wb3_ref[...], preferred_element_type=F32)
        return (g * jax.nn.sigmoid(g)) * u

    _for_valid_rows(valid_ref, o_ref, compute)


def _mm_resid_kernel(te_ref, rows_ref, used_ref, valid_ref, a_ref, w_ref, x_ref, gate_ref, o_ref,
                     wb_ref):
    del rows_ref, used_ref, valid_ref
    _cast_weights_if_changed(te_ref, (w_ref,), (wb_ref,))
    acc = jnp.dot(a_ref[...], wb_ref[...], preferred_element_type=F32)
    o_ref[...] = x_ref[...] + gate_ref[...] * acc


def _a_spec(tm, k):
    return pl.BlockSpec((tm, k), lambda j, i, te, rows, used, valid: (jnp.minimum(i, used[0] - 1), 0))


def _w_spec(k, tn, w_index, col_block0=0, single_buffer=False):
    mode = dict(pipeline_mode=pl.Buffered(1)) if single_buffer else {}
    return pl.BlockSpec((None, k, tn),
                        lambda j, i, te, rows, used, valid: (w_index(te[i]), 0, col_block0 + j), **mode)


def _out_spec(tm, tn):
    return pl.BlockSpec((tm, tn), lambda j, i, te, rows, used, valid: (i, j))


class Tiling:
    def __init__(self, tile_expert, rows, used, valid):
        self.args = (tile_expert, rows, used, valid)


def matmul_qkv(a, w, tiling, *, w_index, tm, tn, col0, n, dilations, q_cols, q_scale):
    m, k = a.shape
    assert sum(d > 1 for d in dilations) <= 2
    return pl.pallas_call(
        functools.partial(_mm_qkv_kernel, dilations=tuple(dilations), q_chunks=q_cols // tn,
                          q_scale=q_scale),
        grid_spec=pltpu.PrefetchScalarGridSpec(
            num_scalar_prefetch=4, grid=(n // tn, m // tm),
            in_specs=[_a_spec(tm, k), _w_spec(k, tn, w_index, col0 // tn)],
            out_specs=[pl.BlockSpec((d, tm // d, tn), lambda j, i, te, rows, used, valid: (0, i, j))
                       for d in dilations],
            scratch_shapes=[pltpu.VMEM((k, tn), BF16)]
                           + [pltpu.VMEM((tn // LANES, tm, LANES), F32)] * 2),
        out_shape=[jax.ShapeDtypeStruct((d, m // d, n), BF16) for d in dilations],
        compiler_params=_params(2),
        name="matmul_qkv",
    )(*tiling.args, a, w)


def matmul(a, w, tiling, *, w_index, tm, tn, out_dtype, n=None, vmem=VMEM_LIMIT):
    m, k = a.shape
    n = w.shape[-1] if n is None else n
    return pl.pallas_call(
        _mm_plain_kernel,
        grid_spec=pltpu.PrefetchScalarGridSpec(
            num_scalar_prefetch=4, grid=(n // tn, m // tm),
            in_specs=[_a_spec(tm, k), _w_spec(k, tn, w_index)],
            out_specs=_out_spec(tm, tn),
            scratch_shapes=[pltpu.VMEM((k, tn), BF16)]),
        out_shape=jax.ShapeDtypeStruct((m, n), out_dtype),
        compiler_params=_params(2, vmem),
        name="matmul",
    )(*tiling.args, a, w)


def matmul_swiglu(a, w1, w3, tiling, *, w_index, tm, tn):
    m, k = a.shape
    n = w1.shape[-1]
    return pl.pallas_call(
        _mm_swiglu_kernel,
        grid_spec=pltpu.PrefetchScalarGridSpec(
            num_scalar_prefetch=4, grid=(n // tn, m // tm),
            in_specs=[_a_spec(tm, k), _w_spec(k, tn, w_index), _w_spec(k, tn, w_index)],
            out_specs=_out_spec(tm, tn),
            scratch_shapes=[pltpu.VMEM((k, tn), BF16), pltpu.VMEM((k, tn), BF16)]),
        out_shape=jax.ShapeDtypeStruct((m, n), BF16),
        compiler_params=_params(2),
        name="matmul_swiglu",
    )(*tiling.args, a, w1, w3)


def matmul_gated_residual(a, w, x, mod4, gate_part, tiling, *, w_index, tm, tn):
    m, k = a.shape
    n = w.shape[-1]
    return pl.pallas_call(
        _mm_resid_kernel,
        grid_spec=pltpu.PrefetchScalarGridSpec(
            num_scalar_prefetch=4, grid=(n // tn, m // tm),
            in_specs=[_a_spec(tm, k), _w_spec(k, tn, w_index), _out_spec(tm, tn),
                      pl.BlockSpec((None, None, 1, tn),
                                   lambda j, i, te, rows, used, valid: (rows[i], gate_part, 0, j))],
            out_specs=_out_spec(tm, tn),
            scratch_shapes=[pltpu.VMEM((k, tn), BF16)]),
        out_shape=jax.ShapeDtypeStruct((m, n), F32),
        compiler_params=_params(2),
        name="matmul_gated_residual",
    )(*tiling.args, a, w, x, mod4)


def _dft_cos_sin(n):
    j = lax.broadcasted_iota(jnp.int32, (n, n), 0)
    k = lax.broadcasted_iota(jnp.int32, (n, n), 1)
    ang = ((j * k) % n).astype(F32) * (2.0 * math.pi / n)
    return jnp.cos(ang), jnp.sin(ang)


def _fourier_ab_kernel(c_ref, s_ref, w_ref, o_ref):
    w = w_ref[...]
    scale = FOURIER_GROUP ** -0.5
    a = jnp.dot(c_ref[...], w, preferred_element_type=F32, precision=lax.Precision.HIGHEST)
    b = jnp.dot(s_ref[...], w, preferred_element_type=F32, precision=lax.Precision.HIGHEST)
    o_ref[:, :FOURIER_GROUP] = (a * scale).astype(o_ref.dtype)
    o_ref[:, FOURIER_GROUP:] = (b * scale).astype(o_ref.dtype)


def fourier_ab(w_fmix):
    n_layers, n_groups, c, _ = w_fmix.shape
    cos_g, sin_g = _dft_cos_sin(c)
    return pl.pallas_call(
        _fourier_ab_kernel,
        grid=(n_layers, n_groups),
        in_specs=[pl.BlockSpec((c, c), lambda l, g: (0, 0)),
                  pl.BlockSpec((c, c), lambda l, g: (0, 0)),
                  pl.BlockSpec((None, None, c, c), lambda l, g: (l, g, 0, 0))],
        out_specs=pl.BlockSpec((None, None, c, 2 * c), lambda l, g: (l, g, 0, 0)),
        out_shape=jax.ShapeDtypeStruct((n_layers, n_groups, c, 2 * c), BF16),
        compiler_params=_params(2),
        name="fourier_ab",
    )(cos_g, sin_g, w_fmix)


def _fourier_channel_kernel(u_ref, ab_ref, pq_ref):
    c = FOURIER_GROUP
    for g in range(N_FOURIER_GROUPS):
        pq = jnp.dot(u_ref[:, g * c:(g + 1) * c], ab_ref[g], preferred_element_type=F32)
        pq_ref[0, :, g * c:(g + 1) * c] = pq[:, :c].astype(pq_ref.dtype)
        pq_ref[1, :, g * c:(g + 1) * c] = pq[:, c:].astype(pq_ref.dtype)


def fourier_channel_stage(u, ab, layer, *, n_rows, seq, row_block0, tm=512):
    tiles = seq // tm
    return pl.pallas_call(
        _fourier_channel_kernel,
        grid=(n_rows, tiles),
        in_specs=[pl.BlockSpec((tm, D_FOURIER), lambda b, i: (row_block0 + b * tiles + i, 0)),
                  pl.BlockSpec((None, N_FOURIER_GROUPS, FOURIER_GROUP, 2 * FOURIER_GROUP),
                               lambda b, i: (layer, 0, 0, 0))],
        out_specs=pl.BlockSpec((None, 2, tm, D_FOURIER), lambda b, i: (b, 0, i, 0)),
        out_shape=jax.ShapeDtypeStruct((n_rows, 2, seq, D_FOURIER), BF16),
        compiler_params=_params(2),
        name="fourier_channel_stage",
    )(u, ab)


def _fourier_seq_kernel(cs_ref, pq_ref, o_ref, *, scale):
    o_ref[...] = jnp.dot(cs_ref[...], pq_ref[...], preferred_element_type=F32) * scale


def fourier_sequence_stage(cs, pq, *, n_rows, seq, tm=512):
    tiles = seq // tm
    return pl.pallas_call(
        functools.partial(_fourier_seq_kernel, scale=seq ** -0.5),
        grid=(n_rows, tiles),
        in_specs=[pl.BlockSpec((tm, 2 * seq), lambda b, i: (i, 0)),
                  pl.BlockSpec((None, 2 * seq, D_FOURIER), lambda b, i: (b, 0, 0))],
        out_specs=pl.BlockSpec((tm, D_FOURIER), lambda b, i: (b * tiles + i, 0)),
        out_shape=jax.ShapeDtypeStruct((n_rows * seq, D_FOURIER), F32),
        compiler_params=_params(2),
        name="fourier_sequence_stage",
    )(cs, pq.reshape(n_rows, 2 * seq, D_FOURIER))


DFT_SPLIT = 64


def _dft_kernel(t1_ref, t2_ref, o_ref):
    seq = t2_ref.shape[2]
    c1, s1 = t1_ref[0], t1_ref[1]
    c2, s2 = t2_ref[0], t2_ref[1]
    o_ref[:, :seq] = (c1 * c2 - s1 * s2).astype(o_ref.dtype)
    o_ref[:, seq:] = (-(s1 * c2 + c1 * s2)).astype(o_ref.dtype)


def dft_matrix(seq):
    k = jnp.arange(seq, dtype=jnp.int32)

    def table(mult):
        ang = ((mult[:, None] * k[None, :]) % seq).astype(F32) * (2.0 * math.pi / seq)
        return jnp.stack([jnp.cos(ang), jnp.sin(ang)])

    n_coarse = seq // DFT_SPLIT
    t1 = table(jnp.arange(n_coarse, dtype=jnp.int32) * DFT_SPLIT).reshape(2, n_coarse, 1, seq)
    t2 = table(jnp.arange(DFT_SPLIT, dtype=jnp.int32))
    return pl.pallas_call(
        _dft_kernel,
        grid=(n_coarse,),
        in_specs=[pl.BlockSpec((2, None, 1, seq), lambda a: (0, a, 0, 0)),
                  pl.BlockSpec((2, DFT_SPLIT, seq), lambda a: (0, 0, 0))],
        out_specs=pl.BlockSpec((DFT_SPLIT, 2 * seq), lambda a: (a, 0)),
        out_shape=jax.ShapeDtypeStruct((seq, 2 * seq), BF16),
        compiler_params=_params(1),
        name="dft_matrix",
    )(t1, t2)


def _attn_kernel(slopes_ref, q_ref, k_ref, v_ref, o_ref, lse_ref, bias_ref, *, dilation, n_side,
                 bq, bk, hps, stack, unroll):
    n_res, sub = q_ref.shape[:2]
    n_cases = bias_ref.shape[0]
    blocks = sub // bq if stack == 1 else 1
    lane = lax.broadcasted_iota(jnp.int32, (bq, LANES), 1)

    @pl.when(jnp.logical_and(pl.program_id(1) == 0, pl.program_id(2) == 0))
    def _():
        row = lax.broadcasted_iota(jnp.int32, (bq, bk), 0)
        col = lax.broadcasted_iota(jnp.int32, (bq, bk), 1)
        for case in range(n_cases):
            dist = jnp.abs(col - row - case * n_side)
            valid = dist <= n_side
            if stack > 1:
                valid = jnp.logical_and(valid, row // sub == col // sub)
            dist_f = dist.astype(F32) * float(dilation)
            for hh in range(hps):
                slope = slopes_ref[pl.program_id(0) * hps + hh] * LOG2_E
                bias_ref[case, hh] = jnp.where(valid, -slope * dist_f, NEG_INF)

    def body(it, carry):
        if stack == 1:
            res = it // blocks
            q0 = pl.multiple_of((it % blocks) * bq, bq)
            ks = pl.multiple_of(jnp.clip(q0 - n_side, 0, sub - bk), n_side)
            case = (q0 - ks) // n_side
            q_at = lambda ref, cols: ref[res, pl.ds(q0, bq), cols]
            k_at = lambda ref, cols: ref[res, pl.ds(ks, bk), cols]
        else:
            case = 0
            res = pl.ds(pl.multiple_of(it * stack, stack), stack)
            q_at = k_at = lambda ref, cols: ref[res, :, cols].reshape(stack * sub, HEAD_DIM)
        lse_tile = jnp.zeros((bq, LANES), F32)
        for hh in range(hps):
            cols = slice(hh * HEAD_DIM, (hh + 1) * HEAD_DIM)
            q = q_at(q_ref, cols)
            k = k_at(k_ref, cols)
            v = k_at(v_ref, cols)
            s = lax.dot_general(q, k, (((1,), (1,)), ((), ())), preferred_element_type=F32)
            s = s + bias_ref[case, hh]
            m = jnp.max(s, axis=-1, keepdims=True)
            p = jnp.exp2(s - m)
            den = jnp.sum(p, axis=-1, keepdims=True)
            num = jnp.dot(p.astype(BF16), v, preferred_element_type=F32)
            o = (num / den).astype(o_ref.dtype)
            if stack == 1:
                o_ref[res, pl.ds(q0, bq), cols] = o
            else:
                o_ref[res, :, cols] = o.reshape(stack, sub, HEAD_DIM)
            lse_tile = jnp.where(lane == pl.program_id(0) * hps + hh, m + jnp.log2(den), lse_tile)
        if stack == 1:
            lse_ref[res, pl.ds(q0, bq), :] = lse_tile
        else:
            lse_ref[res] = lse_tile.reshape(stack, sub, LANES)
        return carry

    lax.fori_loop(0, n_res * blocks // stack, body, 0, unroll=unroll)


ATTN_CHAINS = 12
ATTN_KEY_BLOCK = 256


def attention_step_shape(sub, n_heads, dilation, rows_per_trip):
    def largest(n, unit):
        return max(c for c in range(1, n + 1) if n % c == 0 and c * unit <= ATTN_BLOCK_BYTES)

    hps = largest(n_heads, sub * HEAD_DIM * 2)
    n_res = largest(dilation, sub * hps * HEAD_DIM * 2)
    trips = n_res * sub // rows_per_trip
    unroll = max(u for u in range(1, trips + 1) if trips % u == 0 and u * hps <= max(ATTN_CHAINS, hps))
    return hps, n_res, unroll


def attention_branch(qkv, slopes, *, window, dilation, n_rows, seq, row0, n_heads):
    tg = n_rows * seq
    sub = seq // dilation
    n_side = (window // 2) // dilation
    if sub >= ATTN_KEY_BLOCK:
        stack, bq, bk, n_cases = 1, ATTN_KEY_BLOCK - 2 * n_side, ATTN_KEY_BLOCK, 3
    else:
        stack = ATTN_KEY_BLOCK // sub
        bq, bk, n_cases = ATTN_KEY_BLOCK, ATTN_KEY_BLOCK, 1
    hps, n_res, unroll = attention_step_shape(sub, n_heads, dilation, bq)
    assert n_res % stack == 0 and sub % (bq // stack) == 0
    gw = hps * HEAD_DIM
    groups = n_heads // hps
    d_attn = n_heads * HEAD_DIM
    rb0 = row0 // seq

    def in_map(part):
        return lambda g, b, r, slopes: (r, rb0 + b, part * groups + g)

    out_map = lambda g, b, r, slopes: (r, b, g)
    o, lse = pl.pallas_call(
        functools.partial(_attn_kernel, dilation=dilation, n_side=n_side, bq=bq, bk=bk, hps=hps,
                          stack=stack, unroll=unroll),
        grid_spec=pltpu.PrefetchScalarGridSpec(
            num_scalar_prefetch=1, grid=(groups, n_rows, dilation // n_res),
            in_specs=[pl.BlockSpec((n_res, sub, gw), in_map(part)) for part in range(3)],
            out_specs=[pl.BlockSpec((n_res, sub, gw), out_map),
                       pl.BlockSpec((n_res, sub, LANES), out_map)],
            scratch_shapes=[pltpu.VMEM((n_cases, hps, bq, bk), F32)]),
        out_shape=[jax.ShapeDtypeStruct((dilation, tg // dilation, d_attn), BF16),
                   jax.ShapeDtypeStruct((dilation, tg // dilation, groups * LANES), F32)],
        compiler_params=_params(3),
        name=f"attention_d{dilation}",
    )(slopes, qkv, qkv, qkv)
    return o, lse


def _rms(x, g):
    return x * lax.rsqrt(jnp.mean(x * x, axis=-1, keepdims=True) + EPS) * g


def _merge_tile(f_ref, o_refs, l_refs, gf_ref, ga_ref, y_ref, ya_ref, on_ref, ln_ref, tmp_ref, n_heads):
    tm = y_ref.shape[0]
    lses = []
    for bi, (o_ref, l_ref) in enumerate(zip(o_refs, l_refs)):
        d = o_ref.shape[0]
        blocks = l_ref.shape[2] // LANES
        if d == 1:
            parts = [l_ref[0, :, g * LANES:(g + 1) * LANES] for g in range(blocks)]
        else:
            def to_positions(src, dst, tmp):
                if d <= SLAB_STRIDE:
                    for r in range(d):
                        dst[pl.ds(r, tm // d, stride=d), :] = src(r)
                    return
                f = d // SLAB_STRIDE
                for a in range(SLAB_STRIDE):
                    for b in range(f):
                        tmp[a, pl.ds(b, tm // d, stride=f), :] = src(SLAB_STRIDE * b + a)
                    dst[pl.ds(a, tm // SLAB_STRIDE, stride=SLAB_STRIDE), :] = tmp[a]

            for h in range(n_heads):
                cols = slice(h * HEAD_DIM, (h + 1) * HEAD_DIM)
                to_positions(lambda r: o_ref[r, :, cols].astype(F32), on_ref.at[bi, h], tmp_ref.at[h])
            for g in range(blocks):
                cols = slice(g * LANES, (g + 1) * LANES)
                to_positions(lambda r: l_ref[r, :, cols], ln_ref.at[bi, g], tmp_ref.at[n_heads + g])
            parts = [ln_ref[bi, g] for g in range(blocks)]
        lses.append(functools.reduce(lambda a, b: a + b, parts))

    top = functools.reduce(jnp.maximum, lses)
    ws = [jnp.exp2(l - top) for l in lses]
    inv = 1.0 / functools.reduce(lambda a, b: a + b, ws)
    ws = [w * inv for w in ws]
    for h in range(n_heads):
        cols = slice(h * HEAD_DIM, (h + 1) * HEAD_DIM)
        acc = None
        for bi, o_ref in enumerate(o_refs):
            o = o_ref[0, :, cols].astype(F32) if o_ref.shape[0] == 1 else on_ref[bi, h]
            term = o * ws[bi][:, h:h + 1]
            acc = term if acc is None else acc + term
        ya_ref[:, cols] = acc
    y_ref[:, :D_FOURIER] = _rms(f_ref[...], gf_ref[...]).astype(y_ref.dtype)
    y_ref[:, D_FOURIER:] = _rms(ya_ref[...], ga_ref[...]).astype(y_ref.dtype)


def _merge_kernel(*refs, n_heads, group_tiles):
    n_br = len(DILATED_BRANCHES)
    per_group = 1 + 2 * n_br
    n_groups = len(group_tiles)
    gf_ref, ga_ref, y_ref, ya_ref, on_ref, ln_ref, tmp_ref = refs[n_groups * per_group:]
    i = pl.program_id(0)
    tile0 = 0
    for gi, tiles in enumerate(group_tiles):
        grp = refs[gi * per_group:(gi + 1) * per_group]

        @pl.when(jnp.logical_and(i >= tile0, i < tile0 + tiles))
        def _(grp=grp):
            _merge_tile(grp[0], grp[1:1 + n_br], grp[1 + n_br:], gf_ref, ga_ref, y_ref, ya_ref,
                        on_ref, ln_ref, tmp_ref, n_heads)

        tile0 += tiles


def merge_and_norm(group_inputs, g_out_f, g_out_a, *, n_heads, tm=256):
    d_attn = n_heads * HEAD_DIM
    group_tiles = [g[0].shape[0] // tm for g in group_inputs]
    n_tiles = sum(group_tiles)
    n_br = len(DILATED_BRANCHES)
    in_specs, args = [], []
    tile0 = 0
    for (f, os_, ls_), tiles in zip(group_inputs, group_tiles):
        local = lambda i, tile0=tile0, tiles=tiles: jnp.clip(i - tile0, 0, tiles - 1)
        in_specs.append(pl.BlockSpec((tm, D_FOURIER), lambda i, local=local: (local(i), 0)))
        for arr in (*os_, *ls_):
            d = arr.shape[0]
            in_specs.append(pl.BlockSpec((d, tm // d, arr.shape[2]),
                                         lambda i, local=local: (0, local(i), 0)))
        args += [f, *os_, *ls_]
        tile0 += tiles
    fixed = lambda i: (0, 0)
    in_specs += [pl.BlockSpec((1, D_FOURIER), fixed), pl.BlockSpec((1, d_attn), fixed)]
    max_groups = max(l.shape[2] // LANES for g in group_inputs for l in g[2])
    return pl.pallas_call(
        functools.partial(_merge_kernel, n_heads=n_heads, group_tiles=tuple(group_tiles)),
        grid=(n_tiles,),
        in_specs=in_specs,
        out_specs=pl.BlockSpec((tm, D_FOURIER + d_attn), lambda i: (i, 0)),
        out_shape=jax.ShapeDtypeStruct((n_tiles * tm, D_FOURIER + d_attn), BF16),
        scratch_shapes=[pltpu.VMEM((tm, d_attn), F32),
                        pltpu.VMEM((n_br, n_heads, tm, HEAD_DIM), F32),
                        pltpu.VMEM((n_br, max_groups, tm, LANES), F32),
                        pltpu.VMEM((n_heads + max_groups, SLAB_STRIDE, tm // SLAB_STRIDE, LANES), F32)],
        compiler_params=_params(1),
        name="merge_and_norm",
    )(*args, g_out_f, g_out_a)


def _dispatch_kernel(ends_ref, used_ref, pos_ref, h_ref, o_ref, zero_ref, sem, zero_sem,
                     *, tm, n_experts, n_tiles):
    i = pl.program_id(0)

    @pl.when(i == 0)
    def _():
        zero_ref[...] = jnp.zeros_like(zero_ref)

        def zero_tile(start):
            return pltpu.make_async_copy(zero_ref, o_ref.at[pl.ds(start, tm)], zero_sem)

        def expert_tiles(act):
            for e in range(n_experts):
                begin = ends_ref[e - 1] if e else 0

                @pl.when(ends_ref[e] > begin)
                def _(e=e):
                    act(zero_tile(ends_ref[e] - tm))

        def tail_tiles(act):
            def body(j, carry):
                act(zero_tile(j * tm))
                return carry
            lax.fori_loop(used_ref[0], n_tiles, body, 0)

        expert_tiles(lambda c: c.start())
        tail_tiles(lambda c: c.start())
        expert_tiles(lambda c: c.wait())
        tail_tiles(lambda c: c.wait())

    def copy(s):
        tok = lax.rem(s, tm)
        return pltpu.make_async_copy(h_ref.at[tok], o_ref.at[pos_ref[0, s]], sem)

    def start(s, carry):
        copy(s).start()
        return carry

    lax.fori_loop(0, TOP_K * tm, start, 0, unroll=8)
    for _ in range(TOP_K):
        pltpu.make_async_copy(h_ref, o_ref.at[pl.ds(0, tm)], sem).wait()


def dispatch_rows(h3, pos_tiles, ends, used, *, n_rows, tm):
    t = h3.shape[0]
    n_experts = ends.shape[0]
    return pl.pallas_call(
        functools.partial(_dispatch_kernel, tm=tm, n_experts=n_experts, n_tiles=n_rows // tm),
        grid_spec=pltpu.PrefetchScalarGridSpec(
            num_scalar_prefetch=2, grid=(t // tm,),
            in_specs=[pl.BlockSpec((None, 1, TOP_K * tm), lambda i, ends, used: (i, 0, 0),
                                   memory_space=pltpu.SMEM),
                      pl.BlockSpec((tm,) + h3.shape[1:], lambda i, ends, used: (i, 0, 0))],
            out_specs=pl.BlockSpec(memory_space=pl.ANY),
            scratch_shapes=[pltpu.VMEM((tm,) + h3.shape[1:], F32), pltpu.SemaphoreType.DMA(()),
                            pltpu.SemaphoreType.DMA(())]),
        out_shape=jax.ShapeDtypeStruct((n_rows,) + h3.shape[1:], F32),
        compiler_params=_params(1),
        name="moe_dispatch",
    )(ends, used, pos_tiles, h3)


SLAB_STRIDE = 4


def _rows_to_matrix_kernel(x_ref, o_ref, stage_ref):
    tm = o_ref.shape[0]
    per = x_ref.shape[0] // SLAB_STRIDE
    for a in range(SLAB_STRIDE):
        stage_ref[a] = x_ref[pl.ds(a, per, stride=SLAB_STRIDE), :]
    for a in range(SLAB_STRIDE):
        for b in range(per // tm):
            c = SLAB_STRIDE * b + a
            o_ref[:, c * LANES:(c + 1) * LANES] = (
                stage_ref[a, pl.ds(b, tm, stride=per // tm), :].astype(o_ref.dtype))


def rows_to_matrix(x3, *, tm):
    p, chunks, _ = x3.shape
    assert chunks % SLAB_STRIDE == 0
    return pl.pallas_call(
        _rows_to_matrix_kernel,
        grid=(p // tm,),
        in_specs=[pl.BlockSpec((tm * chunks, LANES), lambda i: (i, 0))],
        out_specs=pl.BlockSpec((tm, chunks * LANES), lambda i: (i, 0)),
        out_shape=jax.ShapeDtypeStruct((p, chunks * LANES), BF16),
        scratch_shapes=[pltpu.VMEM((SLAB_STRIDE, tm * chunks // SLAB_STRIDE, LANES), F32)],
        compiler_params=_params(1),
        name="moe_rows_to_matrix",
    )(x3.reshape(p * chunks, LANES))


def _combine_kernel(rows_ref, pos_ref, o_ref, route_ref, x_ref, gate_ref, *rest, group_tiles):
    del rows_ref
    buf_ref, sem = rest[-2:]
    tm = x_ref.shape[0]

    def copy(r):
        return pltpu.make_async_copy(o_ref.at[pl.ds(pos_ref[0, r], 1)],
                                     buf_ref.at[pl.ds(r, 1)], sem)

    def start(r, carry):
        copy(r).start()
        return carry

    lax.fori_loop(0, TOP_K * tm, start, 0, unroll=8)
    pltpu.make_async_copy(o_ref.at[pl.ds(0, TOP_K * tm)], buf_ref, sem).wait()
    route = route_ref[...]
    y = route[:, 2:3] * buf_ref[pl.ds(0, tm), :] + route[:, 3:4] * buf_ref[pl.ds(tm, tm), :]
    res = x_ref[...] + gate_ref[...] * y
    if group_tiles is None:
        rest[0][...] = res
        return
    normed = _rms(res, rest[0][...])
    i = pl.program_id(0)
    tile0 = 0
    for out_ref, tiles in zip(rest[1:-2], group_tiles):
        @pl.when(jnp.logical_and(i >= tile0, i < tile0 + tiles))
        def _(out_ref=out_ref):
            out_ref[...] = normed

        tile0 += tiles


def moe_combine(o_sorted, pos, route, x, mod4, gate_part, rows, *, tm, final=None):
    t, d = x.shape
    in_specs = [pl.BlockSpec((None, 1, TOP_K * tm), lambda i, rows: (i, 0, 0), memory_space=pltpu.SMEM),
                pl.BlockSpec(memory_space=pl.ANY),
                pl.BlockSpec((tm, LANES), lambda i, rows: (i, 0)),
                pl.BlockSpec((tm, d), lambda i, rows: (i, 0)),
                pl.BlockSpec((None, None, 1, d), lambda i, rows: (rows[i], gate_part, 0, 0))]
    args = [rows, pos, o_sorted, route, x, mod4]
    if final is None:
        group_tiles = None
        out_specs = pl.BlockSpec((tm, d), lambda i, rows: (i, 0))
        out_shape = jax.ShapeDtypeStruct((t, d), F32)
    else:
        g_final, group_tokens = final
        group_tiles = tuple(n // tm for n in group_tokens)
        in_specs.append(pl.BlockSpec((1, d), lambda i, rows: (0, 0)))
        args.append(g_final)
        out_specs, out_shape, tile0 = [], [], 0
        for tiles in group_tiles:
            out_specs.append(pl.BlockSpec(
                (tm, d), lambda i, rows, tile0=tile0, tiles=tiles: (jnp.clip(i - tile0, 0, tiles - 1), 0)))
            out_shape.append(jax.ShapeDtypeStruct((tiles * tm, d), F32))
            tile0 += tiles
    return pl.pallas_call(
        functools.partial(_combine_kernel, group_tiles=group_tiles),
        grid_spec=pltpu.PrefetchScalarGridSpec(
            num_scalar_prefetch=1, grid=(t // tm,), in_specs=in_specs, out_specs=out_specs,
            scratch_shapes=[pltpu.VMEM((TOP_K * tm, d), F32), pltpu.SemaphoreType.DMA(())]),
        out_shape=out_shape,
        compiler_params=_params(1),
        name="moe_combine",
    )(*args)


def moe_dispatch_plan(route, counts, tm):
    t = route.shape[0]
    n_experts = counts.shape[0]
    p = t * TOP_K + n_experts * tm
    expert = route[:, :TOP_K].astype(jnp.int32)
    rank = route[:, 4:4 + TOP_K].astype(jnp.int32)
    padded = ((counts + tm - 1) // tm) * tm
    ends = jnp.cumsum(padded).astype(jnp.int32)
    starts = ends - padded
    is_expert = expert[:, :, None] == jnp.arange(n_experts)[None, None, :]
    pos = jnp.sum(jnp.where(is_expert, starts[None, None, :], 0), axis=-1) + rank
    tile_start = jnp.arange(p // tm, dtype=jnp.int32) * tm
    tile_expert = jnp.sum((tile_start[:, None] >= ends[None, :]).astype(jnp.int32), axis=1)
    tile_expert = jnp.minimum(tile_expert, n_experts - 1)
    used = (ends[-1:] // tm).astype(jnp.int32)
    valid = jnp.clip((starts + counts)[tile_expert] - tile_start, 0, tm).astype(jnp.int32)
    return p, pos, tile_expert, used, ends, valid


def _final_norm_kernel(x_ref, g_ref, o_ref):
    o_ref[...] = _rms(x_ref[...], g_ref[...])


def final_norm(x, g, *, row0, n_rows, tm=512):
    d = x.shape[1]
    blk0 = row0 // tm
    return pl.pallas_call(
        _final_norm_kernel,
        grid=(n_rows // tm,),
        in_specs=[pl.BlockSpec((tm, d), lambda i: (blk0 + i, 0)),
                  pl.BlockSpec((1, d), lambda i: (0, 0))],
        out_specs=pl.BlockSpec((tm, d), lambda i: (i, 0)),
        out_shape=jax.ShapeDtypeStruct((n_rows, d), F32),
        compiler_params=_params(1),
        name="final_norm",
    )(x, g)


TM = 512
TM_BIG = 1024
TM_COMBINE = 256


def _tile_rows(groups, tm):
    rows = []
    base = 0
    for n_rows, seq in groups:
        for b in range(n_rows):
            rows += [base + b] * (seq // tm)
        base += n_rows
    return jnp.asarray(np.asarray(rows, np.int32))


def kernel(x_prompt, x_sample, c_prompt, c_sample, w_ada, b_ada, g_norm_mix, g_norm_ff, w_in, w_fmix,
           g_out_f, g_out_a, w_out, w_ff1, w_ff3, w_ff2, w_router, w_e1, w_e3, w_e2, g_final):
    depth, d, d_in = w_in.shape
    groups = [(x_prompt.shape[0], x_prompt.shape[1]), (x_sample.shape[0], x_sample.shape[1])]
    group_row0 = [0, groups[0][0] * groups[0][1]]
    t = sum(b * s for b, s in groups)
    n_req = sum(b for b, _ in groups)
    d_attn = d - D_FOURIER
    n_heads = d_attn // HEAD_DIM
    n_experts = w_router.shape[-1]
    d_ff_e = w_e1.shape[-1]
    dilations = [dil for _, dil in DILATED_BRANCHES]

    x = None
    c = jnp.concatenate([c_prompt, c_sample], axis=0)
    c = jnp.pad(c, ((0, -n_req % 8), (0, 0)))
    mod = ada_modulation(c, w_ada, b_ada)
    rows = _tile_rows(groups, TM)
    rows_c = _tile_rows(groups, TM_COMBINE)
    def dense_tiling(tm):
        tiles = t // tm
        return Tiling(jnp.zeros((tiles,), jnp.int32), _tile_rows(groups, tm),
                      jnp.full((1,), tiles, jnp.int32), jnp.full((tiles,), tm, jnp.int32))

    dense, dense_big = dense_tiling(TM), dense_tiling(TM_BIG)
    slopes = jnp.exp2(-8.0 * (jnp.arange(n_heads, dtype=F32) + 1.0) / n_heads)
    ab = fourier_ab(w_fmix)
    dft = [dft_matrix(seq) for _, seq in groups]

    for l in range(depth):
        mod4 = mod[l].reshape(mod.shape[1], 6, 1, d)
        layer = lambda e, l=l: l
        if x is None:
            h, x = norm_modulate_groups([x_prompt.reshape(-1, d), x_sample.reshape(-1, d)], rows,
                                        g_norm_mix[l:l + 1], mod4, 0, tm=TM)
        else:
            h = norm_modulate(x, rows, g_norm_mix[l:l + 1], mod4, 0, tm=TM)
        u = matmul(h, w_in, dense_big, w_index=layer, tm=TM_BIG, tn=D_FOURIER, out_dtype=BF16,
                   n=D_FOURIER)
        qkvs = matmul_qkv(h, w_in, dense_big, w_index=layer, tm=TM_BIG, tn=512, col0=D_FOURIER,
                          n=3 * d_attn, dilations=dilations, q_cols=d_attn,
                          q_scale=HEAD_DIM ** -0.5 * LOG2_E)

        group_inputs = []
        for gi, ((n_rows, seq), row0) in enumerate(zip(groups, group_row0)):
            pq = fourier_channel_stage(u, ab, l, n_rows=n_rows, seq=seq, row_block0=row0 // TM, tm=TM)
            f = fourier_sequence_stage(dft[gi], pq, n_rows=n_rows, seq=seq, tm=TM)
            branches = [attention_branch(qkv, slopes, window=window, dilation=dilation, n_rows=n_rows,
                                         seq=seq, row0=row0, n_heads=n_heads)
                        for qkv, (window, dilation) in zip(qkvs, DILATED_BRANCHES)]
            group_inputs.append((f, [b[0] for b in branches], [b[1] for b in branches]))
        y = merge_and_norm(group_inputs, g_out_f[l:l + 1], g_out_a[l:l + 1], n_heads=n_heads)
        x = matmul_gated_residual(y, w_out, x, mod4, 2, dense_big, w_index=layer, tm=TM_BIG, tn=1024)

        j = l // 2
        if l % 2 == 0:
            h = norm_modulate(x, rows, g_norm_ff[l:l + 1], mod4, 3, tm=TM)
            act = matmul_swiglu(h, w_ff1, w_ff3, dense_big, w_index=lambda e, j=j: j, tm=TM_BIG, tn=512)
            x = matmul_gated_residual(act, w_ff2, x, mod4, 5, dense, w_index=lambda e, j=j: j,
                                      tm=TM, tn=512)
        else:
            h, route, counts = norm_modulate(x, rows, g_norm_ff[l:l + 1], mod4, 3, out_dtype=F32,
                                             w_router=w_router[j], tm=TM)
            n_sorted, pos, tile_expert, used, ends, valid = moe_dispatch_plan(route, counts, TM)
            routed = Tiling(tile_expert, tile_expert, used, valid)

            def choice_major(tm):
                tiles = pos.reshape(t // tm, tm, TOP_K).transpose(0, 2, 1)
                return tiles.reshape(t // tm, 1, TOP_K * tm)

            xs = rows_to_matrix(dispatch_rows(h, choice_major(TM), ends, used, n_rows=n_sorted, tm=TM),
                                tm=TM)
            expert_w = lambda e, j=j: j * n_experts + e
            act = matmul_swiglu(xs, w_e1.reshape(-1, d, d_ff_e), w_e3.reshape(-1, d, d_ff_e), routed,
                                w_index=expert_w, tm=TM, tn=1024)
            o_sorted = matmul(act, w_e2.reshape(-1, d_ff_e, d), routed, w_index=expert_w,
                              tm=TM, tn=512, out_dtype=F32, vmem=VMEM_LIMIT_BIG)
            final = (g_final.reshape(1, d), [b * s for b, s in groups]) if l == depth - 1 else None
            x = moe_combine(o_sorted, choice_major(TM_COMBINE), route, x, mod4, 5, rows_c, tm=TM_COMBINE,
                            final=final)

    if isinstance(x, (list, tuple)):
        return tuple(xg.reshape(b, s, d) for xg, (b, s) in zip(x, groups))
    g = g_final.reshape(1, d)
    outs = []
    for (n_rows, seq), row0 in zip(groups, group_row0):
        outs.append(final_norm(x, g, row0=row0, n_rows=n_rows * seq, tm=TM).reshape(n_rows, seq, d))
    return tuple(outs)
```

```python
import functools
import math

import numpy as np
import jax
import jax.numpy as jnp
from jax import lax
from jax.experimental import pallas as pl
from jax.experimental.pallas import tpu as pltpu

HEAD_DIM = 128
FOURIER_GROUP = 128
N_FOURIER_GROUPS = 4
D_FOURIER = FOURIER_GROUP * N_FOURIER_GROUPS
DILATED_BRANCHES = ((128, 1), (512, 4), (2048, 16))
TOP_K = 2
EPS = 1e-6
NEG_INF = -1e30
LOG2_E = math.log2(math.e)
LANES = 128
VMEM_LIMIT = 56 * 1024 * 1024
VMEM_LIMIT_BIG = 62 * 1024 * 1024
ATTN_BLOCK_BYTES = 4 * 1024 * 1024

BF16 = jnp.bfloat16
F32 = jnp.float32


def _params(n_axes, vmem=VMEM_LIMIT):
    return pltpu.CompilerParams(
        dimension_semantics=("arbitrary",) * n_axes, vmem_limit_bytes=vmem)


def _ada_kernel(c_ref, w_ref, b_ref, o_ref):
    c = c_ref[...]
    a = (c * jax.nn.sigmoid(c)).astype(BF16)
    acc = jnp.dot(a, w_ref[...].astype(BF16), preferred_element_type=F32)
    o_ref[...] = acc + b_ref[...]


def ada_modulation(c, w_ada, b_ada, tn=1024):
    n_layers, d, n = w_ada.shape
    r8 = c.shape[0]
    return pl.pallas_call(
        _ada_kernel,
        grid=(n_layers, n // tn),
        in_specs=[
            pl.BlockSpec((r8, d), lambda l, j: (0, 0)),
            pl.BlockSpec((None, d, tn), lambda l, j: (l, 0, j)),
            pl.BlockSpec((None, 1, tn), lambda l, j: (l, 0, j)),
        ],
        out_specs=pl.BlockSpec((None, r8, tn), lambda l, j: (l, 0, j)),
        out_shape=jax.ShapeDtypeStruct((n_layers, r8, n), F32),
        compiler_params=_params(2),
        name="ada_modulation",
    )(c, w_ada, b_ada.reshape(n_layers, 1, n))


def _norm_mod(x, g, sc, sh):
    y = x * lax.rsqrt(jnp.mean(x * x, axis=-1, keepdims=True) + EPS)
    return (y * g) * (1.0 + sc) + sh


def _norm_mod_kernel(rows_ref, x_ref, g_ref, sc_ref, sh_ref, h_ref):
    del rows_ref
    h_ref[...] = _norm_mod(x_ref[...], g_ref[...], sc_ref[...], sh_ref[...]).astype(h_ref.dtype)


def _norm_mod_router_kernel(rows_ref, x_ref, g_ref, sc_ref, sh_ref, wr_ref, h_ref, route_ref,
                            totals_ref, stage_ref, counts_ref, earlier_ref, *, n_experts):
    del rows_ref
    h = _norm_mod(x_ref[...], g_ref[...], sc_ref[...], sh_ref[...])
    tm = h.shape[0]
    per = h_ref.shape[0] // SLAB_STRIDE
    for a in range(SLAB_STRIDE):
        for b in range(per // tm):
            c = SLAB_STRIDE * b + a
            stage_ref[a, pl.ds(b, tm, stride=per // tm), :] = h[:, c * LANES:(c + 1) * LANES]
    for a in range(SLAB_STRIDE):
        h_ref[pl.ds(a, per, stride=SLAB_STRIDE), :] = stage_ref[a]
    w = wr_ref[...]
    h_hi, w_hi = h.astype(BF16), w.astype(BF16)
    h_lo = (h - h_hi.astype(F32)).astype(BF16)
    w_lo = (w - w_hi.astype(F32)).astype(BF16)
    logits = (jnp.dot(h_hi, w_hi, preferred_element_type=F32)
              + (jnp.dot(h_hi, w_lo, preferred_element_type=F32)
                 + jnp.dot(h_lo, w_hi, preferred_element_type=F32)))
    lane = lax.broadcasted_iota(jnp.int32, logits.shape, 1)
    logits = jnp.where(lane < n_experts, logits, -jnp.inf)
    m1 = jnp.max(logits, axis=-1, keepdims=True)
    i1 = jnp.min(jnp.where(logits == m1, lane, LANES), axis=-1, keepdims=True)
    rest = jnp.where(lane == i1, -jnp.inf, logits)
    m2 = jnp.max(rest, axis=-1, keepdims=True)
    i2 = jnp.min(jnp.where(rest == m2, lane, LANES), axis=-1, keepdims=True)
    e2 = jnp.exp(m2 - m1)
    den = 1.0 + e2
    @pl.when(pl.program_id(0) == 0)
    def _():
        counts_ref[...] = jnp.zeros_like(counts_ref)
        earlier_ref[...] = (lax.broadcasted_iota(jnp.int32, earlier_ref.shape, 1)
                            < lax.broadcasted_iota(jnp.int32, earlier_ref.shape, 0)).astype(BF16)

    pick1, pick2 = lane == i1, lane == i2
    picked = jnp.logical_or(pick1, pick2)
    before = counts_ref[0:1, :] + jnp.dot(earlier_ref[...], picked.astype(BF16),
                                          preferred_element_type=F32)
    rank1 = jnp.sum(jnp.where(pick1, before, 0.0), axis=-1, keepdims=True)
    rank2 = jnp.sum(jnp.where(pick2, before, 0.0), axis=-1, keepdims=True)
    counts_ref[...] = counts_ref[...] + jnp.sum(picked.astype(F32), axis=0, keepdims=True)
    totals_ref[...] = counts_ref[...]

    route = jnp.where(lane == 0, i1.astype(F32), 0.0)
    route = jnp.where(lane == 1, i2.astype(F32), route)
    route = jnp.where(lane == 2, 1.0 / den, route)
    route = jnp.where(lane == 3, e2 / den, route)
    route = jnp.where(lane == 4, rank1, route)
    route = jnp.where(lane == 5, rank2, route)
    route_ref[...] = route


def _norm_mod_groups_kernel(rows_ref, *refs, group_tiles):
    del rows_ref
    n_groups = len(group_tiles)
    g_ref, sc_ref, sh_ref, h_ref, x_ref = refs[n_groups:]
    i = pl.program_id(0)
    tile0 = 0
    for xg_ref, tiles in zip(refs[:n_groups], group_tiles):
        @pl.when(jnp.logical_and(i >= tile0, i < tile0 + tiles))
        def _(xg_ref=xg_ref):
            x = xg_ref[...]
            x_ref[...] = x
            h_ref[...] = _norm_mod(x, g_ref[...], sc_ref[...], sh_ref[...]).astype(h_ref.dtype)

        tile0 += tiles


def norm_modulate_groups(x_groups, rows, g, mod4, layer_part, *, tm=512):
    d = x_groups[0].shape[1]
    group_tiles = tuple(xg.shape[0] // tm for xg in x_groups)
    n_tiles = sum(group_tiles)
    sh_part, sc_part = layer_part, layer_part + 1
    in_specs, tile0 = [], 0
    for tiles in group_tiles:
        in_specs.append(pl.BlockSpec(
            (tm, d), lambda i, rows, tile0=tile0, tiles=tiles: (jnp.clip(i - tile0, 0, tiles - 1), 0)))
        tile0 += tiles
    in_specs += [pl.BlockSpec((1, d), lambda i, rows: (0, 0)),
                 pl.BlockSpec((None, None, 1, d), lambda i, rows: (rows[i], sc_part, 0, 0)),
                 pl.BlockSpec((None, None, 1, d), lambda i, rows: (rows[i], sh_part, 0, 0))]
    row = lambda i, rows: (i, 0)
    return pl.pallas_call(
        functools.partial(_norm_mod_groups_kernel, group_tiles=group_tiles),
        grid_spec=pltpu.PrefetchScalarGridSpec(
            num_scalar_prefetch=1, grid=(n_tiles,), in_specs=in_specs,
            out_specs=[pl.BlockSpec((tm, d), row), pl.BlockSpec((tm, d), row)]),
        out_shape=[jax.ShapeDtypeStruct((n_tiles * tm, d), BF16),
                   jax.ShapeDtypeStruct((n_tiles * tm, d), F32)],
        compiler_params=_params(1),
        name="norm_modulate_groups",
    )(rows, *x_groups, g, mod4, mod4)


def norm_modulate(x, rows, g, mod4, layer_part, *, out_dtype=BF16, w_router=None, tm=512):
    t, d = x.shape
    sh_part, sc_part = layer_part, layer_part + 1
    in_specs = [
        pl.BlockSpec((tm, d), lambda i, rows: (i, 0)),
        pl.BlockSpec((1, d), lambda i, rows: (0, 0)),
        pl.BlockSpec((None, None, 1, d), lambda i, rows: (rows[i], sc_part, 0, 0)),
        pl.BlockSpec((None, None, 1, d), lambda i, rows: (rows[i], sh_part, 0, 0)),
    ]
    h_spec = pl.BlockSpec((tm, d), lambda i, rows: (i, 0))
    h_shape = jax.ShapeDtypeStruct((t, d), out_dtype)
    if w_router is None:
        return pl.pallas_call(
            _norm_mod_kernel,
            grid_spec=pltpu.PrefetchScalarGridSpec(
                num_scalar_prefetch=1, grid=(t // tm,), in_specs=in_specs, out_specs=h_spec),
            out_shape=h_shape,
            compiler_params=_params(1),
            name="norm_modulate",
        )(rows, x, g, mod4, mod4)
    n_experts = w_router.shape[1]
    wr = jnp.pad(w_router, ((0, 0), (0, LANES - n_experts)))
    in_specs.append(pl.BlockSpec((d, LANES), lambda i, rows: (0, 0)))
    chunks = d // LANES
    assert chunks % SLAB_STRIDE == 0
    h_flat, route, totals = pl.pallas_call(
        functools.partial(_norm_mod_router_kernel, n_experts=n_experts),
        grid_spec=pltpu.PrefetchScalarGridSpec(
            num_scalar_prefetch=1, grid=(t // tm,), in_specs=in_specs,
            out_specs=[pl.BlockSpec((tm * chunks, LANES), lambda i, rows: (i, 0)),
                       pl.BlockSpec((tm, LANES), lambda i, rows: (i, 0)),
                       pl.BlockSpec((8, LANES), lambda i, rows: (0, 0))],
            scratch_shapes=[pltpu.VMEM((SLAB_STRIDE, tm * chunks // SLAB_STRIDE, LANES), F32),
                            pltpu.VMEM((8, LANES), F32), pltpu.VMEM((tm, tm), BF16)]),
        out_shape=[jax.ShapeDtypeStruct((t * chunks, LANES), F32),
                   jax.ShapeDtypeStruct((t, LANES), F32),
                   jax.ShapeDtypeStruct((8, LANES), F32)],
        compiler_params=_params(1),
        name="norm_modulate_router",
    )(rows, x, g, mod4, mod4, wr)
    return h_flat.reshape(t, chunks, LANES), route, totals[0, :n_experts].astype(jnp.int32)


CAST_ROWS = 512


def _cast_weights_if_changed(te_ref, w_refs, wb_refs):
    i = pl.program_id(1)
    changed = jnp.logical_or(i == 0, te_ref[i] != te_ref[jnp.maximum(i - 1, 0)])

    @pl.when(changed)
    def _():
        k = w_refs[0].shape[0]
        step = math.gcd(k, CAST_ROWS)

        def body(c, carry):
            r0 = pl.multiple_of(c * step, step)
            for w_ref, wb_ref in zip(w_refs, wb_refs):
                wb_ref[pl.ds(r0, step), :] = w_ref[pl.ds(r0, step), :].astype(BF16)
            return carry

        lax.fori_loop(0, k // step, body, 0)


ROW_PARTS = 4


def _for_valid_rows(valid_ref, o_ref, compute):
    tm = o_ref.shape[0]
    quantum = tm // ROW_PARTS
    quanta = (valid_ref[pl.program_id(1)] + quantum - 1) // quantum
    for n in range(ROW_PARTS + 1):
        rows = n * quantum

        @pl.when(quanta == n)
        def _(rows=rows):
            if rows:
                o_ref[:rows, :] = compute(rows).astype(o_ref.dtype)
            if rows < tm:
                o_ref[rows:, :] = jnp.zeros((tm - rows, o_ref.shape[1]), o_ref.dtype)


def _mm_plain_kernel(te_ref, rows_ref, used_ref, valid_ref, a_ref, w_ref, o_ref, wb_ref):
    del rows_ref, used_ref
    _cast_weights_if_changed(te_ref, (w_ref,), (wb_ref,))
    _for_valid_rows(valid_ref, o_ref, lambda rows: jnp.dot(
        a_ref[:rows, :], wb_ref[...], preferred_element_type=F32))


def _mm_qkv_kernel(te_ref, rows_ref, used_ref, valid_ref, a_ref, w_ref, *rest, dilations, q_chunks,
                   q_scale):
    del rows_ref, used_ref, valid_ref
    out_refs = rest[:len(dilations)]
    wb_ref, acc_ref, stage_ref = rest[len(dilations):]
    _cast_weights_if_changed(te_ref, (w_ref,), (wb_ref,))
    acc = jnp.dot(a_ref[...], wb_ref[...], preferred_element_type=F32)
    acc = acc * jnp.where(pl.program_id(0) < q_chunks, q_scale, 1.0).astype(F32)
    tm, tn = acc.shape
    slabs = tn // LANES
    for c in range(slabs):
        acc_ref[c] = acc[:, c * LANES:(c + 1) * LANES]
    cur_ref, cur_d = acc_ref, 1
    for idx, (o_ref, d) in enumerate(zip(out_refs, dilations)):
        if d == 1:
            o_ref[0] = acc.astype(o_ref.dtype)
            continue
        f = d // cur_d
        keep = idx + 1 < len(dilations)
        for rp in range(cur_d):
            for b in range(f):
                r = b * cur_d + rp
                for c in range(slabs):
                    piece = cur_ref[c, pl.ds(rp * (tm // cur_d) + b, tm // d, stride=f), :]
                    o_ref[r, :, c * LANES:(c + 1) * LANES] = piece.astype(o_ref.dtype)
                    if keep:
                        stage_ref[c, pl.ds(r * (tm // d), tm // d), :] = piece
        cur_ref, cur_d = stage_ref, d


def _mm_swiglu_kernel(te_ref, rows_ref, used_ref, valid_ref, a_ref, w1_ref, w3_ref, o_ref, wb1_ref,
                      wb3_ref):
    del rows_ref, used_ref
    _cast_weights_if_changed(te_ref, (w1_ref, w3_ref), (wb1_ref, wb3_ref))

    def compute(rows):
        a = a_ref[:rows, :]
        g = jnp.dot(a, wb1_ref[...], preferred_element_type=F32)
        u = jnp.dot(a, wb3_ref[...], preferred_element_type=F32)
        return (g * jax.nn.sigmoid(g)) * u

    _for_valid_rows(valid_ref, o_ref, compute)


def _mm_resid_kernel(te_ref, rows_ref, used_ref, valid_ref, a_ref, w_ref, x_ref, gate_ref, o_ref,
                     wb_ref):
    del rows_ref, used_ref, valid_ref
    _cast_weights_if_changed(te_ref, (w_ref,), (wb_ref,))
    acc = jnp.dot(a_ref[...], wb_ref[...], preferred_element_type=F32)
    o_ref[...] = x_ref[...] + gate_ref[...] * acc


def _a_spec(tm, k):
    return pl.BlockSpec((tm, k), lambda j, i, te, rows, used, valid: (jnp.minimum(i, used[0] - 1), 0))


def _w_spec(k, tn, w_index, col_block0=0, single_buffer=False):
    mode = dict(pipeline_mode=pl.Buffered(1)) if single_buffer else {}
    return pl.BlockSpec((None, k, tn),
                        lambda j, i, te, rows, used, valid: (w_index(te[i]), 0, col_block0 + j), **mode)


def _out_spec(tm, tn):
    return pl.BlockSpec((tm, tn), lambda j, i, te, rows, used, valid: (i, j))


class Tiling:
    def __init__(self, tile_expert, rows, used, valid):
        self.args = (tile_expert, rows, used, valid)


def matmul_qkv(a, w, tiling, *, w_index, tm, tn, col0, n, dilations, q_cols, q_scale):
    m, k = a.shape
    assert sum(d > 1 for d in dilations) <= 2
    return pl.pallas_call(
        functools.partial(_mm_qkv_kernel, dilations=tuple(dilations), q_chunks=q_cols // tn,
                          q_scale=q_scale),
        grid_spec=pltpu.PrefetchScalarGridSpec(
            num_scalar_prefetch=4, grid=(n // tn, m // tm),
            in_specs=[_a_spec(tm, k), _w_spec(k, tn, w_index, col0 // tn)],
            out_specs=[pl.BlockSpec((d, tm // d, tn), lambda j, i, te, rows, used, valid: (0, i, j))
                       for d in dilations],
            scratch_shapes=[pltpu.VMEM((k, tn), BF16)]
                           + [pltpu.VMEM((tn // LANES, tm, LANES), F32)] * 2),
        out_shape=[jax.ShapeDtypeStruct((d, m // d, n), BF16) for d in dilations],
        compiler_params=_params(2),
        name="matmul_qkv",
    )(*tiling.args, a, w)


def matmul(a, w, tiling, *, w_index, tm, tn, out_dtype, n=None, vmem=VMEM_LIMIT):
    m, k = a.shape
    n = w.shape[-1] if n is None else n
    return pl.pallas_call(
        _mm_plain_kernel,
        grid_spec=pltpu.PrefetchScalarGridSpec(
            num_scalar_prefetch=4, grid=(n // tn, m // tm),
            in_specs=[_a_spec(tm, k), _w_spec(k, tn, w_index)],
            out_specs=_out_spec(tm, tn),
            scratch_shapes=[pltpu.VMEM((k, tn), BF16)]),
        out_shape=jax.ShapeDtypeStruct((m, n), out_dtype),
        compiler_params=_params(2, vmem),
        name="matmul",
    )(*tiling.args, a, w)


def matmul_swiglu(a, w1, w3, tiling, *, w_index, tm, tn):
    m, k = a.shape
    n = w1.shape[-1]
    return pl.pallas_call(
        _mm_swiglu_kernel,
        grid_spec=pltpu.PrefetchScalarGridSpec(
            num_scalar_prefetch=4, grid=(n // tn, m // tm),
            in_specs=[_a_spec(tm, k), _w_spec(k, tn, w_index), _w_spec(k, tn, w_index)],
            out_specs=_out_spec(tm, tn),
            scratch_shapes=[pltpu.VMEM((k, tn), BF16), pltpu.VMEM((k, tn), BF16)]),
        out_shape=jax.ShapeDtypeStruct((m, n), BF16),
        compiler_params=_params(2),
        name="matmul_swiglu",
    )(*tiling.args, a, w1, w3)


def matmul_gated_residual(a, w, x, mod4, gate_part, tiling, *, w_index, tm, tn):
    m, k = a.shape
    n = w.shape[-1]
    return pl.pallas_call(
        _mm_resid_kernel,
        grid_spec=pltpu.PrefetchScalarGridSpec(
            num_scalar_prefetch=4, grid=(n // tn, m // tm),
            in_specs=[_a_spec(tm, k), _w_spec(k, tn, w_index), _out_spec(tm, tn),
                      pl.BlockSpec((None, None, 1, tn),
                                   lambda j, i, te, rows, used, valid: (rows[i], gate_part, 0, j))],
            out_specs=_out_spec(tm, tn),
            scratch_shapes=[pltpu.VMEM((k, tn), BF16)]),
        out_shape=jax.ShapeDtypeStruct((m, n), F32),
        compiler_params=_params(2),
        name="matmul_gated_residual",
    )(*tiling.args, a, w, x, mod4)


def _dft_cos_sin(n):
    j = lax.broadcasted_iota(jnp.int32, (n, n), 0)
    k = lax.broadcasted_iota(jnp.int32, (n, n), 1)
    ang = ((j * k) % n).astype(F32) * (2.0 * math.pi / n)
    return jnp.cos(ang), jnp.sin(ang)


def _fourier_ab_kernel(c_ref, s_ref, w_ref, o_ref):
    w = w_ref[...]
    scale = FOURIER_GROUP ** -0.5
    a = jnp.dot(c_ref[...], w, preferred_element_type=F32, precision=lax.Precision.HIGHEST)
    b = jnp.dot(s_ref[...], w, preferred_element_type=F32, precision=lax.Precision.HIGHEST)
    o_ref[:, :FOURIER_GROUP] = (a * scale).astype(o_ref.dtype)
    o_ref[:, FOURIER_GROUP:] = (b * scale).astype(o_ref.dtype)


def fourier_ab(w_fmix):
    n_layers, n_groups, c, _ = w_fmix.shape
    cos_g, sin_g = _dft_cos_sin(c)
    return pl.pallas_call(
        _fourier_ab_kernel,
        grid=(n_layers, n_groups),
        in_specs=[pl.BlockSpec((c, c), lambda l, g: (0, 0)),
                  pl.BlockSpec((c, c), lambda l, g: (0, 0)),
                  pl.BlockSpec((None, None, c, c), lambda l, g: (l, g, 0, 0))],
        out_specs=pl.BlockSpec((None, None, c, 2 * c), lambda l, g: (l, g, 0, 0)),
        out_shape=jax.ShapeDtypeStruct((n_layers, n_groups, c, 2 * c), BF16),
        compiler_params=_params(2),
        name="fourier_ab",
    )(cos_g, sin_g, w_fmix)


def _fourier_channel_kernel(u_ref, ab_ref, pq_ref):
    c = FOURIER_GROUP
    for g in range(N_FOURIER_GROUPS):
        pq = jnp.dot(u_ref[:, g * c:(g + 1) * c], ab_ref[g], preferred_element_type=F32)
        pq_ref[0, :, g * c:(g + 1) * c] = pq[:, :c].astype(pq_ref.dtype)
        pq_ref[1, :, g * c:(g + 1) * c] = pq[:, c:].astype(pq_ref.dtype)


def fourier_channel_stage(u, ab, layer, *, n_rows, seq, row_block0, tm=512):
    tiles = seq // tm
    return pl.pallas_call(
        _fourier_channel_kernel,
        grid=(n_rows, tiles),
        in_specs=[pl.BlockSpec((tm, D_FOURIER), lambda b, i: (row_block0 + b * tiles + i, 0)),
                  pl.BlockSpec((None, N_FOURIER_GROUPS, FOURIER_GROUP, 2 * FOURIER_GROUP),
                               lambda b, i: (layer, 0, 0, 0))],
        out_specs=pl.BlockSpec((None, 2, tm, D_FOURIER), lambda b, i: (b, 0, i, 0)),
        out_shape=jax.ShapeDtypeStruct((n_rows, 2, seq, D_FOURIER), BF16),
        compiler_params=_params(2),
        name="fourier_channel_stage",
    )(u, ab)


def _fourier_seq_kernel(cs_ref, pq_ref, o_ref, *, scale):
    o_ref[...] = jnp.dot(cs_ref[...], pq_ref[...], preferred_element_type=F32) * scale


def fourier_sequence_stage(cs, pq, *, n_rows, seq, tm=512):
    tiles = seq // tm
    return pl.pallas_call(
        functools.partial(_fourier_seq_kernel, scale=seq ** -0.5),
        grid=(n_rows, tiles),
        in_specs=[pl.BlockSpec((tm, 2 * seq), lambda b, i: (i, 0)),
                  pl.BlockSpec((None, 2 * seq, D_FOURIER), lambda b, i: (b, 0, 0))],
        out_specs=pl.BlockSpec((tm, D_FOURIER), lambda b, i: (b * tiles + i, 0)),
        out_shape=jax.ShapeDtypeStruct((n_rows * seq, D_FOURIER), F32),
        compiler_params=_params(2),
        name="fourier_sequence_stage",
    )(cs, pq.reshape(n_rows, 2 * seq, D_FOURIER))


DFT_SPLIT = 64


def _dft_kernel(t1_ref, t2_ref, o_ref):
    seq = t2_ref.shape[2]
    c1, s1 = t1_ref[0], t1_ref[1]
    c2, s2 = t2_ref[0], t2_ref[1]
    o_ref[:, :seq] = (c1 * c2 - s1 * s2).astype(o_ref.dtype)
    o_ref[:, seq:] = (-(s1 * c2 + c1 * s2)).astype(o_ref.dtype)


def dft_matrix(seq):
    k = jnp.arange(seq, dtype=jnp.int32)

    def table(mult):
        ang = ((mult[:, None] * k[None, :]) % seq).astype(F32) * (2.0 * math.pi / seq)
        return jnp.stack([jnp.cos(ang), jnp.sin(ang)])

    n_coarse = seq // DFT_SPLIT
    t1 = table(jnp.arange(n_coarse, dtype=jnp.int32) * DFT_SPLIT).reshape(2, n_coarse, 1, seq)
    t2 = table(jnp.arange(DFT_SPLIT, dtype=jnp.int32))
    return pl.pallas_call(
        _dft_kernel,
        grid=(n_coarse,),
        in_specs=[pl.BlockSpec((2, None, 1, seq), lambda a: (0, a, 0, 0)),
                  pl.BlockSpec((2, DFT_SPLIT, seq), lambda a: (0, 0, 0))],
        out_specs=pl.BlockSpec((DFT_SPLIT, 2 * seq), lambda a: (a, 0)),
        out_shape=jax.ShapeDtypeStruct((seq, 2 * seq), BF16),
        compiler_params=_params(1),
        name="dft_matrix",
    )(t1, t2)


def _attn_kernel(slopes_ref, q_ref, k_ref, v_ref, o_ref, lse_ref, bias_ref, *, dilation, n_side,
                 bq, bk, hps, stack, unroll):
    n_res, sub = q_ref.shape[:2]
    n_cases = bias_ref.shape[0]
    blocks = sub // bq if stack == 1 else 1
    lane = lax.broadcasted_iota(jnp.int32, (bq, LANES), 1)

    @pl.when(jnp.logical_and(pl.program_id(1) == 0, pl.program_id(2) == 0))
    def _():
        row = lax.broadcasted_iota(jnp.int32, (bq, bk), 0)
        col = lax.broadcasted_iota(jnp.int32, (bq, bk), 1)
        for case in range(n_cases):
            dist = jnp.abs(col - row - case * n_side)
            valid = dist <= n_side
            if stack > 1:
                valid = jnp.logical_and(valid, row // sub == col // sub)
            dist_f = dist.astype(F32) * float(dilation)
            for hh in range(hps):
                slope = slopes_ref[pl.program_id(0) * hps + hh] * LOG2_E
                bias_ref[case, hh] = jnp.where(valid, -slope * dist_f, NEG_INF)

    def body(it, carry):
        if stack == 1:
            res = it // blocks
            q0 = pl.multiple_of((it % blocks) * bq, bq)
            ks = pl.multiple_of(jnp.clip(q0 - n_side, 0, sub - bk), n_side)
            case = (q0 - ks) // n_side
            q_at = lambda ref, cols: ref[res, pl.ds(q0, bq), cols]
            k_at = lambda ref, cols: ref[res, pl.ds(ks, bk), cols]
        else:
            case = 0
            res = pl.ds(pl.multiple_of(it * stack, stack), stack)
            q_at = k_at = lambda ref, cols: ref[res, :, cols].reshape(stack * sub, HEAD_DIM)
        lse_tile = jnp.zeros((bq, LANES), F32)
        for hh in range(hps):
            cols = slice(hh * HEAD_DIM, (hh + 1) * HEAD_DIM)
            q = q_at(q_ref, cols)
            k = k_at(k_ref, cols)
            v = k_at(v_ref, cols)
            s = lax.dot_general(q, k, (((1,), (1,)), ((), ())), preferred_element_type=F32)
            s = s + bias_ref[case, hh]
            m = jnp.max(s, axis=-1, keepdims=True)
            p = jnp.exp2(s - m)
            den = jnp.sum(p, axis=-1, keepdims=True)
            num = jnp.dot(p.astype(BF16), v, preferred_element_type=F32)
            o = (num / den).astype(o_ref.dtype)
            if stack == 1:
                o_ref[res, pl.ds(q0, bq), cols] = o
            else:
                o_ref[res, :, cols] = o.reshape(stack, sub, HEAD_DIM)
            lse_tile = jnp.where(lane == pl.program_id(0) * hps + hh, m + jnp.log2(den), lse_tile)
        if stack == 1:
            lse_ref[res, pl.ds(q0, bq), :] = lse_tile
        else:
            lse_ref[res] = lse_tile.reshape(stack, sub, LANES)
        return carry

    lax.fori_loop(0, n_res * blocks // stack, body, 0, unroll=unroll)


ATTN_CHAINS = 12
ATTN_KEY_BLOCK = 256


def attention_step_shape(sub, n_heads, dilation, rows_per_trip):
    def largest(n, unit):
        return max(c for c in range(1, n + 1) if n % c == 0 and c * unit <= ATTN_BLOCK_BYTES)

    hps = largest(n_heads, sub * HEAD_DIM * 2)
    n_res = largest(dilation, sub * hps * HEAD_DIM * 2)
    trips = n_res * sub // rows_per_trip
    unroll = max(u for u in range(1, trips + 1) if trips % u == 0 and u * hps <= max(ATTN_CHAINS, hps))
    return hps, n_res, unroll


def attention_branch(qkv, slopes, *, window, dilation, n_rows, seq, row0, n_heads):
    tg = n_rows * seq
    sub = seq // dilation
    n_side = (window // 2) // dilation
    if sub >= ATTN_KEY_BLOCK:
        stack, bq, bk, n_cases = 1, ATTN_KEY_BLOCK - 2 * n_side, ATTN_KEY_BLOCK, 3
    else:
        stack = ATTN_KEY_BLOCK // sub
        bq, bk, n_cases = ATTN_KEY_BLOCK, ATTN_KEY_BLOCK, 1
    hps, n_res, unroll = attention_step_shape(sub, n_heads, dilation, bq)
    assert n_res % stack == 0 and sub % (bq // stack) == 0
    gw = hps * HEAD_DIM
    groups = n_heads // hps
    d_attn = n_heads * HEAD_DIM
    rb0 = row0 // seq

    def in_map(part):
        return lambda g, b, r, slopes: (r, rb0 + b, part * groups + g)

    out_map = lambda g, b, r, slopes: (r, b, g)
    o, lse = pl.pallas_call(
        functools.partial(_attn_kernel, dilation=dilation, n_side=n_side, bq=bq, bk=bk, hps=hps,
                          stack=stack, unroll=unroll),
        grid_spec=pltpu.PrefetchScalarGridSpec(
            num_scalar_prefetch=1, grid=(groups, n_rows, dilation // n_res),
            in_specs=[pl.BlockSpec((n_res, sub, gw), in_map(part)) for part in range(3)],
            out_specs=[pl.BlockSpec((n_res, sub, gw), out_map),
                       pl.BlockSpec((n_res, sub, LANES), out_map)],
            scratch_shapes=[pltpu.VMEM((n_cases, hps, bq, bk), F32)]),
        out_shape=[jax.ShapeDtypeStruct((dilation, tg // dilation, d_attn), BF16),
                   jax.ShapeDtypeStruct((dilation, tg // dilation, groups * LANES), F32)],
        compiler_params=_params(3),
        name=f"attention_d{dilation}",
    )(slopes, qkv, qkv, qkv)
    return o, lse


def _rms(x, g):
    return x * lax.rsqrt(jnp.mean(x * x, axis=-1, keepdims=True) + EPS) * g


def _merge_tile(f_ref, o_refs, l_refs, gf_ref, ga_ref, y_ref, ya_ref, on_ref, ln_ref, tmp_ref, n_heads):
    tm = y_ref.shape[0]
    lses = []
    for bi, (o_ref, l_ref) in enumerate(zip(o_refs, l_refs)):
        d = o_ref.shape[0]
        blocks = l_ref.shape[2] // LANES
        if d == 1:
            parts = [l_ref[0, :, g * LANES:(g + 1) * LANES] for g in range(blocks)]
        else:
            def to_positions(src, dst, tmp):
                if d <= SLAB_STRIDE:
                    for r in range(d):
                        dst[pl.ds(r, tm // d, stride=d), :] = src(r)
                    return
                f = d // SLAB_STRIDE
                for a in range(SLAB_STRIDE):
                    for b in range(f):
                        tmp[a, pl.ds(b, tm // d, stride=f), :] = src(SLAB_STRIDE * b + a)
                    dst[pl.ds(a, tm // SLAB_STRIDE, stride=SLAB_STRIDE), :] = tmp[a]

            for h in range(n_heads):
                cols = slice(h * HEAD_DIM, (h + 1) * HEAD_DIM)
                to_positions(lambda r: o_ref[r, :, cols].astype(F32), on_ref.at[bi, h], tmp_ref.at[h])
            for g in range(blocks):
                cols = slice(g * LANES, (g + 1) * LANES)
                to_positions(lambda r: l_ref[r, :, cols], ln_ref.at[bi, g], tmp_ref.at[n_heads + g])
            parts = [ln_ref[bi, g] for g in range(blocks)]
        lses.append(functools.reduce(lambda a, b: a + b, parts))

    top = functools.reduce(jnp.maximum, lses)
    ws = [jnp.exp2(l - top) for l in lses]
    inv = 1.0 / functools.reduce(lambda a, b: a + b, ws)
    ws = [w * inv for w in ws]
    for h in range(n_heads):
        cols = slice(h * HEAD_DIM, (h + 1) * HEAD_DIM)
        acc = None
        for bi, o_ref in enumerate(o_refs):
            o = o_ref[0, :, cols].astype(F32) if o_ref.shape[0] == 1 else on_ref[bi, h]
            term = o * ws[bi][:, h:h + 1]
            acc = term if acc is None else acc + term
        ya_ref[:, cols] = acc
    y_ref[:, :D_FOURIER] = _rms(f_ref[...], gf_ref[...]).astype(y_ref.dtype)
    y_ref[:, D_FOURIER:] = _rms(ya_ref[...], ga_ref[...]).astype(y_ref.dtype)


def _merge_kernel(*refs, n_heads, group_tiles):
    n_br = len(DILATED_BRANCHES)
    per_group = 1 + 2 * n_br
    n_groups = len(group_tiles)
    gf_ref, ga_ref, y_ref, ya_ref, on_ref, ln_ref, tmp_ref = refs[n_groups * per_group:]
    i = pl.program_id(0)
    tile0 = 0
    for gi, tiles in enumerate(group_tiles):
        grp = refs[gi * per_group:(gi + 1) * per_group]

        @pl.when(jnp.logical_and(i >= tile0, i < tile0 + tiles))
        def _(grp=grp):
            _merge_tile(grp[0], grp[1:1 + n_br], grp[1 + n_br:], gf_ref, ga_ref, y_ref, ya_ref,
                        on_ref, ln_ref, tmp_ref, n_heads)

        tile0 += tiles


def merge_and_norm(group_inputs, g_out_f, g_out_a, *, n_heads, tm=256):
    d_attn = n_heads * HEAD_DIM
    group_tiles = [g[0].shape[0] // tm for g in group_inputs]
    n_tiles = sum(group_tiles)
    n_br = len(DILATED_BRANCHES)
    in_specs, args = [], []
    tile0 = 0
    for (f, os_, ls_), tiles in zip(group_inputs, group_tiles):
        local = lambda i, tile0=tile0, tiles=tiles: jnp.clip(i - tile0, 0, tiles - 1)
        in_specs.append(pl.BlockSpec((tm, D_FOURIER), lambda i, local=local: (local(i), 0)))
        for arr in (*os_, *ls_):
            d = arr.shape[0]
            in_specs.append(pl.BlockSpec((d, tm // d, arr.shape[2]),
                                         lambda i, local=local: (0, local(i), 0)))
        args += [f, *os_, *ls_]
        tile0 += tiles
    fixed = lambda i: (0, 0)
    in_specs += [pl.BlockSpec((1, D_FOURIER), fixed), pl.BlockSpec((1, d_attn), fixed)]
    max_groups = max(l.shape[2] // LANES for g in group_inputs for l in g[2])
    return pl.pallas_call(
        functools.partial(_merge_kernel, n_heads=n_heads, group_tiles=tuple(group_tiles)),
        grid=(n_tiles,),
        in_specs=in_specs,
        out_specs=pl.BlockSpec((tm, D_FOURIER + d_attn), lambda i: (i, 0)),
        out_shape=jax.ShapeDtypeStruct((n_tiles * tm, D_FOURIER + d_attn), BF16),
        scratch_shapes=[pltpu.VMEM((tm, d_attn), F32),
                        pltpu.VMEM((n_br, n_heads, tm, HEAD_DIM), F32),
                        pltpu.VMEM((n_br, max_groups, tm, LANES), F32),
                        pltpu.VMEM((n_heads + max_groups, SLAB_STRIDE, tm // SLAB_STRIDE, LANES), F32)],
        compiler_params=_params(1),
        name="merge_and_norm",
    )(*args, g_out_f, g_out_a)


def _dispatch_kernel(ends_ref, used_ref, pos_ref, h_ref, o_ref, zero_ref, sem, zero_sem,
                     *, tm, n_experts, n_tiles):
    i = pl.program_id(0)

    @pl.when(i == 0)
    def _():
        zero_ref[...] = jnp.zeros_like(zero_ref)

        def zero_tile(start):
            return pltpu.make_async_copy(zero_ref, o_ref.at[pl.ds(start, tm)], zero_sem)

        def expert_tiles(act):
            for e in range(n_experts):
                begin = ends_ref[e - 1] if e else 0

                @pl.when(ends_ref[e] > begin)
                def _(e=e):
                    act(zero_tile(ends_ref[e] - tm))

        def tail_tiles(act):
            def body(j, carry):
                act(zero_tile(j * tm))
                return carry
            lax.fori_loop(used_ref[0], n_tiles, body, 0)

        expert_tiles(lambda c: c.start())
        tail_tiles(lambda c: c.start())
        expert_tiles(lambda c: c.wait())
        tail_tiles(lambda c: c.wait())

    def copy(s):
        tok = lax.rem(s, tm)
        return pltpu.make_async_copy(h_ref.at[tok], o_ref.at[pos_ref[0, s]], sem)

    def start(pair, carry):
        copy(2 * pair).start(priority=0)
        copy(2 * pair + 1).start(priority=1)
        return carry

    lax.fori_loop(0, TOP_K * tm // 2, start, 0, unroll=4)
    for _ in range(TOP_K):
        pltpu.make_async_copy(h_ref, o_ref.at[pl.ds(0, tm)], sem).wait()


def dispatch_rows(h3, pos_tiles, ends, used, *, n_rows, tm):
    t = h3.shape[0]
    n_experts = ends.shape[0]
    return pl.pallas_call(
        functools.partial(_dispatch_kernel, tm=tm, n_experts=n_experts, n_tiles=n_rows // tm),
        grid_spec=pltpu.PrefetchScalarGridSpec(
            num_scalar_prefetch=2, grid=(t // tm,),
            in_specs=[pl.BlockSpec((None, 1, TOP_K * tm), lambda i, ends, used: (i, 0, 0),
                                   memory_space=pltpu.SMEM),
                      pl.BlockSpec((tm,) + h3.shape[1:], lambda i, ends, used: (i, 0, 0))],
            out_specs=pl.BlockSpec(memory_space=pl.ANY),
            scratch_shapes=[pltpu.VMEM((tm,) + h3.shape[1:], F32), pltpu.SemaphoreType.DMA(()),
                            pltpu.SemaphoreType.DMA(())]),
        out_shape=jax.ShapeDtypeStruct((n_rows,) + h3.shape[1:], F32),
        compiler_params=_params(1),
        name="moe_dispatch",
    )(ends, used, pos_tiles, h3)


SLAB_STRIDE = 4


def _rows_to_matrix_kernel(x_ref, o_ref, stage_ref):
    tm = o_ref.shape[0]
    per = x_ref.shape[0] // SLAB_STRIDE
    for a in range(SLAB_STRIDE):
        stage_ref[a] = x_ref[pl.ds(a, per, stride=SLAB_STRIDE), :]
    for a in range(SLAB_STRIDE):
        for b in range(per // tm):
            c = SLAB_STRIDE * b + a
            o_ref[:, c * LANES:(c + 1) * LANES] = (
                stage_ref[a, pl.ds(b, tm, stride=per // tm), :].astype(o_ref.dtype))


def rows_to_matrix(x3, *, tm):
    p, chunks, _ = x3.shape
    assert chunks % SLAB_STRIDE == 0
    return pl.pallas_call(
        _rows_to_matrix_kernel,
        grid=(p // tm,),
        in_specs=[pl.BlockSpec((tm * chunks, LANES), lambda i: (i, 0))],
        out_specs=pl.BlockSpec((tm, chunks * LANES), lambda i: (i, 0)),
        out_shape=jax.ShapeDtypeStruct((p, chunks * LANES), BF16),
        scratch_shapes=[pltpu.VMEM((SLAB_STRIDE, tm * chunks // SLAB_STRIDE, LANES), F32)],
        compiler_params=_params(1),
        name="moe_rows_to_matrix",
    )(x3.reshape(p * chunks, LANES))


def _combine_kernel(rows_ref, pos_ref, o_ref, route_ref, x_ref, gate_ref, *rest, group_tiles):
    del rows_ref
    buf_ref, sem = rest[-2:]
    tm = x_ref.shape[0]

    def copy(r):
        return pltpu.make_async_copy(o_ref.at[pl.ds(pos_ref[0, r], 1)],
                                     buf_ref.at[pl.ds(r, 1)], sem)

    def start(pair, carry):
        copy(2 * pair).start(priority=0)
        copy(2 * pair + 1).start(priority=1)
        return carry

    lax.fori_loop(0, TOP_K * tm // 2, start, 0, unroll=4)
    pltpu.make_async_copy(o_ref.at[pl.ds(0, TOP_K * tm)], buf_ref, sem).wait()
    route = route_ref[...]
    y = route[:, 2:3] * buf_ref[pl.ds(0, tm), :] + route[:, 3:4] * buf_ref[pl.ds(tm, tm), :]
    res = x_ref[...] + gate_ref[...] * y
    if group_tiles is None:
        rest[0][...] = res
        return
    normed = _rms(res, rest[0][...])
    i = pl.program_id(0)
    tile0 = 0
    for out_ref, tiles in zip(rest[1:-2], group_tiles):
        @pl.when(jnp.logical_and(i >= tile0, i < tile0 + tiles))
        def _(out_ref=out_ref):
            out_ref[...] = normed

        tile0 += tiles


def moe_combine(o_sorted, pos, route, x, mod4, gate_part, rows, *, tm, final=None):
    t, d = x.shape
    in_specs = [pl.BlockSpec((None, 1, TOP_K * tm), lambda i, rows: (i, 0, 0), memory_space=pltpu.SMEM),
                pl.BlockSpec(memory_space=pl.ANY),
                pl.BlockSpec((tm, LANES), lambda i, rows: (i, 0)),
                pl.BlockSpec((tm, d), lambda i, rows: (i, 0)),
                pl.BlockSpec((None, None, 1, d), lambda i, rows: (rows[i], gate_part, 0, 0))]
    args = [rows, pos, o_sorted, route, x, mod4]
    if final is None:
        group_tiles = None
        out_specs = pl.BlockSpec((tm, d), lambda i, rows: (i, 0))
        out_shape = jax.ShapeDtypeStruct((t, d), F32)
    else:
        g_final, group_tokens = final
        group_tiles = tuple(n // tm for n in group_tokens)
        in_specs.append(pl.BlockSpec((1, d), lambda i, rows: (0, 0)))
        args.append(g_final)
        out_specs, out_shape, tile0 = [], [], 0
        for tiles in group_tiles:
            out_specs.append(pl.BlockSpec(
                (tm, d), lambda i, rows, tile0=tile0, tiles=tiles: (jnp.clip(i - tile0, 0, tiles - 1), 0)))
            out_shape.append(jax.ShapeDtypeStruct((tiles * tm, d), F32))
            tile0 += tiles
    return pl.pallas_call(
        functools.partial(_combine_kernel, group_tiles=group_tiles),
        grid_spec=pltpu.PrefetchScalarGridSpec(
            num_scalar_prefetch=1, grid=(t // tm,), in_specs=in_specs, out_specs=out_specs,
            scratch_shapes=[pltpu.VMEM((TOP_K * tm, d), F32), pltpu.SemaphoreType.DMA(())]),
        out_shape=out_shape,
        compiler_params=_params(1),
        name="moe_combine",
    )(*args)


def moe_dispatch_plan(route, counts, tm):
    t = route.shape[0]
    n_experts = counts.shape[0]
    p = t * TOP_K + n_experts * tm
    expert = route[:, :TOP_K].astype(jnp.int32)
    rank = route[:, 4:4 + TOP_K].astype(jnp.int32)
    padded = ((counts + tm - 1) // tm) * tm
    ends = jnp.cumsum(padded).astype(jnp.int32)
    starts = ends - padded
    is_expert = expert[:, :, None] == jnp.arange(n_experts)[None, None, :]
    pos = jnp.sum(jnp.where(is_expert, starts[None, None, :], 0), axis=-1) + rank
    tile_start = jnp.arange(p // tm, dtype=jnp.int32) * tm
    tile_expert = jnp.sum((tile_start[:, None] >= ends[None, :]).astype(jnp.int32), axis=1)
    tile_expert = jnp.minimum(tile_expert, n_experts - 1)
    used = (ends[-1:] // tm).astype(jnp.int32)
    valid = jnp.clip((starts + counts)[tile_expert] - tile_start, 0, tm).astype(jnp.int32)
    return p, pos, tile_expert, used, ends, valid


def _final_norm_kernel(x_ref, g_ref, o_ref):
    o_ref[...] = _rms(x_ref[...], g_ref[...])


def final_norm(x, g, *, row0, n_rows, tm=512):
    d = x.shape[1]
    blk0 = row0 // tm
    return pl.pallas_call(
        _final_norm_kernel,
        grid=(n_rows // tm,),
        in_specs=[pl.BlockSpec((tm, d), lambda i: (blk0 + i, 0)),
                  pl.BlockSpec((1, d), lambda i: (0, 0))],
        out_specs=pl.BlockSpec((tm, d), lambda i: (i, 0)),
        out_shape=jax.ShapeDtypeStruct((n_rows, d), F32),
        compiler_params=_params(1),
        name="final_norm",
    )(x, g)


TM = 512
TM_BIG = 1024
TM_COMBINE = 256


def _tile_rows(groups, tm):
    rows = []
    base = 0
    for n_rows, seq in groups:
        for b in range(n_rows):
            rows += [base + b] * (seq // tm)
        base += n_rows
    return jnp.asarray(np.asarray(rows, np.int32))


def kernel(x_prompt, x_sample, c_prompt, c_sample, w_ada, b_ada, g_norm_mix, g_norm_ff, w_in, w_fmix,
           g_out_f, g_out_a, w_out, w_ff1, w_ff3, w_ff2, w_router, w_e1, w_e3, w_e2, g_final):
    depth, d, d_in = w_in.shape
    groups = [(x_prompt.shape[0], x_prompt.shape[1]), (x_sample.shape[0], x_sample.shape[1])]
    group_row0 = [0, groups[0][0] * groups[0][1]]
    t = sum(b * s for b, s in groups)
    n_req = sum(b for b, _ in groups)
    d_attn = d - D_FOURIER
    n_heads = d_attn // HEAD_DIM
    n_experts = w_router.shape[-1]
    d_ff_e = w_e1.shape[-1]
    dilations = [dil for _, dil in DILATED_BRANCHES]

    x = None
    c = jnp.concatenate([c_prompt, c_sample], axis=0)
    c = jnp.pad(c, ((0, -n_req % 8), (0, 0)))
    mod = ada_modulation(c, w_ada, b_ada)
    rows = _tile_rows(groups, TM)
    rows_c = _tile_rows(groups, TM_COMBINE)
    def dense_tiling(tm):
        tiles = t // tm
        return Tiling(jnp.zeros((tiles,), jnp.int32), _tile_rows(groups, tm),
                      jnp.full((1,), tiles, jnp.int32), jnp.full((tiles,), tm, jnp.int32))

    dense, dense_big = dense_tiling(TM), dense_tiling(TM_BIG)
    slopes = jnp.exp2(-8.0 * (jnp.arange(n_heads, dtype=F32) + 1.0) / n_heads)
    ab = fourier_ab(w_fmix)
    dft = [dft_matrix(seq) for _, seq in groups]

    for l in range(depth):
        mod4 = mod[l].reshape(mod.shape[1], 6, 1, d)
        layer = lambda e, l=l: l
        if x is None:
            h, x = norm_modulate_groups([x_prompt.reshape(-1, d), x_sample.reshape(-1, d)], rows,
                                        g_norm_mix[l:l + 1], mod4, 0, tm=TM)
        else:
            h = norm_modulate(x, rows, g_norm_mix[l:l + 1], mod4, 0, tm=TM)
        u = matmul(h, w_in, dense_big, w_index=layer, tm=TM_BIG, tn=D_FOURIER, out_dtype=BF16,
                   n=D_FOURIER)
        qkvs = matmul_qkv(h, w_in, dense_big, w_index=layer, tm=TM_BIG, tn=512, col0=D_FOURIER,
                          n=3 * d_attn, dilations=dilations, q_cols=d_attn,
                          q_scale=HEAD_DIM ** -0.5 * LOG2_E)

        group_inputs = []
        for gi, ((n_rows, seq), row0) in enumerate(zip(groups, group_row0)):
            pq = fourier_channel_stage(u, ab, l, n_rows=n_rows, seq=seq, row_block0=row0 // TM, tm=TM)
            f = fourier_sequence_stage(dft[gi], pq, n_rows=n_rows, seq=seq, tm=TM)
            branches = [attention_branch(qkv, slopes, window=window, dilation=dilation, n_rows=n_rows,
                                         seq=seq, row0=row0, n_heads=n_heads)
                        for qkv, (window, dilation) in zip(qkvs, DILATED_BRANCHES)]
            group_inputs.append((f, [b[0] for b in branches], [b[1] for b in branches]))
        y = merge_and_norm(group_inputs, g_out_f[l:l + 1], g_out_a[l:l + 1], n_heads=n_heads)
        x = matmul_gated_residual(y, w_out, x, mod4, 2, dense_big, w_index=layer, tm=TM_BIG, tn=1024)

        j = l // 2
        if l % 2 == 0:
            h = norm_modulate(x, rows, g_norm_ff[l:l + 1], mod4, 3, tm=TM)
            act = matmul_swiglu(h, w_ff1, w_ff3, dense_big, w_index=lambda e, j=j: j, tm=TM_BIG, tn=512)
            x = matmul_gated_residual(act, w_ff2, x, mod4, 5, dense, w_index=lambda e, j=j: j,
                                      tm=TM, tn=512)
        else:
            h, route, counts = norm_modulate(x, rows, g_norm_ff[l:l + 1], mod4, 3, out_dtype=F32,
                                             w_router=w_router[j], tm=TM)
            n_sorted, pos, tile_expert, used, ends, valid = moe_dispatch_plan(route, counts, TM)
            routed = Tiling(tile_expert, tile_expert, used, valid)

            def choice_major(tm):
                tiles = pos.reshape(t // tm, tm, TOP_K).transpose(0, 2, 1)
                return tiles.reshape(t // tm, 1, TOP_K * tm)

            xs = rows_to_matrix(dispatch_rows(h, choice_major(TM), ends, used, n_rows=n_sorted, tm=TM),
                                tm=TM)
            expert_w = lambda e, j=j: j * n_experts + e
            act = matmul_swiglu(xs, w_e1.reshape(-1, d, d_ff_e), w_e3.reshape(-1, d, d_ff_e), routed,
                                w_index=expert_w, tm=TM, tn=1024)
            o_sorted = matmul(act, w_e2.reshape(-1, d_ff_e, d), routed, w_index=expert_w,
                              tm=TM, tn=512, out_dtype=F32, vmem=VMEM_LIMIT_BIG)
            final = (g_final.reshape(1, d), [b * s for b, s in groups]) if l == depth - 1 else None
            x = moe_combine(o_sorted, choice_major(TM_COMBINE), route, x, mod4, 5, rows_c, tm=TM_COMBINE,
                            final=final)

    if isinstance(x, (list, tuple)):
        return tuple(xg.reshape(b, s, d) for xg, (b, s) in zip(x, groups))
    g = g_final.reshape(1, d)
    outs = []
    for (n_rows, seq), row0 in zip(groups, group_row0):
        outs.append(final_norm(x, g, row0=row0, n_rows=n_rows * seq, tm=TM).reshape(n_rows, seq, d))
    return tuple(outs)
```

```python
import functools
import math

import numpy as np
import jax
import jax.numpy as jnp
from jax import lax
from jax.experimental import pallas as pl
from jax.experimental.pallas import tpu as pltpu

HEAD_DIM = 128
FOURIER_GROUP = 128
N_FOURIER_GROUPS = 4
D_FOURIER = FOURIER_GROUP * N_FOURIER_GROUPS
DILATED_BRANCHES = ((128, 1), (512, 4), (2048, 16))
TOP_K = 2
EPS = 1e-6
NEG_INF = -1e30
LOG2_E = math.log2(math.e)
LANES = 128
VMEM_LIMIT = 56 * 1024 * 1024
VMEM_LIMIT_BIG = 62 * 1024 * 1024
ATTN_BLOCK_BYTES = 4 * 1024 * 1024

BF16 = jnp.bfloat16
F32 = jnp.float32


def _params(n_axes, vmem=VMEM_LIMIT):
    return pltpu.CompilerParams(
        dimension_semantics=("arbitrary",) * n_axes, vmem_limit_bytes=vmem)


def _ada_kernel(c_ref, w_ref, b_ref, o_ref):
    c = c_ref[...]
    a = (c * jax.nn.sigmoid(c)).astype(BF16)
    acc = jnp.dot(a, w_ref[...].astype(BF16), preferred_element_type=F32)
    o_ref[...] = acc + b_ref[...]


def ada_modulation(c, w_ada, b_ada, tn=1024):
    n_layers, d, n = w_ada.shape
    r8 = c.shape[0]
    return pl.pallas_call(
        _ada_kernel,
        grid=(n_layers, n // tn),
        in_specs=[
            pl.BlockSpec((r8, d), lambda l, j: (0, 0)),
            pl.BlockSpec((None, d, tn), lambda l, j: (l, 0, j)),
            pl.BlockSpec((None, 1, tn), lambda l, j: (l, 0, j)),
        ],
        out_specs=pl.BlockSpec((None, r8, tn), lambda l, j: (l, 0, j)),
        out_shape=jax.ShapeDtypeStruct((n_layers, r8, n), F32),
        compiler_params=_params(2),
        name="ada_modulation",
    )(c, w_ada, b_ada.reshape(n_layers, 1, n))


def _norm_mod(x, g, sc, sh):
    y = x * lax.rsqrt(jnp.mean(x * x, axis=-1, keepdims=True) + EPS)
    return (y * g) * (1.0 + sc) + sh


def _norm_mod_kernel(rows_ref, x_ref, g_ref, sc_ref, sh_ref, h_ref):
    del rows_ref
    h_ref[...] = _norm_mod(x_ref[...], g_ref[...], sc_ref[...], sh_ref[...]).astype(h_ref.dtype)


def _norm_mod_router_kernel(rows_ref, x_ref, g_ref, sc_ref, sh_ref, wr_ref, h_ref, route_ref,
                            totals_ref, stage_ref, counts_ref, earlier_ref, *, n_experts):
    del rows_ref
    h = _norm_mod(x_ref[...], g_ref[...], sc_ref[...], sh_ref[...])
    tm = h.shape[0]
    per = h_ref.shape[0] // SLAB_STRIDE
    for a in range(SLAB_STRIDE):
        for b in range(per // tm):
            c = SLAB_STRIDE * b + a
            stage_ref[a, pl.ds(b, tm, stride=per // tm), :] = h[:, c * LANES:(c + 1) * LANES]
    for a in range(SLAB_STRIDE):
        h_ref[pl.ds(a, per, stride=SLAB_STRIDE), :] = stage_ref[a]
    w = wr_ref[...]
    h_hi, w_hi = h.astype(BF16), w.astype(BF16)
    h_lo = (h - h_hi.astype(F32)).astype(BF16)
    w_lo = (w - w_hi.astype(F32)).astype(BF16)
    logits = (jnp.dot(h_hi, w_hi, preferred_element_type=F32)
              + (jnp.dot(h_hi, w_lo, preferred_element_type=F32)
                 + jnp.dot(h_lo, w_hi, preferred_element_type=F32)))
    lane = lax.broadcasted_iota(jnp.int32, logits.shape, 1)
    logits = jnp.where(lane < n_experts, logits, -jnp.inf)
    m1 = jnp.max(logits, axis=-1, keepdims=True)
    i1 = jnp.min(jnp.where(logits == m1, lane, LANES), axis=-1, keepdims=True)
    rest = jnp.where(lane == i1, -jnp.inf, logits)
    m2 = jnp.max(rest, axis=-1, keepdims=True)
    i2 = jnp.min(jnp.where(rest == m2, lane, LANES), axis=-1, keepdims=True)
    e2 = jnp.exp(m2 - m1)
    den = 1.0 + e2
    @pl.when(pl.program_id(0) == 0)
    def _():
        counts_ref[...] = jnp.zeros_like(counts_ref)
        earlier_ref[...] = (lax.broadcasted_iota(jnp.int32, earlier_ref.shape, 1)
                            < lax.broadcasted_iota(jnp.int32, earlier_ref.shape, 0)).astype(BF16)

    pick1, pick2 = lane == i1, lane == i2
    picked = jnp.logical_or(pick1, pick2)
    before = counts_ref[0:1, :] + jnp.dot(earlier_ref[...], picked.astype(BF16),
                                          preferred_element_type=F32)
    rank1 = jnp.sum(jnp.where(pick1, before, 0.0), axis=-1, keepdims=True)
    rank2 = jnp.sum(jnp.where(pick2, before, 0.0), axis=-1, keepdims=True)
    counts_ref[...] = counts_ref[...] + jnp.sum(picked.astype(F32), axis=0, keepdims=True)
    totals_ref[...] = counts_ref[...]

    route = jnp.where(lane == 0, i1.astype(F32), 0.0)
    route = jnp.where(lane == 1, i2.astype(F32), route)
    route = jnp.where(lane == 2, 1.0 / den, route)
    route = jnp.where(lane == 3, e2 / den, route)
    route = jnp.where(lane == 4, rank1, route)
    route = jnp.where(lane == 5, rank2, route)
    route_ref[...] = route


def _norm_mod_groups_kernel(rows_ref, *refs, group_tiles):
    del rows_ref
    n_groups = len(group_tiles)
    g_ref, sc_ref, sh_ref, h_ref, x_ref = refs[n_groups:]
    i = pl.program_id(0)
    tile0 = 0
    for xg_ref, tiles in zip(refs[:n_groups], group_tiles):
        @pl.when(jnp.logical_and(i >= tile0, i < tile0 + tiles))
        def _(xg_ref=xg_ref):
            x = xg_ref[...]
            x_ref[...] = x
            h_ref[...] = _norm_mod(x, g_ref[...], sc_ref[...], sh_ref[...]).astype(h_ref.dtype)

        tile0 += tiles


def norm_modulate_groups(x_groups, rows, g, mod4, layer_part, *, tm=512):
    d = x_groups[0].shape[1]
    group_tiles = tuple(xg.shape[0] // tm for xg in x_groups)
    n_tiles = sum(group_tiles)
    sh_part, sc_part = layer_part, layer_part + 1
    in_specs, tile0 = [], 0
    for tiles in group_tiles:
        in_specs.append(pl.BlockSpec(
            (tm, d), lambda i, rows, tile0=tile0, tiles=tiles: (jnp.clip(i - tile0, 0, tiles - 1), 0)))
        tile0 += tiles
    in_specs += [pl.BlockSpec((1, d), lambda i, rows: (0, 0)),
                 pl.BlockSpec((None, None, 1, d), lambda i, rows: (rows[i], sc_part, 0, 0)),
                 pl.BlockSpec((None, None, 1, d), lambda i, rows: (rows[i], sh_part, 0, 0))]
    row = lambda i, rows: (i, 0)
    return pl.pallas_call(
        functools.partial(_norm_mod_groups_kernel, group_tiles=group_tiles),
        grid_spec=pltpu.PrefetchScalarGridSpec(
            num_scalar_prefetch=1, grid=(n_tiles,), in_specs=in_specs,
            out_specs=[pl.BlockSpec((tm, d), row), pl.BlockSpec((tm, d), row)]),
        out_shape=[jax.ShapeDtypeStruct((n_tiles * tm, d), BF16),
                   jax.ShapeDtypeStruct((n_tiles * tm, d), F32)],
        compiler_params=_params(1),
        name="norm_modulate_groups",
    )(rows, *x_groups, g, mod4, mod4)


def norm_modulate(x, rows, g, mod4, layer_part, *, out_dtype=BF16, w_router=None, tm=512):
    t, d = x.shape
    sh_part, sc_part = layer_part, layer_part + 1
    in_specs = [
        pl.BlockSpec((tm, d), lambda i, rows: (i, 0)),
        pl.BlockSpec((1, d), lambda i, rows: (0, 0)),
        pl.BlockSpec((None, None, 1, d), lambda i, rows: (rows[i], sc_part, 0, 0)),
        pl.BlockSpec((None, None, 1, d), lambda i, rows: (rows[i], sh_part, 0, 0)),
    ]
    h_spec = pl.BlockSpec((tm, d), lambda i, rows: (i, 0))
    h_shape = jax.ShapeDtypeStruct((t, d), out_dtype)
    if w_router is None:
        return pl.pallas_call(
            _norm_mod_kernel,
            grid_spec=pltpu.PrefetchScalarGridSpec(
                num_scalar_prefetch=1, grid=(t // tm,), in_specs=in_specs, out_specs=h_spec),
            out_shape=h_shape,
            compiler_params=_params(1),
            name="norm_modulate",
        )(rows, x, g, mod4, mod4)
    n_experts = w_router.shape[1]
    wr = jnp.pad(w_router, ((0, 0), (0, LANES - n_experts)))
    in_specs.append(pl.BlockSpec((d, LANES), lambda i, rows: (0, 0)))
    chunks = d // LANES
    assert chunks % SLAB_STRIDE == 0
    h_flat, route, totals = pl.pallas_call(
        functools.partial(_norm_mod_router_kernel, n_experts=n_experts),
        grid_spec=pltpu.PrefetchScalarGridSpec(
            num_scalar_prefetch=1, grid=(t // tm,), in_specs=in_specs,
            out_specs=[pl.BlockSpec((tm * chunks, LANES), lambda i, rows: (i, 0)),
                       pl.BlockSpec((tm, LANES), lambda i, rows: (i, 0)),
                       pl.BlockSpec((8, LANES), lambda i, rows: (0, 0))],
            scratch_shapes=[pltpu.VMEM((SLAB_STRIDE, tm * chunks // SLAB_STRIDE, LANES), F32),
                            pltpu.VMEM((8, LANES), F32), pltpu.VMEM((tm, tm), BF16)]),
        out_shape=[jax.ShapeDtypeStruct((t * chunks, LANES), F32),
                   jax.ShapeDtypeStruct((t, LANES), F32),
                   jax.ShapeDtypeStruct((8, LANES), F32)],
        compiler_params=_params(1),
        name="norm_modulate_router",
    )(rows, x, g, mod4, mod4, wr)
    return h_flat.reshape(t, chunks, LANES), route, totals[0, :n_experts].astype(jnp.int32)


CAST_ROWS = 512


def _cast_weights_if_changed(te_ref, w_refs, wb_refs):
    i = pl.program_id(1)
    changed = jnp.logical_or(i == 0, te_ref[i] != te_ref[jnp.maximum(i - 1, 0)])

    @pl.when(changed)
    def _():
        k = w_refs[0].shape[0]
        step = math.gcd(k, CAST_ROWS)

        def body(c, carry):
            r0 = pl.multiple_of(c * step, step)
            for w_ref, wb_ref in zip(w_refs, wb_refs):
                wb_ref[pl.ds(r0, step), :] = w_ref[pl.ds(r0, step), :].astype(BF16)
            return carry

        lax.fori_loop(0, k // step, body, 0)


ROW_PARTS = 4


def _for_valid_rows(valid_ref, o_ref, compute):
    tm = o_ref.shape[0]
    quantum = tm // ROW_PARTS
    quanta = (valid_ref[pl.program_id(1)] + quantum - 1) // quantum
    for n in range(ROW_PARTS + 1):
        rows = n * quantum

        @pl.when(quanta == n)
        def _(rows=rows):
            if rows:
                o_ref[:rows, :] = compute(rows).astype(o_ref.dtype)
            if rows < tm:
                o_ref[rows:, :] = jnp.zeros((tm - rows, o_ref.shape[1]), o_ref.dtype)


def _mm_plain_kernel(te_ref, rows_ref, used_ref, valid_ref, a_ref, w_ref, o_ref, wb_ref):
    del rows_ref, used_ref
    _cast_weights_if_changed(te_ref, (w_ref,), (wb_ref,))
    _for_valid_rows(valid_ref, o_ref, lambda rows: jnp.dot(
        a_ref[:rows, :], wb_ref[...], preferred_element_type=F32))


def _mm_qkv_kernel(te_ref, rows_ref, used_ref, valid_ref, a_ref, w_ref, *rest, dilations, q_chunks,
                   q_scale):
    del rows_ref, used_ref, valid_ref
    out_refs = rest[:len(dilations)]
    wb_ref, acc_ref, stage_ref = rest[len(dilations):]
    _cast_weights_if_changed(te_ref, (w_ref,), (wb_ref,))
    acc = jnp.dot(a_ref[...], wb_ref[...], preferred_element_type=F32)
    acc = acc * jnp.where(pl.program_id(0) < q_chunks, q_scale, 1.0).astype(F32)
    tm, tn = acc.shape
    slabs = tn // LANES
    for c in range(slabs):
        acc_ref[c] = acc[:, c * LANES:(c + 1) * LANES]
    cur_ref, cur_d = acc_ref, 1
    for idx, (o_ref, d) in enumerate(zip(out_refs, dilations)):
        if d == 1:
            o_ref[0] = acc.astype(o_ref.dtype)
            continue
        f = d // cur_d
        keep = idx + 1 < len(dilations)
        for rp in range(cur_d):
            for b in range(f):
                r = b * cur_d + rp
                for c in range(slabs):
                    piece = cur_ref[c, pl.ds(rp * (tm // cur_d) + b, tm // d, stride=f), :]
                    o_ref[r, :, c * LANES:(c + 1) * LANES] = piece.astype(o_ref.dtype)
                    if keep:
                        stage_ref[c, pl.ds(r * (tm // d), tm // d), :] = piece
        cur_ref, cur_d = stage_ref, d


def _mm_swiglu_kernel(te_ref, rows_ref, used_ref, valid_ref, a_ref, w1_ref, w3_ref, o_ref, wb1_ref,
                      wb3_ref):
    del rows_ref, used_ref
    _cast_weights_if_changed(te_ref, (w1_ref, w3_ref), (wb1_ref, wb3_ref))

    def compute(rows):
        a = a_ref[:rows, :]
        g = jnp.dot(a, wb1_ref[...], preferred_element_type=F32)
        u = jnp.dot(a, wb3_ref[...], preferred_element_type=F32)
        return (g * jax.nn.sigmoid(g)) * u

    _for_valid_rows(valid_ref, o_ref, compute)


def _mm_resid_kernel(te_ref, rows_ref, used_ref, valid_ref, a_ref, w_ref, x_ref, gate_ref, o_ref,
                     wb_ref):
    del rows_ref, used_ref, valid_ref
    _cast_weights_if_changed(te_ref, (w_ref,), (wb_ref,))
    acc = jnp.dot(a_ref[...], wb_ref[...], preferred_element_type=F32)
    o_ref[...] = x_ref[...] + gate_ref[...] * acc


def _a_spec(tm, k):
    return pl.BlockSpec((tm, k), lambda j, i, te, rows, used, valid: (jnp.minimum(i, used[0] - 1), 0))


def _w_spec(k, tn, w_index, col_block0=0):
    return pl.BlockSpec((None, k, tn),
                        lambda j, i, te, rows, used, valid: (w_index(te[i]), 0, col_block0 + j))


def _out_spec(tm, tn):
    return pl.BlockSpec((tm, tn), lambda j, i, te, rows, used, valid: (i, j))


class Tiling:
    def __init__(self, tile_expert, rows, used, valid):
        self.args = (tile_expert, rows, used, valid)


def matmul_qkv(a, w, tiling, *, w_index, tm, tn, col0, n, dilations, q_cols, q_scale):
    m, k = a.shape
    assert sum(d > 1 for d in dilations) <= 2
    return pl.pallas_call(
        functools.partial(_mm_qkv_kernel, dilations=tuple(dilations), q_chunks=q_cols // tn,
                          q_scale=q_scale),
        grid_spec=pltpu.PrefetchScalarGridSpec(
            num_scalar_prefetch=4, grid=(n // tn, m // tm),
            in_specs=[_a_spec(tm, k), _w_spec(k, tn, w_index, col0 // tn)],
            out_specs=[pl.BlockSpec((d, tm // d, tn), lambda j, i, te, rows, used, valid: (0, i, j))
                       for d in dilations],
            scratch_shapes=[pltpu.VMEM((k, tn), BF16)]
                           + [pltpu.VMEM((tn // LANES, tm, LANES), F32)] * 2),
        out_shape=[jax.ShapeDtypeStruct((d, m // d, n), BF16) for d in dilations],
        compiler_params=_params(2),
        name="matmul_qkv",
    )(*tiling.args, a, w)


def matmul(a, w, tiling, *, w_index, tm, tn, out_dtype, n=None, vmem=VMEM_LIMIT):
    m, k = a.shape
    n = w.shape[-1] if n is None else n
    return pl.pallas_call(
        _mm_plain_kernel,
        grid_spec=pltpu.PrefetchScalarGridSpec(
            num_scalar_prefetch=4, grid=(n // tn, m // tm),
            in_specs=[_a_spec(tm, k), _w_spec(k, tn, w_index)],
            out_specs=_out_spec(tm, tn),
            scratch_shapes=[pltpu.VMEM((k, tn), BF16)]),
        out_shape=jax.ShapeDtypeStruct((m, n), out_dtype),
        compiler_params=_params(2, vmem),
        name="matmul",
    )(*tiling.args, a, w)


def matmul_swiglu(a, w1, w3, tiling, *, w_index, tm, tn):
    m, k = a.shape
    n = w1.shape[-1]
    return pl.pallas_call(
        _mm_swiglu_kernel,
        grid_spec=pltpu.PrefetchScalarGridSpec(
            num_scalar_prefetch=4, grid=(n // tn, m // tm),
            in_specs=[_a_spec(tm, k), _w_spec(k, tn, w_index), _w_spec(k, tn, w_index)],
            out_specs=_out_spec(tm, tn),
            scratch_shapes=[pltpu.VMEM((k, tn), BF16), pltpu.VMEM((k, tn), BF16)]),
        out_shape=jax.ShapeDtypeStruct((m, n), BF16),
        compiler_params=_params(2),
        name="matmul_swiglu",
    )(*tiling.args, a, w1, w3)


def matmul_gated_residual(a, w, x, mod4, gate_part, tiling, *, w_index, tm, tn):
    m, k = a.shape
    n = w.shape[-1]
    return pl.pallas_call(
        _mm_resid_kernel,
        grid_spec=pltpu.PrefetchScalarGridSpec(
            num_scalar_prefetch=4, grid=(n // tn, m // tm),
            in_specs=[_a_spec(tm, k), _w_spec(k, tn, w_index), _out_spec(tm, tn),
                      pl.BlockSpec((None, None, 1, tn),
                                   lambda j, i, te, rows, used, valid: (rows[i], gate_part, 0, j))],
            out_specs=_out_spec(tm, tn),
            scratch_shapes=[pltpu.VMEM((k, tn), BF16)]),
        out_shape=jax.ShapeDtypeStruct((m, n), F32),
        compiler_params=_params(2),
        name="matmul_gated_residual",
    )(*tiling.args, a, w, x, mod4)


def _dft_cos_sin(n):
    j = lax.broadcasted_iota(jnp.int32, (n, n), 0)
    k = lax.broadcasted_iota(jnp.int32, (n, n), 1)
    ang = ((j * k) % n).astype(F32) * (2.0 * math.pi / n)
    return jnp.cos(ang), jnp.sin(ang)


def _fourier_ab_kernel(c_ref, s_ref, w_ref, o_ref):
    w = w_ref[...]
    scale = FOURIER_GROUP ** -0.5
    a = jnp.dot(c_ref[...], w, preferred_element_type=F32, precision=lax.Precision.HIGHEST)
    b = jnp.dot(s_ref[...], w, preferred_element_type=F32, precision=lax.Precision.HIGHEST)
    o_ref[:, :FOURIER_GROUP] = (a * scale).astype(o_ref.dtype)
    o_ref[:, FOURIER_GROUP:] = (b * scale).astype(o_ref.dtype)


def fourier_ab(w_fmix):
    n_layers, n_groups, c, _ = w_fmix.shape
    cos_g, sin_g = _dft_cos_sin(c)
    return pl.pallas_call(
        _fourier_ab_kernel,
        grid=(n_layers, n_groups),
        in_specs=[pl.BlockSpec((c, c), lambda l, g: (0, 0)),
                  pl.BlockSpec((c, c), lambda l, g: (0, 0)),
                  pl.BlockSpec((None, None, c, c), lambda l, g: (l, g, 0, 0))],
        out_specs=pl.BlockSpec((None, None, c, 2 * c), lambda l, g: (l, g, 0, 0)),
        out_shape=jax.ShapeDtypeStruct((n_layers, n_groups, c, 2 * c), BF16),
        compiler_params=_params(2),
        name="fourier_ab",
    )(cos_g, sin_g, w_fmix)


def _fourier_channel_kernel(u_ref, ab_ref, pq_ref):
    c = FOURIER_GROUP
    for g in range(N_FOURIER_GROUPS):
        pq = jnp.dot(u_ref[:, g * c:(g + 1) * c], ab_ref[g], preferred_element_type=F32)
        pq_ref[0, :, g * c:(g + 1) * c] = pq[:, :c].astype(pq_ref.dtype)
        pq_ref[1, :, g * c:(g + 1) * c] = pq[:, c:].astype(pq_ref.dtype)


def fourier_channel_stage(u, ab, layer, *, n_rows, seq, row_block0, tm=512):
    tiles = seq // tm
    return pl.pallas_call(
        _fourier_channel_kernel,
        grid=(n_rows, tiles),
        in_specs=[pl.BlockSpec((tm, D_FOURIER), lambda b, i: (row_block0 + b * tiles + i, 0)),
                  pl.BlockSpec((None, N_FOURIER_GROUPS, FOURIER_GROUP, 2 * FOURIER_GROUP),
                               lambda b, i: (layer, 0, 0, 0))],
        out_specs=pl.BlockSpec((None, 2, tm, D_FOURIER), lambda b, i: (b, 0, i, 0)),
        out_shape=jax.ShapeDtypeStruct((n_rows, 2, seq, D_FOURIER), BF16),
        compiler_params=_params(2),
        name="fourier_channel_stage",
    )(u, ab)


def _fourier_seq_kernel(cs_ref, pq_ref, o_ref, *, scale):
    o_ref[...] = jnp.dot(cs_ref[...], pq_ref[...], preferred_element_type=F32) * scale


def fourier_sequence_stage(cs, pq, *, n_rows, seq, tm=512):
    tiles = seq // tm
    return pl.pallas_call(
        functools.partial(_fourier_seq_kernel, scale=seq ** -0.5),
        grid=(n_rows, tiles),
        in_specs=[pl.BlockSpec((tm, 2 * seq), lambda b, i: (i, 0)),
                  pl.BlockSpec((None, 2 * seq, D_FOURIER), lambda b, i: (b, 0, 0))],
        out_specs=pl.BlockSpec((tm, D_FOURIER), lambda b, i: (b * tiles + i, 0)),
        out_shape=jax.ShapeDtypeStruct((n_rows * seq, D_FOURIER), F32),
        compiler_params=_params(2),
        name="fourier_sequence_stage",
    )(cs, pq.reshape(n_rows, 2 * seq, D_FOURIER))


DFT_SPLIT = 64


def _dft_kernel(t1_ref, t2_ref, o_ref):
    seq = t2_ref.shape[2]
    c1, s1 = t1_ref[0], t1_ref[1]
    c2, s2 = t2_ref[0], t2_ref[1]
    o_ref[:, :seq] = (c1 * c2 - s1 * s2).astype(o_ref.dtype)
    o_ref[:, seq:] = (-(s1 * c2 + c1 * s2)).astype(o_ref.dtype)


def dft_matrix(seq):
    k = jnp.arange(seq, dtype=jnp.int32)

    def table(mult):
        ang = ((mult[:, None] * k[None, :]) % seq).astype(F32) * (2.0 * math.pi / seq)
        return jnp.stack([jnp.cos(ang), jnp.sin(ang)])

    n_coarse = seq // DFT_SPLIT
    t1 = table(jnp.arange(n_coarse, dtype=jnp.int32) * DFT_SPLIT).reshape(2, n_coarse, 1, seq)
    t2 = table(jnp.arange(DFT_SPLIT, dtype=jnp.int32))
    return pl.pallas_call(
        _dft_kernel,
        grid=(n_coarse,),
        in_specs=[pl.BlockSpec((2, None, 1, seq), lambda a: (0, a, 0, 0)),
                  pl.BlockSpec((2, DFT_SPLIT, seq), lambda a: (0, 0, 0))],
        out_specs=pl.BlockSpec((DFT_SPLIT, 2 * seq), lambda a: (a, 0)),
        out_shape=jax.ShapeDtypeStruct((seq, 2 * seq), BF16),
        compiler_params=_params(1),
        name="dft_matrix",
    )(t1, t2)


def _attn_kernel(slopes_ref, q_ref, k_ref, v_ref, o_ref, lse_ref, bias_ref, *, dilation, n_side,
                 bq, bk, hps, stack, unroll):
    n_res, sub = q_ref.shape[:2]
    n_cases = bias_ref.shape[0]
    blocks = sub // bq if stack == 1 else 1
    lane = lax.broadcasted_iota(jnp.int32, (bq, LANES), 1)

    @pl.when(jnp.logical_and(pl.program_id(1) == 0, pl.program_id(2) == 0))
    def _():
        row = lax.broadcasted_iota(jnp.int32, (bq, bk), 0)
        col = lax.broadcasted_iota(jnp.int32, (bq, bk), 1)
        for case in range(n_cases):
            dist = jnp.abs(col - row - case * n_side)
            valid = dist <= n_side
            if stack > 1:
                valid = jnp.logical_and(valid, row // sub == col // sub)
            dist_f = dist.astype(F32) * float(dilation)
            for hh in range(hps):
                slope = slopes_ref[pl.program_id(0) * hps + hh] * LOG2_E
                bias_ref[case, hh] = jnp.where(valid, -slope * dist_f, NEG_INF)

    def body(it, carry):
        if stack == 1:
            res = it // blocks
            q0 = pl.multiple_of((it % blocks) * bq, bq)
            ks = pl.multiple_of(jnp.clip(q0 - n_side, 0, sub - bk), n_side)
            case = (q0 - ks) // n_side
            q_at = lambda ref, cols: ref[res, pl.ds(q0, bq), cols]
            k_at = lambda ref, cols: ref[res, pl.ds(ks, bk), cols]
        else:
            case = 0
            res = pl.ds(pl.multiple_of(it * stack, stack), stack)
            q_at = k_at = lambda ref, cols: ref[res, :, cols].reshape(stack * sub, HEAD_DIM)
        lse_tile = jnp.zeros((bq, LANES), F32)
        for hh in range(hps):
            cols = slice(hh * HEAD_DIM, (hh + 1) * HEAD_DIM)
            q = q_at(q_ref, cols)
            k = k_at(k_ref, cols)
            v = k_at(v_ref, cols)
            s = lax.dot_general(q, k, (((1,), (1,)), ((), ())), preferred_element_type=F32)
            s = s + bias_ref[case, hh]
            m = jnp.max(s, axis=-1, keepdims=True)
            p = jnp.exp2(s - m)
            den = jnp.sum(p, axis=-1, keepdims=True)
            num = jnp.dot(p.astype(BF16), v, preferred_element_type=F32)
            o = (num / den).astype(o_ref.dtype)
            if stack == 1:
                o_ref[res, pl.ds(q0, bq), cols] = o
            else:
                o_ref[res, :, cols] = o.reshape(stack, sub, HEAD_DIM)
            lse_tile = jnp.where(lane == pl.program_id(0) * hps + hh, m + jnp.log2(den), lse_tile)
        if stack == 1:
            lse_ref[res, pl.ds(q0, bq), :] = lse_tile
        else:
            lse_ref[res] = lse_tile.reshape(stack, sub, LANES)
        return carry

    lax.fori_loop(0, n_res * blocks // stack, body, 0, unroll=unroll)


ATTN_CHAINS = 12
ATTN_KEY_BLOCK = 256


def attention_step_shape(sub, n_heads, dilation, rows_per_trip):
    def largest(n, unit):
        return max(c for c in range(1, n + 1) if n % c == 0 and c * unit <= ATTN_BLOCK_BYTES)

    hps = largest(n_heads, sub * HEAD_DIM * 2)
    n_res = largest(dilation, sub * hps * HEAD_DIM * 2)
    trips = n_res * sub // rows_per_trip
    unroll = max(u for u in range(1, trips + 1) if trips % u == 0 and u * hps <= max(ATTN_CHAINS, hps))
    return hps, n_res, unroll


def attention_branch(qkv, slopes, *, window, dilation, n_rows, seq, row0, n_heads):
    tg = n_rows * seq
    sub = seq // dilation
    n_side = (window // 2) // dilation
    if sub >= ATTN_KEY_BLOCK:
        stack, bq, bk, n_cases = 1, ATTN_KEY_BLOCK - 2 * n_side, ATTN_KEY_BLOCK, 3
    else:
        stack = ATTN_KEY_BLOCK // sub
        bq, bk, n_cases = ATTN_KEY_BLOCK, ATTN_KEY_BLOCK, 1
    hps, n_res, unroll = attention_step_shape(sub, n_heads, dilation, bq)
    assert n_res % stack == 0 and sub % (bq // stack) == 0
    gw = hps * HEAD_DIM
    groups = n_heads // hps
    d_attn = n_heads * HEAD_DIM
    rb0 = row0 // seq

    def in_map(part):
        return lambda g, b, r, slopes: (r, rb0 + b, part * groups + g)

    out_map = lambda g, b, r, slopes: (r, b, g)
    o, lse = pl.pallas_call(
        functools.partial(_attn_kernel, dilation=dilation, n_side=n_side, bq=bq, bk=bk, hps=hps,
                          stack=stack, unroll=unroll),
        grid_spec=pltpu.PrefetchScalarGridSpec(
            num_scalar_prefetch=1, grid=(groups, n_rows, dilation // n_res),
            in_specs=[pl.BlockSpec((n_res, sub, gw), in_map(part)) for part in range(3)],
            out_specs=[pl.BlockSpec((n_res, sub, gw), out_map),
                       pl.BlockSpec((n_res, sub, LANES), out_map)],
            scratch_shapes=[pltpu.VMEM((n_cases, hps, bq, bk), F32)]),
        out_shape=[jax.ShapeDtypeStruct((dilation, tg // dilation, d_attn), BF16),
                   jax.ShapeDtypeStruct((dilation, tg // dilation, groups * LANES), F32)],
        compiler_params=_params(3),
        name=f"attention_d{dilation}",
    )(slopes, qkv, qkv, qkv)
    return o, lse


def _rms(x, g):
    return x * lax.rsqrt(jnp.mean(x * x, axis=-1, keepdims=True) + EPS) * g


def _merge_tile(f_ref, o_refs, l_refs, gf_ref, ga_ref, y_ref, ya_ref, on_ref, ln_ref, tmp_ref, n_heads):
    tm = y_ref.shape[0]
    lses = []
    for bi, (o_ref, l_ref) in enumerate(zip(o_refs, l_refs)):
        d = o_ref.shape[0]
        blocks = l_ref.shape[2] // LANES
        if d == 1:
            parts = [l_ref[0, :, g * LANES:(g + 1) * LANES] for g in range(blocks)]
        else:
            def to_positions(src, dst, tmp):
                if d <= SLAB_STRIDE:
                    for r in range(d):
                        dst[pl.ds(r, tm // d, stride=d), :] = src(r)
                    return
                f = d // SLAB_STRIDE
                for a in range(SLAB_STRIDE):
                    for b in range(f):
                        tmp[a, pl.ds(b, tm // d, stride=f), :] = src(SLAB_STRIDE * b + a)
                    dst[pl.ds(a, tm // SLAB_STRIDE, stride=SLAB_STRIDE), :] = tmp[a]

            for h in range(n_heads):
                cols = slice(h * HEAD_DIM, (h + 1) * HEAD_DIM)
                to_positions(lambda r: o_ref[r, :, cols].astype(F32), on_ref.at[bi, h], tmp_ref.at[h])
            for g in range(blocks):
                cols = slice(g * LANES, (g + 1) * LANES)
                to_positions(lambda r: l_ref[r, :, cols], ln_ref.at[bi, g], tmp_ref.at[n_heads + g])
            parts = [ln_ref[bi, g] for g in range(blocks)]
        lses.append(functools.reduce(lambda a, b: a + b, parts))

    top = functools.reduce(jnp.maximum, lses)
    ws = [jnp.exp2(l - top) for l in lses]
    inv = 1.0 / functools.reduce(lambda a, b: a + b, ws)
    ws = [w * inv for w in ws]
    for h in range(n_heads):
        cols = slice(h * HEAD_DIM, (h + 1) * HEAD_DIM)
        acc = None
        for bi, o_ref in enumerate(o_refs):
            o = o_ref[0, :, cols].astype(F32) if o_ref.shape[0] == 1 else on_ref[bi, h]
            term = o * ws[bi][:, h:h + 1]
            acc = term if acc is None else acc + term
        ya_ref[:, cols] = acc
    y_ref[:, :D_FOURIER] = _rms(f_ref[...], gf_ref[...]).astype(y_ref.dtype)
    y_ref[:, D_FOURIER:] = _rms(ya_ref[...], ga_ref[...]).astype(y_ref.dtype)


def _merge_kernel(*refs, n_heads, group_tiles):
    n_br = len(DILATED_BRANCHES)
    per_group = 1 + 2 * n_br
    n_groups = len(group_tiles)
    gf_ref, ga_ref, y_ref, ya_ref, on_ref, ln_ref, tmp_ref = refs[n_groups * per_group:]
    i = pl.program_id(0)
    tile0 = 0
    for gi, tiles in enumerate(group_tiles):
        grp = refs[gi * per_group:(gi + 1) * per_group]

        @pl.when(jnp.logical_and(i >= tile0, i < tile0 + tiles))
        def _(grp=grp):
            _merge_tile(grp[0], grp[1:1 + n_br], grp[1 + n_br:], gf_ref, ga_ref, y_ref, ya_ref,
                        on_ref, ln_ref, tmp_ref, n_heads)

        tile0 += tiles


def merge_and_norm(group_inputs, g_out_f, g_out_a, *, n_heads, tm=256):
    d_attn = n_heads * HEAD_DIM
    group_tiles = [g[0].shape[0] // tm for g in group_inputs]
    n_tiles = sum(group_tiles)
    n_br = len(DILATED_BRANCHES)
    in_specs, args = [], []
    tile0 = 0
    for (f, os_, ls_), tiles in zip(group_inputs, group_tiles):
        local = lambda i, tile0=tile0, tiles=tiles: jnp.clip(i - tile0, 0, tiles - 1)
        in_specs.append(pl.BlockSpec((tm, D_FOURIER), lambda i, local=local: (local(i), 0)))
        for arr in (*os_, *ls_):
            d = arr.shape[0]
            in_specs.append(pl.BlockSpec((d, tm // d, arr.shape[2]),
                                         lambda i, local=local: (0, local(i), 0)))
        args += [f, *os_, *ls_]
        tile0 += tiles
    fixed = lambda i: (0, 0)
    in_specs += [pl.BlockSpec((1, D_FOURIER), fixed), pl.BlockSpec((1, d_attn), fixed)]
    max_groups = max(l.shape[2] // LANES for g in group_inputs for l in g[2])
    return pl.pallas_call(
        functools.partial(_merge_kernel, n_heads=n_heads, group_tiles=tuple(group_tiles)),
        grid=(n_tiles,),
        in_specs=in_specs,
        out_specs=pl.BlockSpec((tm, D_FOURIER + d_attn), lambda i: (i, 0)),
        out_shape=jax.ShapeDtypeStruct((n_tiles * tm, D_FOURIER + d_attn), BF16),
        scratch_shapes=[pltpu.VMEM((tm, d_attn), F32),
                        pltpu.VMEM((n_br, n_heads, tm, HEAD_DIM), F32),
                        pltpu.VMEM((n_br, max_groups, tm, LANES), F32),
                        pltpu.VMEM((n_heads + max_groups, SLAB_STRIDE, tm // SLAB_STRIDE, LANES), F32)],
        compiler_params=_params(1),
        name="merge_and_norm",
    )(*args, g_out_f, g_out_a)


def _dispatch_kernel(ends_ref, used_ref, pos_ref, h_ref, o_ref, zero_ref, sem, zero_sem,
                     *, tm, n_experts, n_tiles):
    i = pl.program_id(0)

    @pl.when(i == 0)
    def _():
        zero_ref[...] = jnp.zeros_like(zero_ref)

        def zero_tile(start):
            return pltpu.make_async_copy(zero_ref, o_ref.at[pl.ds(start, tm)], zero_sem)

        def expert_tiles(act):
            for e in range(n_experts):
                begin = ends_ref[e - 1] if e else 0

                @pl.when(ends_ref[e] > begin)
                def _(e=e):
                    act(zero_tile(ends_ref[e] - tm))

        def tail_tiles(act):
            def body(j, carry):
                act(zero_tile(j * tm))
                return carry
            lax.fori_loop(used_ref[0], n_tiles, body, 0)

        expert_tiles(lambda c: c.start())
        tail_tiles(lambda c: c.start())
        expert_tiles(lambda c: c.wait())
        tail_tiles(lambda c: c.wait())

    def copy(s):
        tok = lax.rem(s, tm)
        return pltpu.make_async_copy(h_ref.at[tok], o_ref.at[pos_ref[0, s]], sem)

    def start(s, carry):
        copy(s).start()
        return carry

    lax.fori_loop(0, TOP_K * tm, start, 0, unroll=8)
    for _ in range(TOP_K):
        pltpu.make_async_copy(h_ref, o_ref.at[pl.ds(0, tm)], sem).wait()


def dispatch_rows(h3, pos_tiles, ends, used, *, n_rows, tm):
    t = h3.shape[0]
    n_experts = ends.shape[0]
    return pl.pallas_call(
        functools.partial(_dispatch_kernel, tm=tm, n_experts=n_experts, n_tiles=n_rows // tm),
        grid_spec=pltpu.PrefetchScalarGridSpec(
            num_scalar_prefetch=2, grid=(t // tm,),
            in_specs=[pl.BlockSpec((None, 1, TOP_K * tm), lambda i, ends, used: (i, 0, 0),
                                   memory_space=pltpu.SMEM),
                      pl.BlockSpec((tm,) + h3.shape[1:], lambda i, ends, used: (i, 0, 0))],
            out_specs=pl.BlockSpec(memory_space=pl.ANY),
            scratch_shapes=[pltpu.VMEM((tm,) + h3.shape[1:], F32), pltpu.SemaphoreType.DMA(()),
                            pltpu.SemaphoreType.DMA(())]),
        out_shape=jax.ShapeDtypeStruct((n_rows,) + h3.shape[1:], F32),
        compiler_params=_params(1),
        name="moe_dispatch",
    )(ends, used, pos_tiles, h3)


SLAB_STRIDE = 4


def _rows_to_matrix_kernel(x_ref, o_ref, stage_ref):
    tm = o_ref.shape[0]
    per = x_ref.shape[0] // SLAB_STRIDE
    for a in range(SLAB_STRIDE):
        stage_ref[a] = x_ref[pl.ds(a, per, stride=SLAB_STRIDE), :]
    for a in range(SLAB_STRIDE):
        for b in range(per // tm):
            c = SLAB_STRIDE * b + a
            o_ref[:, c * LANES:(c + 1) * LANES] = (
                stage_ref[a, pl.ds(b, tm, stride=per // tm), :].astype(o_ref.dtype))


def rows_to_matrix(x3, *, tm):
    p, chunks, _ = x3.shape
    assert chunks % SLAB_STRIDE == 0
    return pl.pallas_call(
        _rows_to_matrix_kernel,
        grid=(p // tm,),
        in_specs=[pl.BlockSpec((tm * chunks, LANES), lambda i: (i, 0))],
        out_specs=pl.BlockSpec((tm, chunks * LANES), lambda i: (i, 0)),
        out_shape=jax.ShapeDtypeStruct((p, chunks * LANES), BF16),
        scratch_shapes=[pltpu.VMEM((SLAB_STRIDE, tm * chunks // SLAB_STRIDE, LANES), F32)],
        compiler_params=_params(1),
        name="moe_rows_to_matrix",
    )(x3.reshape(p * chunks, LANES))


def _combine_kernel(rows_ref, pos_ref, o_ref, route_ref, x_ref, gate_ref, *rest, group_tiles):
    del rows_ref
    buf_ref, sem = rest[-2:]
    tm = x_ref.shape[0]

    def copy(r):
        return pltpu.make_async_copy(o_ref.at[pl.ds(pos_ref[0, r], 1)],
                                     buf_ref.at[pl.ds(r, 1)], sem)

    def start(r, carry):
        copy(r).start()
        return carry

    lax.fori_loop(0, TOP_K * tm, start, 0, unroll=8)
    pltpu.make_async_copy(o_ref.at[pl.ds(0, TOP_K * tm)], buf_ref, sem).wait()
    route = route_ref[...]
    y = route[:, 2:3] * buf_ref[pl.ds(0, tm), :] + route[:, 3:4] * buf_ref[pl.ds(tm, tm), :]
    res = x_ref[...] + gate_ref[...] * y
    if group_tiles is None:
        rest[0][...] = res
        return
    normed = _rms(res, rest[0][...])
    i = pl.program_id(0)
    tile0 = 0
    for out_ref, tiles in zip(rest[1:-2], group_tiles):
        @pl.when(jnp.logical_and(i >= tile0, i < tile0 + tiles))
        def _(out_ref=out_ref):
            out_ref[...] = normed

        tile0 += tiles


def moe_combine(o_sorted, pos, route, x, mod4, gate_part, rows, *, tm, final=None):
    t, d = x.shape
    in_specs = [pl.BlockSpec((None, 1, TOP_K * tm), lambda i, rows: (i, 0, 0), memory_space=pltpu.SMEM),
                pl.BlockSpec(memory_space=pl.ANY),
                pl.BlockSpec((tm, LANES), lambda i, rows: (i, 0)),
                pl.BlockSpec((tm, d), lambda i, rows: (i, 0)),
                pl.BlockSpec((None, None, 1, d), lambda i, rows: (rows[i], gate_part, 0, 0))]
    args = [rows, pos, o_sorted, route, x, mod4]
    if final is None:
        group_tiles = None
        out_specs = pl.BlockSpec((tm, d), lambda i, rows: (i, 0))
        out_shape = jax.ShapeDtypeStruct((t, d), F32)
    else:
        g_final, group_tokens = final
        group_tiles = tuple(n // tm for n in group_tokens)
        in_specs.append(pl.BlockSpec((1, d), lambda i, rows: (0, 0)))
        args.append(g_final)
        out_specs, out_shape, tile0 = [], [], 0
        for tiles in group_tiles:
            out_specs.append(pl.BlockSpec(
                (tm, d), lambda i, rows, tile0=tile0, tiles=tiles: (jnp.clip(i - tile0, 0, tiles - 1), 0)))
            out_shape.append(jax.ShapeDtypeStruct((tiles * tm, d), F32))
            tile0 += tiles
    return pl.pallas_call(
        functools.partial(_combine_kernel, group_tiles=group_tiles),
        grid_spec=pltpu.PrefetchScalarGridSpec(
            num_scalar_prefetch=1, grid=(t // tm,), in_specs=in_specs, out_specs=out_specs,
            scratch_shapes=[pltpu.VMEM((TOP_K * tm, d), F32), pltpu.SemaphoreType.DMA(())]),
        out_shape=out_shape,
        compiler_params=_params(1),
        name="moe_combine",
    )(*args)


def moe_dispatch_plan(route, counts, tm):
    t = route.shape[0]
    n_experts = counts.shape[0]
    p = t * TOP_K + n_experts * tm
    expert = route[:, :TOP_K].astype(jnp.int32)
    rank = route[:, 4:4 + TOP_K].astype(jnp.int32)
    padded = ((counts + tm - 1) // tm) * tm
    ends = jnp.cumsum(padded).astype(jnp.int32)
    starts = ends - padded
    is_expert = expert[:, :, None] == jnp.arange(n_experts)[None, None, :]
    pos = jnp.sum(jnp.where(is_expert, starts[None, None, :], 0), axis=-1) + rank
    tile_start = jnp.arange(p // tm, dtype=jnp.int32) * tm
    tile_expert = jnp.sum((tile_start[:, None] >= ends[None, :]).astype(jnp.int32), axis=1)
    tile_expert = jnp.minimum(tile_expert, n_experts - 1)
    used = (ends[-1:] // tm).astype(jnp.int32)
    valid = jnp.clip((starts + counts)[tile_expert] - tile_start, 0, tm).astype(jnp.int32)
    return p, pos, tile_expert, used, ends, valid


def _final_norm_kernel(x_ref, g_ref, o_ref):
    o_ref[...] = _rms(x_ref[...], g_ref[...])


def final_norm(x, g, *, row0, n_rows, tm=512):
    d = x.shape[1]
    blk0 = row0 // tm
    return pl.pallas_call(
        _final_norm_kernel,
        grid=(n_rows // tm,),
        in_specs=[pl.BlockSpec((tm, d), lambda i: (blk0 + i, 0)),
                  pl.BlockSpec((1, d), lambda i: (0, 0))],
        out_specs=pl.BlockSpec((tm, d), lambda i: (i, 0)),
        out_shape=jax.ShapeDtypeStruct((n_rows, d), F32),
        compiler_params=_params(1),
        name="final_norm",
    )(x, g)


TM = 512
TM_BIG = 1024
TM_COMBINE = 256


def _tile_rows(groups, tm):
    rows = []
    base = 0
    for n_rows, seq in groups:
        for b in range(n_rows):
            rows += [base + b] * (seq // tm)
        base += n_rows
    return jnp.asarray(np.asarray(rows, np.int32))


def kernel(x_prompt, x_sample, c_prompt, c_sample, w_ada, b_ada, g_norm_mix, g_norm_ff, w_in, w_fmix,
           g_out_f, g_out_a, w_out, w_ff1, w_ff3, w_ff2, w_router, w_e1, w_e3, w_e2, g_final):
    depth, d, d_in = w_in.shape
    groups = [(x_prompt.shape[0], x_prompt.shape[1]), (x_sample.shape[0], x_sample.shape[1])]
    group_row0 = [0, groups[0][0] * groups[0][1]]
    t = sum(b * s for b, s in groups)
    n_req = sum(b for b, _ in groups)
    d_attn = d - D_FOURIER
    n_heads = d_attn // HEAD_DIM
    n_experts = w_router.shape[-1]
    d_ff_e = w_e1.shape[-1]
    dilations = [dil for _, dil in DILATED_BRANCHES]

    x = None
    c = jnp.concatenate([c_prompt, c_sample], axis=0)
    c = jnp.pad(c, ((0, -n_req % 8), (0, 0)))
    mod = ada_modulation(c, w_ada, b_ada)
    rows = _tile_rows(groups, TM)
    rows_c = _tile_rows(groups, TM_COMBINE)
    def dense_tiling(tm):
        tiles = t // tm
        return Tiling(jnp.zeros((tiles,), jnp.int32), _tile_rows(groups, tm),
                      jnp.full((1,), tiles, jnp.int32), jnp.full((tiles,), tm, jnp.int32))

    dense, dense_big = dense_tiling(TM), dense_tiling(TM_BIG)
    slopes = jnp.exp2(-8.0 * (jnp.arange(n_heads, dtype=F32) + 1.0) / n_heads)
    ab = fourier_ab(w_fmix)
    dft = [dft_matrix(seq) for _, seq in groups]

    for l in range(depth):
        mod4 = mod[l].reshape(mod.shape[1], 6, 1, d)
        layer = lambda e, l=l: l
        if x is None:
            h, x = norm_modulate_groups([x_prompt.reshape(-1, d), x_sample.reshape(-1, d)], rows,
                                        g_norm_mix[l:l + 1], mod4, 0, tm=TM)
        else:
            h = norm_modulate(x, rows, g_norm_mix[l:l + 1], mod4, 0, tm=TM)
        u = matmul(h, w_in, dense_big, w_index=layer, tm=TM_BIG, tn=D_FOURIER, out_dtype=BF16,
                   n=D_FOURIER)
        qkvs = matmul_qkv(h, w_in, dense_big, w_index=layer, tm=TM_BIG, tn=512, col0=D_FOURIER,
                          n=3 * d_attn, dilations=dilations, q_cols=d_attn,
                          q_scale=HEAD_DIM ** -0.5 * LOG2_E)

        group_inputs = []
        for gi, ((n_rows, seq), row0) in enumerate(zip(groups, group_row0)):
            pq = fourier_channel_stage(u, ab, l, n_rows=n_rows, seq=seq, row_block0=row0 // TM, tm=TM)
            f = fourier_sequence_stage(dft[gi], pq, n_rows=n_rows, seq=seq, tm=TM)
            branches = [attention_branch(qkv, slopes, window=window, dilation=dilation, n_rows=n_rows,
                                         seq=seq, row0=row0, n_heads=n_heads)
                        for qkv, (window, dilation) in zip(qkvs, DILATED_BRANCHES)]
            group_inputs.append((f, [b[0] for b in branches], [b[1] for b in branches]))
        y = merge_and_norm(group_inputs, g_out_f[l:l + 1], g_out_a[l:l + 1], n_heads=n_heads)
        x = matmul_gated_residual(y, w_out, x, mod4, 2, dense_big, w_index=layer, tm=TM_BIG, tn=1024)

        j = l // 2
        if l % 2 == 0:
            h = norm_modulate(x, rows, g_norm_ff[l:l + 1], mod4, 3, tm=TM)
            act = matmul_swiglu(h, w_ff1, w_ff3, dense_big, w_index=lambda e, j=j: j, tm=TM_BIG, tn=512)
            x = matmul_gated_residual(act, w_ff2, x, mod4, 5, dense, w_index=lambda e, j=j: j,
                                      tm=TM, tn=512)
        else:
            h, route, counts = norm_modulate(x, rows, g_norm_ff[l:l + 1], mod4, 3, out_dtype=F32,
                                             w_router=w_router[j], tm=TM)
            n_sorted, pos, tile_expert, used, ends, valid = moe_dispatch_plan(route, counts, TM)
            routed = Tiling(tile_expert, tile_expert, used, valid)

            def choice_major(tm):
                tiles = pos.reshape(t // tm, tm, TOP_K).transpose(0, 2, 1)
                return tiles.reshape(t // tm, 1, TOP_K * tm)

            xs = rows_to_matrix(dispatch_rows(h, choice_major(TM), ends, used, n_rows=n_sorted, tm=TM),
                                tm=TM)
            expert_w = lambda e, j=j: j * n_experts + e
            act = matmul_swiglu(xs, w_e1.reshape(-1, d, d_ff_e), w_e3.reshape(-1, d, d_ff_e), routed,
                                w_index=expert_w, tm=TM, tn=1024)
            o_sorted = matmul(act, w_e2.reshape(-1, d_ff_e, d), routed, w_index=expert_w,
                              tm=TM, tn=512, out_dtype=F32, vmem=VMEM_LIMIT_BIG)
            final = (g_final.reshape(1, d), [b * s for b, s in groups]) if l == depth - 1 else None
            x = moe_combine(o_sorted, choice_major(TM_COMBINE), route, x, mod4, 5, rows_c, tm=TM_COMBINE,
                            final=final)

    if isinstance(x, (list, tuple)):
        return tuple(xg.reshape(b, s, d) for xg, (b, s) in zip(x, groups))
    g = g_final.reshape(1, d)
    outs = []
    for (n_rows, seq), row0 in zip(groups, group_row0):
        outs.append(final_norm(x, g, row0=row0, n_rows=n_rows * seq, tm=TM).reshape(n_rows, seq, d))
    return tuple(outs)
```
